```python
import math
import jax, jax.numpy as jnp
from jax import lax
import numpy as np

D_MODEL = 1024
BATCH = 4
SEQ = 8192
DEPTH = 2

N_MIXERS = 2
N_CONV_LAYERS = (DEPTH + 1) // 2
N_ATTN_LAYERS = DEPTH // 2
D_FF = 2816
CONV_WIDTH = 3
HEAD_DIM = 64
N_HEADS = D_MODEL // HEAD_DIM
DILATION_GROUPS = ((128, 1), (512, 4), (2048, 16))
N_GROUPS = len(DILATION_GROUPS)
BLOCK = 128
N_BUCKETS = 32
MAX_EXACT = N_BUCKETS // 2
MAX_DISTANCE = 2048
RMS_EPS = 1e-6
NEG_INF = -1e30

kernel_name = "hybrid_shortconv_dilated_swa_macaron"


def rmsnorm(x, g):
    xf = x.astype(jnp.float32)
    y = xf * lax.rsqrt(jnp.mean(xf * xf, axis=-1, keepdims=True) + RMS_EPS) * g.astype(jnp.float32)
    return y.astype(x.dtype)


def swiglu(h, w_in, w_out):
    gate, up = jnp.split(h @ w_in, 2, axis=-1)
    return (jax.nn.silu(gate) * up) @ w_out


def short_conv_mixer(h, w_in, w_conv, w_out):
    b_gate, c_gate, u = jnp.split(h @ w_in, 3, axis=-1)
    v = c_gate * u
    T = v.shape[1]
    vp = jnp.pad(v, ((0, 0), (CONV_WIDTH - 1, 0), (0, 0)))
    conv = sum(w_conv[lag] * vp[:, CONV_WIDTH - 1 - lag:CONV_WIDTH - 1 - lag + T]
               for lag in range(CONV_WIDTH))
    return (b_gate * conv) @ w_out


def t5_bucket(dist):
    is_small = dist < MAX_EXACT
    nf = jnp.maximum(dist, 1).astype(jnp.float32)
    large = MAX_EXACT + (jnp.log(nf / MAX_EXACT) / math.log(MAX_DISTANCE / MAX_EXACT)
                         * (N_BUCKETS - MAX_EXACT)).astype(jnp.int32)
    large = jnp.minimum(large, N_BUCKETS - 1)
    return jnp.where(is_small, dist, large)


def band_bias(rel_bias_g, dilation, n_steps):
    i = jnp.arange(BLOCK)[:, None]
    j = jnp.arange(2 * BLOCK)[None, :]
    step = BLOCK + i - j
    in_band = (step >= 0) & (step <= n_steps)
    bucket = t5_bucket(jnp.clip(step, 0, n_steps) * dilation)
    bias = rel_bias_g[bucket]
    return jnp.transpose(bias, (2, 0, 1)).astype(jnp.float32), in_band


def dilated_group(q, k, v, bias, in_band, dilation):
    B, T, H, Dh = q.shape
    span = dilation * BLOCK
    Tp = -(-T // span) * span
    L = Tp // dilation
    Lb = L // BLOCK

    def to_sub(a):
        a = jnp.pad(a, ((0, 0), (0, Tp - T), (0, 0), (0, 0))).reshape(B, L, dilation, H, Dh)
        return jnp.transpose(a, (0, 2, 1, 3, 4)).reshape(B, dilation, Lb, BLOCK, H, Dh)

    def with_prev(a):
        prev = jnp.pad(a, ((0, 0), (0, 0), (1, 0), (0, 0), (0, 0), (0, 0)))[:, :, :-1]
        return jnp.concatenate([prev, a], axis=3)

    qs = to_sub(q)
    kb = with_prev(to_sub(k))
    vb = with_prev(to_sub(v))
    logits = jnp.einsum('bsnqhd,bsnkhd->bsnhqk', qs, kb) + bias
    n_idx = jnp.arange(Lb)[:, None, None, None]
    j_idx = jnp.arange(2 * BLOCK)[None, None, None, :]
    valid = in_band[None, None] & ~((n_idx == 0) & (j_idx < BLOCK))
    logits = jnp.where(valid, logits, NEG_INF)
    m = jnp.max(logits, axis=-1, keepdims=True)
    p = jnp.exp(logits - m)
    s = jnp.sum(p, axis=-1, keepdims=True)
    o = jnp.einsum('bsnhqk,bsnkhd->bsnqhd', p, vb)
    o = o / jnp.transpose(s, (0, 1, 2, 4, 3, 5))
    lse = jnp.transpose((m + jnp.log(s))[..., 0], (0, 1, 2, 4, 3))

    def from_sub(a):
        rest = a.shape[4:]
        a = a.reshape((B, dilation, L) + rest)
        a = jnp.moveaxis(a, 1, 2).reshape((B, Tp) + rest)
        return a[:, :T]

    return from_sub(o), from_sub(lse)


def dilated_attention_mixer(h, w_qkv, q_gain, k_gain, w_out, rel_bias):
    B, T, _ = h.shape
    qkv = (h @ w_qkv).reshape(B, T, N_GROUPS, 3, N_HEADS, HEAD_DIM)
    outs, lses = [], []
    for g, (window, dilation) in enumerate(DILATION_GROUPS):
        q = rmsnorm(qkv[:, :, g, 0].astype(jnp.float32), q_gain[g]) * (HEAD_DIM ** -0.5)
        k = rmsnorm(qkv[:, :, g, 1].astype(jnp.float32), k_gain[g])
        v = qkv[:, :, g, 2].astype(jnp.float32)
        bias, in_band = band_bias(rel_bias[:, g * N_HEADS:(g + 1) * N_HEADS], dilation, window // dilation)
        o, lse = dilated_group(q, k, v, bias, in_band, dilation)
        outs.append(o)
        lses.append(lse)
    wts = jax.nn.softmax(jnp.stack(lses), axis=0)
    o = jnp.einsum('gbth,gbthd->bthd', wts, jnp.stack(outs))
    return o.reshape(B, T, N_HEADS * HEAD_DIM).astype(h.dtype) @ w_out


def setup_inputs(seed: int = 0) -> dict:
    key = jax.random.key(seed)
    ks = jax.random.split(key, 16)
    f32 = jnp.float32

    def w(k, shape, fan_in):
        return jax.random.normal(k, shape, f32) * (fan_in ** -0.5)

    def gain(k, shape):
        return 1.0 + 0.05 * jax.random.normal(k, shape, f32)

    return {
        "x": jax.random.normal(ks[0], (BATCH, SEQ, D_MODEL), f32),
        "norm_ffn1": gain(ks[1], (DEPTH, D_MODEL)),
        "ffn1_w_in": w(ks[2], (DEPTH, D_MODEL, 2 * D_FF), D_MODEL),
        "ffn1_w_out": w(ks[3], (DEPTH, D_FF, D_MODEL), D_FF),
        "norm_mix": gain(ks[4], (DEPTH, D_MODEL)),
        "conv_w_in": w(ks[5], (N_CONV_LAYERS, D_MODEL, 3 * D_MODEL), D_MODEL),
        "conv_w": w(ks[6], (N_CONV_LAYERS, CONV_WIDTH, D_MODEL), CONV_WIDTH),
        "conv_w_out": w(ks[7], (N_CONV_LAYERS, D_MODEL, D_MODEL), D_MODEL),
        "attn_w_qkv": w(ks[8], (N_ATTN_LAYERS, D_MODEL, N_GROUPS * 3 * N_HEADS * HEAD_DIM), D_MODEL),
        "attn_q_gain": gain(ks[9], (N_ATTN_LAYERS, N_GROUPS, HEAD_DIM)),
        "attn_k_gain": gain(ks[10], (N_ATTN_LAYERS, N_GROUPS, HEAD_DIM)),
        "attn_w_out": w(ks[11], (N_ATTN_LAYERS, N_HEADS * HEAD_DIM, D_MODEL), N_HEADS * HEAD_DIM),
        "rel_bias": 0.5 * jax.random.normal(ks[12], (N_BUCKETS, N_GROUPS * N_HEADS), f32),
        "norm_ffn2": gain(ks[13], (DEPTH, D_MODEL)),
        "ffn2_w_in": w(ks[14], (DEPTH, D_MODEL, 2 * D_FF), D_MODEL),
        "ffn2_w_out": w(ks[15], (DEPTH, D_FF, D_MODEL), D_FF),
    }


def reference(x, norm_ffn1, ffn1_w_in, ffn1_w_out, norm_mix, conv_w_in, conv_w, conv_w_out,
              attn_w_qkv, attn_q_gain, attn_k_gain, attn_w_out, rel_bias,
              norm_ffn2, ffn2_w_in, ffn2_w_out):
    h = x
    for i in range(DEPTH):
        h = h + 0.5 * swiglu(rmsnorm(h, norm_ffn1[i]), ffn1_w_in[i], ffn1_w_out[i])
        hn = rmsnorm(h, norm_mix[i])
        j = i // N_MIXERS
        if i % N_MIXERS == 0:
            mix = short_conv_mixer(hn, conv_w_in[j], conv_w[j], conv_w_out[j])
        else:
            mix = dilated_attention_mixer(hn, attn_w_qkv[j], attn_q_gain[j], attn_k_gain[j],
                                          attn_w_out[j], rel_bias)
        h = h + mix
        h = h + 0.5 * swiglu(rmsnorm(h, norm_ffn2[i]), ffn2_w_in[i], ffn2_w_out[i])
    return h
```

```python
import functools
import math

import numpy as np
import jax
import jax.numpy as jnp
from jax import lax
from jax.experimental import pallas as pl
from jax.experimental.pallas import tpu as pltpu

D_MODEL = 1024
D_FF = 2816
CONV_WIDTH = 3
HEAD_DIM = 64
N_HEADS = D_MODEL // HEAD_DIM
N_PAIRS = N_HEADS // 2
DILATION_GROUPS = ((128, 1), (512, 4), (2048, 16))
N_GROUPS = len(DILATION_GROUPS)
BLOCK = 128
N_BUCKETS = 32
MAX_EXACT = N_BUCKETS // 2
MAX_DISTANCE = 2048
RMS_EPS = 1e-6
NEG_INF = -1e30

LANES = 128
MXU_DIM = 256
VMEM_LIMIT = 56 * 1024 * 1024

F32 = jnp.float32
BF16 = jnp.bfloat16


def _rmsnorm_bf16(x, g):
    ms = jnp.mean(x * x, axis=-1, keepdims=True)
    return (x * lax.rsqrt(ms + RMS_EPS) * g).astype(BF16)


def _dot(a, b):
    return jnp.dot(a, b, preferred_element_type=F32)


def _resident(shape):
    return pl.BlockSpec(shape, lambda *_: (0,) * len(shape), pipeline_mode=pl.Buffered(1))


def _params(*sem):
    return pltpu.CompilerParams(dimension_semantics=sem, vmem_limit_bytes=VMEM_LIMIT)


FFN_TM = 512
FFN_TF = D_FF // 2


def _ffn_kernel(x_ref, g_ref, win_ref, wout_ref, o_ref):
    x = x_ref[...]
    xn = _rmsnorm_bf16(x, g_ref[...])
    acc = None
    for c in range(D_FF // FFN_TF):
        gate = _dot(xn, win_ref[:, pl.ds(c * FFN_TF, FFN_TF)])
        up = _dot(xn, win_ref[:, pl.ds(D_FF + c * FFN_TF, FFN_TF)])
        a = (gate * jax.nn.sigmoid(gate) * up).astype(BF16)
        part = _dot(a, wout_ref[pl.ds(c * FFN_TF, FFN_TF), :])
        acc = part if acc is None else acc + part
    o_ref[...] = x + 0.5 * acc


def _ffn(h, g, w_in, w_out):
    n_tok = h.shape[0]
    return pl.pallas_call(
        _ffn_kernel,
        grid=(n_tok // FFN_TM,),
        in_specs=[
            pl.BlockSpec((FFN_TM, D_MODEL), lambda i: (i, 0)),
            _resident((1, D_MODEL)),
            _resident((D_MODEL, 2 * D_FF)),
            _resident((D_FF, D_MODEL)),
        ],
        out_specs=pl.BlockSpec((FFN_TM, D_MODEL), lambda i: (i, 0)),
        out_shape=jax.ShapeDtypeStruct((n_tok, D_MODEL), F32),
        compiler_params=_params("arbitrary"),
        name="ffn",
    )(h, g.reshape(1, D_MODEL), w_in.astype(BF16), w_out.astype(BF16))


CONV_TM = 512
CARRY = 8


def _conv_kernel(tiles_per_seq, x_ref, g_ref, win_ref, cw_ref, wout_ref, o_ref, vp_ref):
    @pl.when(pl.program_id(0) % tiles_per_seq == 0)
    def _():
        vp_ref[pl.ds(0, CARRY), :] = jnp.zeros((CARRY, D_MODEL), F32)

    x = x_ref[...]
    xn = _rmsnorm_bf16(x, g_ref[...])
    b_gate = _dot(xn, win_ref[:, pl.ds(0, D_MODEL)])
    c_gate = _dot(xn, win_ref[:, pl.ds(D_MODEL, D_MODEL)])
    u = _dot(xn, win_ref[:, pl.ds(2 * D_MODEL, D_MODEL)])
    v = c_gate * u
    vp_ref[pl.ds(CARRY, CONV_TM), :] = v
    conv = cw_ref[pl.ds(0, 1), :] * v
    for lag in range(1, CONV_WIDTH):
        conv = conv + cw_ref[pl.ds(lag, 1), :] * vp_ref[pl.ds(CARRY - lag, CONV_TM), :]
    z = (b_gate * conv).astype(BF16)
    o_ref[...] = x + _dot(z, wout_ref[...])
    vp_ref[pl.ds(0, CARRY), :] = vp_ref[pl.ds(CONV_TM, CARRY), :]


def _conv_mixer(h, seq, g, w_in, w_conv, w_out):
    n_tok = h.shape[0]
    return pl.pallas_call(
        functools.partial(_conv_kernel, seq // CONV_TM),
        grid=(n_tok // CONV_TM,),
        in_specs=[
            pl.BlockSpec((CONV_TM, D_MODEL), lambda i: (i, 0)),
            _resident((1, D_MODEL)),
            _resident((D_MODEL, 3 * D_MODEL)),
            _resident((CONV_WIDTH, D_MODEL)),
            _resident((D_MODEL, D_MODEL)),
        ],
        out_specs=pl.BlockSpec((CONV_TM, D_MODEL), lambda i: (i, 0)),
        out_shape=jax.ShapeDtypeStruct((n_tok, D_MODEL), F32),
        scratch_shapes=[pltpu.VMEM((CARRY + CONV_TM, D_MODEL), F32)],
        compiler_params=_params("arbitrary"),
        name="conv_mixer",
    )(h, g.reshape(1, D_MODEL), w_in.astype(BF16), w_conv, w_out.astype(BF16))


QKV_TM = 1024
N_QKV_TILES = 3 * N_GROUPS


def _qkv_kernel(x_ref, g_ref, w_ref, gain_ref, bd_ref, o_ref, xn_ref):
    j = pl.program_id(1)

    @pl.when(j == 0)
    def _():
        xn_ref[...] = _rmsnorm_bf16(x_ref[...], g_ref[...])

    y = _dot(xn_ref[...], w_ref[...])

    @pl.when(j % 3 == 2)
    def _():
        o_ref[...] = y.astype(BF16)

    @pl.when(j % 3 != 2)
    def _():
        y2 = (y * y).astype(BF16)
        ss = jnp.concatenate(
            [_dot(y2[:, c * MXU_DIM:(c + 1) * MXU_DIM], bd_ref[...]) for c in range(D_MODEL // MXU_DIM)],
            axis=1)
        o_ref[...] = (y * lax.rsqrt(ss * (1.0 / HEAD_DIM) + RMS_EPS) * gain_ref[...]).astype(BF16)


def _qkv_proj(h, g, w_qkv, q_gain, k_gain):
    n_tok = h.shape[0]
    gains = jnp.stack(
        [jnp.tile(q_gain, (1, N_HEADS)) * (HEAD_DIM ** -0.5), jnp.tile(k_gain, (1, N_HEADS)),
         jnp.ones((N_GROUPS, D_MODEL), F32)], axis=1).reshape(N_QKV_TILES, 1, D_MODEL)
    seg = np.arange(MXU_DIM) // HEAD_DIM
    block_diag = jnp.asarray(seg[:, None] == seg[None, :], BF16)
    return pl.pallas_call(
        _qkv_kernel,
        grid=(n_tok // QKV_TM, N_QKV_TILES),
        in_specs=[
            pl.BlockSpec((QKV_TM, D_MODEL), lambda i, j: (i, 0)),
            _resident((1, D_MODEL)),
            pl.BlockSpec((D_MODEL, D_MODEL), lambda i, j: (0, j)),
            pl.BlockSpec((None, 1, D_MODEL), lambda i, j: (j, 0, 0)),
            _resident((MXU_DIM, MXU_DIM)),
        ],
        out_specs=pl.BlockSpec((QKV_TM, D_MODEL), lambda i, j: (i, j)),
        out_shape=jax.ShapeDtypeStruct((n_tok, N_QKV_TILES * D_MODEL), BF16),
        scratch_shapes=[pltpu.VMEM((QKV_TM, D_MODEL), BF16)],
        compiler_params=_params("arbitrary", "arbitrary"),
        name="qkv_proj",
    )(h, g.reshape(1, D_MODEL), w_qkv.astype(BF16), gains, block_diag)


def _t5_bucket_np(dist):
    nf = np.maximum(dist, 1).astype(np.float32)
    large = MAX_EXACT + (np.log(nf / np.float32(MAX_EXACT)) / np.float32(math.log(MAX_DISTANCE / MAX_EXACT))
                         * np.float32(N_BUCKETS - MAX_EXACT)).astype(np.int32)
    large = np.minimum(large, N_BUCKETS - 1)
    return np.where(dist < MAX_EXACT, dist, large)


def _bucket_maps():
    i = (np.arange(2 * BLOCK) % BLOCK)[:, None]
    j = np.arange(2 * BLOCK)[None, :]
    step = BLOCK + i - j
    maps = []
    for window, dilation in DILATION_GROUPS:
        n_steps = window // dilation
        in_band = (step >= 0) & (step <= n_steps)
        bucket = _t5_bucket_np(np.clip(step, 0, n_steps) * dilation)
        maps.append(np.where(in_band, bucket, -1).astype(np.int32))
    return np.stack(maps)


def _bias_kernel(rel_ref, bm_ref, o_ref):
    g = pl.program_id(0)
    p = pl.program_id(1)
    bm = bm_ref[...]
    shape = bm.shape
    top = lax.broadcasted_iota(jnp.int32, shape, 0) < BLOCK
    col = lax.broadcasted_iota(jnp.int32, shape, 1)
    acc = jnp.zeros(shape, F32)
    for b in range(N_BUCKETS):
        val = jnp.where(top, rel_ref[b, g * N_HEADS + 2 * p], rel_ref[b, g * N_HEADS + 2 * p + 1])
        acc = jnp.where(bm == b, val, acc)
    rest = jnp.where(bm < 0, NEG_INF, acc)
    o_ref[1] = rest
    o_ref[0] = jnp.where(col < BLOCK, NEG_INF, rest)


def _bias_tiles(rel_bias):
    tile = (2 * BLOCK, 2 * BLOCK)
    return pl.pallas_call(
        _bias_kernel,
        grid=(N_GROUPS, N_PAIRS),
        in_specs=[
            pl.BlockSpec(memory_space=pltpu.SMEM),
            pl.BlockSpec((None,) + tile, lambda g, p: (g, 0, 0)),
        ],
        out_specs=pl.BlockSpec((None, 2, None) + tile, lambda g, p: (g, 0, p, 0, 0)),
        out_shape=jax.ShapeDtypeStruct((N_GROUPS, 2, N_PAIRS) + tile, F32),
        compiler_params=_params("arbitrary", "arbitrary"),
        name="bias_tiles",
    )(rel_bias, jnp.asarray(_bucket_maps()))


def _attn_kernel(q_ref, kp_ref, kc_ref, vp_ref, vc_ref, bias_ref, o_ref, lse_ref):
    first = jnp.minimum(pl.program_id(2), 1)
    lane = lax.broadcasted_iota(jnp.int32, (BLOCK, LANES), 1)
    lo = lane < HEAD_DIM
    ones = jnp.ones((2 * BLOCK, LANES), BF16)
    lse_all = jnp.zeros((BLOCK, LANES), F32)
    for p in range(N_PAIRS):
        cs = pl.ds(p * LANES, LANES)
        q = q_ref[:, cs]
        zero = jnp.zeros_like(q)
        q2 = jnp.concatenate([jnp.where(lo, q, zero), jnp.where(lo, zero, q)], axis=0)
        k = jnp.concatenate([kp_ref[:, cs], kc_ref[:, cs]], axis=0)
        logits = lax.dot_general(q2, k, (((1,), (1,)), ((), ())), preferred_element_type=F32)
        logits = logits + bias_ref[first, p]
        m = jnp.max(logits, axis=-1, keepdims=True)
        prob = jnp.exp(logits - m).astype(BF16)
        v = jnp.concatenate([vp_ref[:, cs], vc_ref[:, cs]], axis=0)
        acc = _dot(prob, jnp.concatenate([v, ones], axis=1))
        o = jnp.where(lo, acc[:BLOCK, :LANES], acc[BLOCK:, :LANES])
        s = jnp.where(lo, acc[:BLOCK, LANES:], acc[BLOCK:, LANES:])
        mm = jnp.where(lo, m[:BLOCK], m[BLOCK:])
        o_ref[:, cs] = (o / s).astype(BF16)
        lse = mm + jnp.log(s)
        keep = (lane == 2 * p) | (lane == HEAD_DIM + 2 * p + 1)
        lse_all = jnp.where(keep, lse, lse_all)
    lse_ref[...] = lse_all


def _attn_group(qkv, bias, g, batch, seq):
    dilation = DILATION_GROUPS[g][1]
    sub_len = seq // dilation
    n_blocks = sub_len // BLOCK
    qkv_cols = N_QKV_TILES * D_MODEL
    view = qkv.reshape(batch, sub_len, dilation * qkv_cols)

    def col(which):
        return lambda b, s, n: (b, n, s * N_QKV_TILES + 3 * g + which)

    def col_prev(which):
        return lambda b, s, n: (b, jnp.maximum(n - 1, 0), s * N_QKV_TILES + 3 * g + which)

    blk = (None, BLOCK, D_MODEL)
    o, lse = pl.pallas_call(
        _attn_kernel,
        grid=(batch, dilation, n_blocks),
        in_specs=[
            pl.BlockSpec(blk, col(0)),
            pl.BlockSpec(blk, col_prev(1)),
            pl.BlockSpec(blk, col(1)),
            pl.BlockSpec(blk, col_prev(2)),
            pl.BlockSpec(blk, col(2)),
            pl.BlockSpec((None, 2, N_PAIRS, 2 * BLOCK, 2 * BLOCK), lambda b, s, n: (g, 0, 0, 0, 0)),
        ],
        out_specs=[
            pl.BlockSpec(blk, lambda b, s, n: (b, n, s)),
            pl.BlockSpec((None, BLOCK, LANES), lambda b, s, n: (b, n, s)),
        ],
        out_shape=[
            jax.ShapeDtypeStruct((batch, sub_len, dilation * D_MODEL), BF16),
            jax.ShapeDtypeStruct((batch, sub_len, dilation * LANES), F32),
        ],
        compiler_params=_params("arbitrary", "arbitrary", "arbitrary"),
        name=f"attn_r{dilation}",
    )(view, view, view, view, view, bias)
    return o.reshape(batch * seq, D_MODEL), lse.reshape(batch * seq, LANES)


MERGE_TM = 512


def _merge_kernel(h_ref, o0_ref, o1_ref, o2_ref, l0_ref, l1_ref, l2_ref, e_ref, w_ref, out_ref):
    lses = [l0_ref[...], l1_ref[...], l2_ref[...]]
    m = jnp.maximum(jnp.maximum(lses[0], lses[1]), lses[2])
    es = [jnp.exp(l - m) for l in lses]
    den = es[0] + es[1] + es[2]
    merged = None
    for e, o_ref in zip(es, (o0_ref, o1_ref, o2_ref)):
        w = e / den
        w_hi = w.astype(BF16)
        w_lo = (w - w_hi.astype(F32)).astype(BF16)
        w_full = _dot(w_hi, e_ref[...]) + _dot(w_lo, e_ref[...])
        term = w_full * o_ref[...].astype(F32)
        merged = term if merged is None else merged + term
    out_ref[...] = h_ref[...] + _dot(merged.astype(BF16), w_ref[...])


def _merge_proj(h, outs, lses, w_out):
    n_tok = h.shape[0]
    expand = np.zeros((LANES, D_MODEL), np.float32)
    for head in range(N_HEADS):
        src = head if head % 2 == 0 else HEAD_DIM + head
        expand[src, head * HEAD_DIM:(head + 1) * HEAD_DIM] = 1.0
    tok = lambda width: pl.BlockSpec((MERGE_TM, width), lambda i: (i, 0))
    return pl.pallas_call(
        _merge_kernel,
        grid=(n_tok // MERGE_TM,),
        in_specs=[tok(D_MODEL)] + [tok(D_MODEL)] * N_GROUPS + [tok(LANES)] * N_GROUPS
        + [_resident((LANES, D_MODEL)), _resident((D_MODEL, D_MODEL))],
        out_specs=tok(D_MODEL),
        out_shape=jax.ShapeDtypeStruct((n_tok, D_MODEL), F32),
        compiler_params=_params("arbitrary"),
        name="merge_proj",
    )(h, *outs, *lses, jnp.asarray(expand, BF16), w_out.astype(BF16))


def kernel(x, norm_ffn1, ffn1_w_in, ffn1_w_out, norm_mix, conv_w_in, conv_w, conv_w_out, attn_w_qkv,
           attn_q_gain, attn_k_gain, attn_w_out, rel_bias, norm_ffn2, ffn2_w_in, ffn2_w_out):
    batch, seq, d_model = x.shape
    assert d_model == D_MODEL and seq % max(CONV_TM, DILATION_GROUPS[-1][1] * BLOCK) == 0
    depth = norm_ffn1.shape[0]
    h = x.reshape(batch * seq, D_MODEL)
    bias = _bias_tiles(rel_bias) if depth > 1 else None
    for i in range(depth):
        h = _ffn(h, norm_ffn1[i], ffn1_w_in[i], ffn1_w_out[i])
        j = i // 2
        if i % 2 == 0:
            h = _conv_mixer(h, seq, norm_mix[i], conv_w_in[j], conv_w[j], conv_w_out[j])
        else:
            qkv = _qkv_proj(h, norm_mix[i], attn_w_qkv[j], attn_q_gain[j], attn_k_gain[j])
            parts = [_attn_group(qkv, bias, g, batch, seq) for g in range(N_GROUPS)]
            h = _merge_proj(h, [o for o, _ in parts], [l for _, l in parts], attn_w_out[j])
        h = _ffn(h, norm_ffn2[i], ffn2_w_in[i], ffn2_w_out[i])
    return h.reshape(batch, seq, D_MODEL)
```

```python
import functools
import math

import numpy as np
import jax
import jax.numpy as jnp
from jax import lax
from jax.experimental import pallas as pl
from jax.experimental.pallas import tpu as pltpu

D_MODEL = 1024
D_FF = 2816
CONV_WIDTH = 3
HEAD_DIM = 64
N_HEADS = D_MODEL // HEAD_DIM
N_PAIRS = N_HEADS // 2
DILATION_GROUPS = ((128, 1), (512, 4), (2048, 16))
N_GROUPS = len(DILATION_GROUPS)
BLOCK = 128
N_BUCKETS = 32
MAX_EXACT = N_BUCKETS // 2
MAX_DISTANCE = 2048
RMS_EPS = 1e-6
NEG_INF = -1e30
LOG2E = math.log2(math.e)
LN2 = math.log(2.0)

LANES = 128
MXU_DIM = 256
UNIT = MXU_DIM
VMEM_LIMIT = 56 * 1024 * 1024

F32 = jnp.float32
BF16 = jnp.bfloat16


def _rmsnorm_bf16(x, g):
    ms = jnp.mean(x * x, axis=-1, keepdims=True)
    return (x * lax.rsqrt(ms + RMS_EPS) * g).astype(BF16)


def _dot(a, b):
    return jnp.dot(a, b, preferred_element_type=F32)


def _resident(shape):
    return pl.BlockSpec(shape, lambda *_: (0,) * len(shape), pipeline_mode=pl.Buffered(1))


def _params(*sem):
    return pltpu.CompilerParams(dimension_semantics=sem, vmem_limit_bytes=VMEM_LIMIT)


def _strand_perms():
    perms = np.zeros((N_GROUPS, UNIT, UNIT), np.float32)
    for g, (_, r) in enumerate(DILATION_GROUPS):
        tok = np.arange(UNIT)
        perms[g, (tok % r) * (UNIT // r) + tok // r, tok] = 1.0
    return perms


FFN_TM = 512
FFN_TF = D_FF // 2


def _ffn_kernel(x_ref, g_ref, win_ref, wout_ref, o_ref):
    x = x_ref[...]
    xn = _rmsnorm_bf16(x, g_ref[...])
    acc = None
    for c in range(D_FF // FFN_TF):
        gate = _dot(xn, win_ref[:, pl.ds(c * FFN_TF, FFN_TF)])
        up = _dot(xn, win_ref[:, pl.ds(D_FF + c * FFN_TF, FFN_TF)])
        a = (gate * jax.nn.sigmoid(gate) * up).astype(BF16)
        part = _dot(a, wout_ref[pl.ds(c * FFN_TF, FFN_TF), :])
        acc = part if acc is None else acc + part
    o_ref[...] = x + 0.5 * acc


def _ffn(h, g, w_in, w_out):
    n_tok = h.shape[0]
    return pl.pallas_call(
        _ffn_kernel,
        grid=(n_tok // FFN_TM,),
        in_specs=[
            pl.BlockSpec((FFN_TM, D_MODEL), lambda i: (i, 0)),
            _resident((1, D_MODEL)),
            _resident((D_MODEL, 2 * D_FF)),
            _resident((D_FF, D_MODEL)),
        ],
        out_specs=pl.BlockSpec((FFN_TM, D_MODEL), lambda i: (i, 0)),
        out_shape=jax.ShapeDtypeStruct((n_tok, D_MODEL), F32),
        compiler_params=_params("arbitrary"),
        name="ffn",
    )(h, g.reshape(1, D_MODEL), w_in.astype(BF16), w_out.astype(BF16))


CONV_TM = 512
CARRY = 8


def _conv_kernel(tiles_per_seq, x_ref, g_ref, win_ref, cw_ref, wout_ref, o_ref, vp_ref):
    @pl.when(pl.program_id(0) % tiles_per_seq == 0)
    def _():
        vp_ref[pl.ds(0, CARRY), :] = jnp.zeros((CARRY, D_MODEL), F32)

    x = x_ref[...]
    xn = _rmsnorm_bf16(x, g_ref[...])
    b_gate = _dot(xn, win_ref[:, pl.ds(0, D_MODEL)])
    c_gate = _dot(xn, win_ref[:, pl.ds(D_MODEL, D_MODEL)])
    u = _dot(xn, win_ref[:, pl.ds(2 * D_MODEL, D_MODEL)])
    v = c_gate * u
    vp_ref[pl.ds(CARRY, CONV_TM), :] = v
    conv = cw_ref[pl.ds(0, 1), :] * v
    for lag in range(1, CONV_WIDTH):
        conv = conv + cw_ref[pl.ds(lag, 1), :] * vp_ref[pl.ds(CARRY - lag, CONV_TM), :]
    z = (b_gate * conv).astype(BF16)
    o_ref[...] = x + _dot(z, wout_ref[...])
    vp_ref[pl.ds(0, CARRY), :] = vp_ref[pl.ds(CONV_TM, CARRY), :]


def _conv_mixer(h, seq, g, w_in, w_conv, w_out):
    n_tok = h.shape[0]
    return pl.pallas_call(
        functools.partial(_conv_kernel, seq // CONV_TM),
        grid=(n_tok // CONV_TM,),
        in_specs=[
            pl.BlockSpec((CONV_TM, D_MODEL), lambda i: (i, 0)),
            _resident((1, D_MODEL)),
            _resident((D_MODEL, 3 * D_MODEL)),
            _resident((CONV_WIDTH, D_MODEL)),
            _resident((D_MODEL, D_MODEL)),
        ],
        out_specs=pl.BlockSpec((CONV_TM, D_MODEL), lambda i: (i, 0)),
        out_shape=jax.ShapeDtypeStruct((n_tok, D_MODEL), F32),
        scratch_shapes=[pltpu.VMEM((CARRY + CONV_TM, D_MODEL), F32)],
        compiler_params=_params("arbitrary"),
        name="conv_mixer",
    )(h, g.reshape(1, D_MODEL), w_in.astype(BF16), w_conv, w_out.astype(BF16))


QKV_TM = 1024
N_QKV_TILES = 3 * N_GROUPS


def _qkv_kernel(x_ref, g_ref, w_ref, gain_ref, bd_ref, perm_ref, o_ref, xn_ref, xp_ref):
    j = pl.program_id(1)

    @pl.when(j == 0)
    def _():
        xn_ref[...] = _rmsnorm_bf16(x_ref[...], g_ref[...])

    @pl.when(j % 3 == 0)
    def _():
        for u in range(QKV_TM // UNIT):
            rows = pl.ds(u * UNIT, UNIT)
            xp_ref[rows, :] = _dot(perm_ref[...], xn_ref[rows, :]).astype(BF16)

    y = _dot(xp_ref[...], w_ref[...])

    @pl.when(j % 3 == 2)
    def _():
        o_ref[...] = y.astype(BF16)

    @pl.when(j % 3 != 2)
    def _():
        y2 = (y * y).astype(BF16)
        ss = jnp.concatenate(
            [_dot(y2[:, c * MXU_DIM:(c + 1) * MXU_DIM], bd_ref[...]) for c in range(D_MODEL // MXU_DIM)],
            axis=1)
        o_ref[...] = (y * lax.rsqrt(ss * (1.0 / HEAD_DIM) + RMS_EPS) * gain_ref[...]).astype(BF16)


def _qkv_proj(h, g, w_qkv, q_gain, k_gain):
    n_tok = h.shape[0]
    gains = jnp.stack(
        [jnp.tile(q_gain, (1, N_HEADS)) * (HEAD_DIM ** -0.5 * LOG2E), jnp.tile(k_gain, (1, N_HEADS)),
         jnp.ones((N_GROUPS, D_MODEL), F32)], axis=1).reshape(N_QKV_TILES, 1, D_MODEL)
    seg = np.arange(MXU_DIM) // HEAD_DIM
    block_diag = jnp.asarray(seg[:, None] == seg[None, :], BF16)
    return pl.pallas_call(
        _qkv_kernel,
        grid=(n_tok // QKV_TM, N_QKV_TILES),
        in_specs=[
            pl.BlockSpec((QKV_TM, D_MODEL), lambda i, j: (i, 0)),
            _resident((1, D_MODEL)),
            pl.BlockSpec((D_MODEL, D_MODEL), lambda i, j: (0, j)),
            pl.BlockSpec((None, 1, D_MODEL), lambda i, j: (j, 0, 0)),
            _resident((MXU_DIM, MXU_DIM)),
            pl.BlockSpec((None, UNIT, UNIT), lambda i, j: (j // 3, 0, 0)),
        ],
        out_specs=pl.BlockSpec((QKV_TM, D_MODEL), lambda i, j: (i, j)),
        out_shape=jax.ShapeDtypeStruct((n_tok, N_QKV_TILES * D_MODEL), BF16),
        scratch_shapes=[pltpu.VMEM((QKV_TM, D_MODEL), BF16), pltpu.VMEM((QKV_TM, D_MODEL), BF16)],
        compiler_params=_params("arbitrary", "arbitrary"),
        name="qkv_proj",
    )(h, g.reshape(1, D_MODEL), w_qkv.astype(BF16), gains, block_diag, jnp.asarray(_strand_perms(), BF16))


def _t5_bucket_np(dist):
    nf = np.maximum(dist, 1).astype(np.float32)
    large = MAX_EXACT + (np.log(nf / np.float32(MAX_EXACT)) / np.float32(math.log(MAX_DISTANCE / MAX_EXACT))
                         * np.float32(N_BUCKETS - MAX_EXACT)).astype(np.int32)
    large = np.minimum(large, N_BUCKETS - 1)
    return np.where(dist < MAX_EXACT, dist, large)


def _bucket_maps():
    i = (np.arange(2 * BLOCK) % BLOCK)[:, None]
    j = np.arange(2 * BLOCK)[None, :]
    step = BLOCK + i - j
    maps = []
    for window, dilation in DILATION_GROUPS:
        n_steps = window // dilation
        in_band = (step >= 0) & (step <= n_steps)
        bucket = _t5_bucket_np(np.clip(step, 0, n_steps) * dilation)
        maps.append(np.where(in_band, bucket, -1).astype(np.int32))
    return np.stack(maps)


def _bias_kernel(rel_ref, bm_ref, o_ref):
    g = pl.program_id(0)
    p = pl.program_id(1)
    bm = bm_ref[...]
    shape = bm.shape
    top = lax.broadcasted_iota(jnp.int32, shape, 0) < BLOCK
    col = lax.broadcasted_iota(jnp.int32, shape, 1)
    acc = jnp.zeros(shape, F32)
    for b in range(N_BUCKETS):
        val = jnp.where(top, rel_ref[b, g * N_HEADS + 2 * p], rel_ref[b, g * N_HEADS + 2 * p + 1])
        acc = jnp.where(bm == b, val, acc)
    rest = jnp.where(bm < 0, NEG_INF, acc * LOG2E)
    o_ref[1] = rest
    o_ref[0] = jnp.where(col < BLOCK, NEG_INF, rest)


def _bias_tiles(rel_bias):
    tile = (2 * BLOCK, 2 * BLOCK)
    return pl.pallas_call(
        _bias_kernel,
        grid=(N_GROUPS, N_PAIRS),
        in_specs=[
            pl.BlockSpec(memory_space=pltpu.SMEM),
            pl.BlockSpec((None,) + tile, lambda g, p: (g, 0, 0)),
        ],
        out_specs=pl.BlockSpec((None, 2, None) + tile, lambda g, p: (g, 0, p, 0, 0)),
        out_shape=jax.ShapeDtypeStruct((N_GROUPS, 2, N_PAIRS) + tile, F32),
        compiler_params=_params("arbitrary", "arbitrary"),
        name="bias_tiles",
    )(rel_bias, jnp.asarray(_bucket_maps()))


ATTN_NB = 4
ATTN_ROWS = ATTN_NB * BLOCK


def _attn_kernel(q_ref, kp_ref, kc_ref, vp_ref, vc_ref, bias_ref, o_ref, lse_ref, k_all, v_all):
    chunk = pl.program_id(2)
    upb = q_ref.shape[0] // ATTN_NB
    rows_per_unit = q_ref.shape[1]

    @pl.when((pl.program_id(0) == 0) & (pl.program_id(1) == 0) & (chunk == 0))
    def _():
        v_all[...] = jnp.ones(v_all.shape, BF16)

    k_all[pl.ds(0, BLOCK), :] = kp_ref[...].reshape(BLOCK, D_MODEL)
    k_all[pl.ds(BLOCK, ATTN_ROWS), :] = kc_ref[...].reshape(ATTN_ROWS, D_MODEL)
    for p in range(N_PAIRS):
        cs = pl.ds(p * LANES, LANES)
        v_all[pl.ds(0, BLOCK), pl.ds(2 * p * LANES, LANES)] = vp_ref[:, :, cs].reshape(BLOCK, LANES)
        v_all[pl.ds(BLOCK, ATTN_ROWS), pl.ds(2 * p * LANES, LANES)] = vc_ref[:, :, cs].reshape(ATTN_ROWS, LANES)

    lane = lax.broadcasted_iota(jnp.int32, (BLOCK, LANES), 1)
    lo = lane < HEAD_DIM

    def block_body(blk, carry):
        not_first = jnp.minimum(chunk * ATTN_NB + blk, 1)
        units = pl.ds(blk * upb, upb)
        keys = pl.ds(pl.multiple_of(blk * BLOCK, BLOCK), 2 * BLOCK)
        lse_all = jnp.zeros((BLOCK, LANES), F32)
        for p in range(N_PAIRS):
            cs = pl.ds(p * LANES, LANES)
            q = q_ref[units, :, cs].reshape(BLOCK, LANES)
            zero = jnp.zeros_like(q)
            q2 = jnp.concatenate([jnp.where(lo, q, zero), jnp.where(lo, zero, q)], axis=0)
            logits = lax.dot_general(q2, k_all[keys, cs], (((1,), (1,)), ((), ())), preferred_element_type=F32)
            logits = logits + bias_ref[not_first, p]
            m = jnp.max(logits, axis=-1, keepdims=True)
            prob = jnp.exp2(logits - m).astype(BF16)
            acc = _dot(prob, v_all[keys, pl.ds(2 * p * LANES, 2 * LANES)])
            o = jnp.where(lo, acc[:BLOCK, :LANES], acc[BLOCK:, :LANES])
            s = jnp.where(lo, acc[:BLOCK, LANES:], acc[BLOCK:, LANES:])
            mm = jnp.where(lo, m[:BLOCK], m[BLOCK:])
            o_ref[units, :, cs] = (o / s).astype(BF16).reshape(upb, rows_per_unit, LANES)
            lse = mm * LN2 + jnp.log(s)
            keep = (lane == 2 * p) | (lane == HEAD_DIM + 2 * p + 1)
            lse_all = jnp.where(keep, lse, lse_all)
        lse_ref[units] = lse_all.reshape(upb, rows_per_unit, LANES)
        return carry

    lax.fori_loop(0, ATTN_NB, block_body, 0)


def _attn_group(qkv, bias, g, batch, seq):
    r = DILATION_GROUPS[g][1]
    rows_per_unit = min(UNIT // r, BLOCK)
    n_units = seq // (r * rows_per_unit)
    upb = BLOCK // rows_per_unit
    qkv_cols = N_QKV_TILES * D_MODEL
    view = qkv.reshape(batch, n_units, r, rows_per_unit, qkv_cols)

    def cur(which):
        return lambda b, s, c: (b, c, s, 0, 3 * g + which)

    def prev(which):
        return lambda b, s, c: (b, jnp.maximum(c * ATTN_NB - 1, 0), s, 0, 3 * g + which)

    cur_blk = (None, ATTN_NB * upb, None, rows_per_unit, D_MODEL)
    prev_blk = (None, upb, None, rows_per_unit, D_MODEL)
    o, lse = pl.pallas_call(
        _attn_kernel,
        grid=(batch, r, seq // (r * ATTN_ROWS)),
        in_specs=[
            pl.BlockSpec(cur_blk, cur(0)),
            pl.BlockSpec(prev_blk, prev(1)),
            pl.BlockSpec(cur_blk, cur(1)),
            pl.BlockSpec(prev_blk, prev(2)),
            pl.BlockSpec(cur_blk, cur(2)),
            pl.BlockSpec((None, 2, N_PAIRS, 2 * BLOCK, 2 * BLOCK), lambda b, s, c: (g, 0, 0, 0, 0)),
        ],
        out_specs=[
            pl.BlockSpec(cur_blk, lambda b, s, c: (b, c, s, 0, 0)),
            pl.BlockSpec((None, ATTN_NB * upb, None, rows_per_unit, LANES), lambda b, s, c: (b, c, s, 0, 0)),
        ],
        out_shape=[
            jax.ShapeDtypeStruct((batch, n_units, r, rows_per_unit, D_MODEL), BF16),
            jax.ShapeDtypeStruct((batch, n_units, r, rows_per_unit, LANES), F32),
        ],
        scratch_shapes=[
            pltpu.VMEM((BLOCK + ATTN_ROWS, D_MODEL), BF16),
            pltpu.VMEM((BLOCK + ATTN_ROWS, 2 * D_MODEL), BF16),
        ],
        compiler_params=_params("arbitrary", "arbitrary", "arbitrary"),
        name=f"attn_r{r}",
    )(view, view, view, view, view, bias)
    return o.reshape(batch * seq, D_MODEL), lse.reshape(batch * seq, LANES)


MERGE_TM = 512


def _split3(x):
    hi = x.astype(BF16)
    r1 = x - hi.astype(F32)
    mid = r1.astype(BF16)
    lo = (r1 - mid.astype(F32)).astype(BF16)
    return hi, mid, lo


def _merge_kernel(h_ref, o0_ref, o1_ref, o2_ref, l0_ref, l1_ref, l2_ref, pt_ref, e_ref, w_ref, out_ref):
    o_refs = (o0_ref, o1_ref, o2_ref)
    l_refs = (l0_ref, l1_ref, l2_ref)
    n_units = MERGE_TM // UNIT

    def to_token_order(g, rows_bf16):
        return rows_bf16.astype(F32) if g == 0 else _dot(pt_ref[g], rows_bf16)

    lses = []
    for g in range(N_GROUPS):
        parts = []
        for u in range(n_units):
            l = l_refs[g][pl.ds(u * UNIT, UNIT), :]
            if g == 0:
                parts.append(l)
            else:
                t = to_token_order(g, jnp.concatenate(_split3(l), axis=1))
                parts.append(t[:, :LANES] + t[:, LANES:2 * LANES] + t[:, 2 * LANES:])
        lses.append(jnp.concatenate(parts, axis=0))
    m = jnp.maximum(jnp.maximum(lses[0], lses[1]), lses[2])
    es = [jnp.exp(l - m) for l in lses]
    den = es[0] + es[1] + es[2]
    merged = None
    for g in range(N_GROUPS):
        w = es[g] / den
        w_hi = w.astype(BF16)
        w_lo = (w - w_hi.astype(F32)).astype(BF16)
        w_full = _dot(jnp.concatenate([w_hi, w_lo], axis=1), e_ref[...])
        o_tok = jnp.concatenate(
            [to_token_order(g, o_refs[g][pl.ds(u * UNIT, UNIT), :]) for u in range(n_units)], axis=0)
        term = w_full * o_tok
        merged = term if merged is None else merged + term
    out_ref[...] = h_ref[...] + _dot(merged.astype(BF16), w_ref[...])


def _merge_proj(h, outs, lses, w_out):
    n_tok = h.shape[0]
    expand = np.zeros((2 * LANES, D_MODEL), np.float32)
    for head in range(N_HEADS):
        src = head if head % 2 == 0 else HEAD_DIM + head
        expand[src, head * HEAD_DIM:(head + 1) * HEAD_DIM] = 1.0
    expand[LANES:] = expand[:LANES]
    perms_t = np.transpose(_strand_perms(), (0, 2, 1))
    tok = lambda width: pl.BlockSpec((MERGE_TM, width), lambda i: (i, 0))
    return pl.pallas_call(
        _merge_kernel,
        grid=(n_tok // MERGE_TM,),
        in_specs=[tok(D_MODEL)] + [tok(D_MODEL)] * N_GROUPS + [tok(LANES)] * N_GROUPS
        + [_resident((N_GROUPS, UNIT, UNIT)), _resident((2 * LANES, D_MODEL)), _resident((D_MODEL, D_MODEL))],
        out_specs=tok(D_MODEL),
        out_shape=jax.ShapeDtypeStruct((n_tok, D_MODEL), F32),
        compiler_params=_params("arbitrary"),
        name="merge_proj",
    )(h, *outs, *lses, jnp.asarray(perms_t, BF16), jnp.asarray(expand, BF16), w_out.astype(BF16))


def kernel(x, norm_ffn1, ffn1_w_in, ffn1_w_out, norm_mix, conv_w_in, conv_w, conv_w_out, attn_w_qkv,
           attn_q_gain, attn_k_gain, attn_w_out, rel_bias, norm_ffn2, ffn2_w_in, ffn2_w_out):
    batch, seq, d_model = x.shape
    assert d_model == D_MODEL and seq % max(QKV_TM, DILATION_GROUPS[-1][1] * ATTN_ROWS) == 0
    depth = norm_ffn1.shape[0]
    h = x.reshape(batch * seq, D_MODEL)
    bias = _bias_tiles(rel_bias) if depth > 1 else None
    for i in range(depth):
        h = _ffn(h, norm_ffn1[i], ffn1_w_in[i], ffn1_w_out[i])
        j = i // 2
        if i % 2 == 0:
            h = _conv_mixer(h, seq, norm_mix[i], conv_w_in[j], conv_w[j], conv_w_out[j])
        else:
            qkv = _qkv_proj(h, norm_mix[i], attn_w_qkv[j], attn_q_gain[j], attn_k_gain[j])
            parts = [_attn_group(qkv, bias, g, batch, seq) for g in range(N_GROUPS)]
            h = _merge_proj(h, [o for o, _ in parts], [l for _, l in parts], attn_w_out[j])
        h = _ffn(h, norm_ffn2[i], ffn2_w_in[i], ffn2_w_out[i])
    return h.reshape(batch, seq, D_MODEL)
```

```python
import functools
import math

import numpy as np
import jax
import jax.numpy as jnp
from jax import lax
from jax.experimental import pallas as pl
from jax.experimental.pallas import tpu as pltpu

D_MODEL = 1024
D_FF = 2816
CONV_WIDTH = 3
HEAD_DIM = 64
N_HEADS = D_MODEL // HEAD_DIM
N_PAIRS = N_HEADS // 2
DILATION_GROUPS = ((128, 1), (512, 4), (2048, 16))
N_GROUPS = len(DILATION_GROUPS)
BLOCK = 128
N_BUCKETS = 32
MAX_EXACT = N_BUCKETS // 2
MAX_DISTANCE = 2048
RMS_EPS = 1e-6
NEG_INF = -1e30
LOG2E = math.log2(math.e)
LN2 = math.log(2.0)

LANES = 128
MXU_DIM = 256
UNIT = MXU_DIM
VMEM_LIMIT = 56 * 1024 * 1024

F32 = jnp.float32
BF16 = jnp.bfloat16


def _rmsnorm_bf16(x, g):
    ms = jnp.mean(x * x, axis=-1, keepdims=True)
    return (x * lax.rsqrt(ms + RMS_EPS) * g).astype(BF16)


def _dot(a, b):
    return jnp.dot(a, b, preferred_element_type=F32)


def _resident(shape):
    return pl.BlockSpec(shape, lambda *_: (0,) * len(shape), pipeline_mode=pl.Buffered(1))


def _layer(shape, layer):
    return pl.BlockSpec((None,) + shape, lambda *_: (layer,) + (0,) * len(shape), pipeline_mode=pl.Buffered(1))


def _params(*sem):
    return pltpu.CompilerParams(dimension_semantics=sem, vmem_limit_bytes=VMEM_LIMIT)


def _strand_perms():
    perms = np.zeros((N_GROUPS, UNIT, UNIT), np.float32)
    for g, (_, r) in enumerate(DILATION_GROUPS):
        tok = np.arange(UNIT)
        perms[g, (tok % r) * (UNIT // r) + tok // r, tok] = 1.0
    return perms


FFN_TM = 512
FFN_TF = D_FF // 2


def _ffn_kernel(x_ref, g_ref, win_ref, wout_ref, o_ref):
    x = x_ref[...]
    xn = _rmsnorm_bf16(x, g_ref[...])
    acc = None
    for c in range(D_FF // FFN_TF):
        gate = _dot(xn, win_ref[:, pl.ds(c * FFN_TF, FFN_TF)])
        up = _dot(xn, win_ref[:, pl.ds(D_FF + c * FFN_TF, FFN_TF)])
        a = (gate * jax.nn.sigmoid(gate) * up).astype(BF16)
        part = _dot(a, wout_ref[pl.ds(c * FFN_TF, FFN_TF), :])
        acc = part if acc is None else acc + part
    o_ref[...] = x + 0.5 * acc


def _ffn(h, layer, g, w_in, w_out):
    n_tok = h.shape[0]
    return pl.pallas_call(
        _ffn_kernel,
        grid=(n_tok // FFN_TM,),
        in_specs=[
            pl.BlockSpec((FFN_TM, D_MODEL), lambda i: (i, 0)),
            _layer((1, D_MODEL), layer),
            _layer((D_MODEL, 2 * D_FF), layer),
            _layer((D_FF, D_MODEL), layer),
        ],
        out_specs=pl.BlockSpec((FFN_TM, D_MODEL), lambda i: (i, 0)),
        out_shape=jax.ShapeDtypeStruct((n_tok, D_MODEL), F32),
        compiler_params=_params("arbitrary"),
        name="ffn",
    )(h, g, w_in, w_out)


CONV_TM = 512
CARRY = 8


def _conv_kernel(tiles_per_seq, x_ref, g_ref, win_ref, cw_ref, wout_ref, o_ref, vp_ref):
    @pl.when(pl.program_id(0) % tiles_per_seq == 0)
    def _():
        vp_ref[pl.ds(0, CARRY), :] = jnp.zeros((CARRY, D_MODEL), F32)

    x = x_ref[...]
    xn = _rmsnorm_bf16(x, g_ref[...])
    b_gate = _dot(xn, win_ref[:, pl.ds(0, D_MODEL)])
    c_gate = _dot(xn, win_ref[:, pl.ds(D_MODEL, D_MODEL)])
    u = _dot(xn, win_ref[:, pl.ds(2 * D_MODEL, D_MODEL)])
    v = c_gate * u
    vp_ref[pl.ds(CARRY, CONV_TM), :] = v
    conv = cw_ref[pl.ds(0, 1), :] * v
    for lag in range(1, CONV_WIDTH):
        conv = conv + cw_ref[pl.ds(lag, 1), :] * vp_ref[pl.ds(CARRY - lag, CONV_TM), :]
    z = (b_gate * conv).astype(BF16)
    o_ref[...] = x + _dot(z, wout_ref[...])
    vp_ref[pl.ds(0, CARRY), :] = vp_ref[pl.ds(CONV_TM, CARRY), :]


def _conv_mixer(h, seq, layer, mixer, g, w_in, w_conv, w_out):
    n_tok = h.shape[0]
    return pl.pallas_call(
        functools.partial(_conv_kernel, seq // CONV_TM),
        grid=(n_tok // CONV_TM,),
        in_specs=[
            pl.BlockSpec((CONV_TM, D_MODEL), lambda i: (i, 0)),
            _layer((1, D_MODEL), layer),
            _layer((D_MODEL, 3 * D_MODEL), mixer),
            _layer((CONV_WIDTH, D_MODEL), mixer),
            _layer((D_MODEL, D_MODEL), mixer),
        ],
        out_specs=pl.BlockSpec((CONV_TM, D_MODEL), lambda i: (i, 0)),
        out_shape=jax.ShapeDtypeStruct((n_tok, D_MODEL), F32),
        scratch_shapes=[pltpu.VMEM((CARRY + CONV_TM, D_MODEL), F32)],
        compiler_params=_params("arbitrary"),
        name="conv_mixer",
    )(h, g, w_in, w_conv, w_out)


QKV_TM = 1024
QKV_TN = 2 * MXU_DIM
N_QKV_TILES = 3 * N_GROUPS


def _qkv_kernel(x_ref, g_ref, w_ref, gain_ref, bd_ref, perm_ref, o_ref, xn_ref, xp_ref):
    @pl.when(pl.program_id(1) == 0)
    def _():
        xn_ref[...] = _rmsnorm_bf16(x_ref[...], g_ref[...])

    for u in range(QKV_TM // UNIT):
        rows = pl.ds(u * UNIT, UNIT)
        xp_ref[rows, :] = _dot(perm_ref[...], xn_ref[rows, :]).astype(BF16)

    for c in range(3 * D_MODEL // QKV_TN):
        cols = pl.ds(c * QKV_TN, QKV_TN)
        y = _dot(xp_ref[...], w_ref[:, cols])
        if c * QKV_TN >= 2 * D_MODEL:
            o_ref[:, cols] = y.astype(BF16)
        else:
            y2 = (y * y).astype(BF16)
            ms = jnp.concatenate(
                [_dot(y2[:, t * MXU_DIM:(t + 1) * MXU_DIM], bd_ref[...]) for t in range(QKV_TN // MXU_DIM)],
                axis=1)
            o_ref[:, cols] = (y * lax.rsqrt(ms + RMS_EPS) * gain_ref[:, cols]).astype(BF16)


def _qkv_proj(h, layer, mixer, g, w_qkv, q_gain, k_gain):
    n_tok = h.shape[0]
    gains = jnp.concatenate(
        [jnp.tile(q_gain, (1, N_HEADS)) * (HEAD_DIM ** -0.5 * LOG2E), jnp.tile(k_gain, (1, N_HEADS)),
         jnp.ones((N_GROUPS, D_MODEL), F32)], axis=1).reshape(N_GROUPS, 1, 3 * D_MODEL)
    seg = np.arange(MXU_DIM) // HEAD_DIM
    block_diag = jnp.asarray((seg[:, None] == seg[None, :]) / HEAD_DIM, BF16)
    return pl.pallas_call(
        _qkv_kernel,
        grid=(n_tok // QKV_TM, N_GROUPS),
        in_specs=[
            pl.BlockSpec((QKV_TM, D_MODEL), lambda i, j: (i, 0)),
            _layer((1, D_MODEL), layer),
            pl.BlockSpec((None, D_MODEL, 3 * D_MODEL), lambda i, j: (mixer, 0, j)),
            pl.BlockSpec((None, 1, 3 * D_MODEL), lambda i, j: (j, 0, 0)),
            _resident((MXU_DIM, MXU_DIM)),
            pl.BlockSpec((None, UNIT, UNIT), lambda i, j: (j, 0, 0)),
        ],
        out_specs=pl.BlockSpec((QKV_TM, 3 * D_MODEL), lambda i, j: (i, j)),
        out_shape=jax.ShapeDtypeStruct((n_tok, N_QKV_TILES * D_MODEL), BF16),
        scratch_shapes=[pltpu.VMEM((QKV_TM, D_MODEL), BF16), pltpu.VMEM((QKV_TM, D_MODEL), BF16)],
        compiler_params=_params("arbitrary", "arbitrary"),
        name="qkv_proj",
    )(h, g, w_qkv, gains, block_diag, jnp.asarray(_strand_perms(), BF16))


def _t5_bucket_np(dist):
    nf = np.maximum(dist, 1).astype(np.float32)
    large = MAX_EXACT + (np.log(nf / np.float32(MAX_EXACT)) / np.float32(math.log(MAX_DISTANCE / MAX_EXACT))
                         * np.float32(N_BUCKETS - MAX_EXACT)).astype(np.int32)
    large = np.minimum(large, N_BUCKETS - 1)
    return np.where(dist < MAX_EXACT, dist, large)


def _bucket_maps():
    i = (np.arange(2 * BLOCK) % BLOCK)[:, None]
    j = np.arange(2 * BLOCK)[None, :]
    step = BLOCK + i - j
    maps = []
    for window, dilation in DILATION_GROUPS:
        n_steps = window // dilation
        in_band = (step >= 0) & (step <= n_steps)
        bucket = _t5_bucket_np(np.clip(step, 0, n_steps) * dilation)
        maps.append(np.where(in_band, bucket, -1).astype(np.int32))
    return np.stack(maps)


def _bias_kernel(rel_ref, bm_ref, o_ref):
    g = pl.program_id(0)
    p = pl.program_id(1)
    bm = bm_ref[...]
    shape = bm.shape
    top = lax.broadcasted_iota(jnp.int32, shape, 0) < BLOCK
    col = lax.broadcasted_iota(jnp.int32, shape, 1)
    acc = jnp.zeros(shape, F32)
    for b in range(N_BUCKETS):
        val = jnp.where(top, rel_ref[b, g * N_HEADS + 2 * p], rel_ref[b, g * N_HEADS + 2 * p + 1])
        acc = jnp.where(bm == b, val, acc)
    rest = jnp.where(bm < 0, NEG_INF, acc * LOG2E)
    o_ref[1] = rest
    o_ref[0] = jnp.where(col < BLOCK, NEG_INF, rest)


def _bias_tiles(rel_bias):
    tile = (2 * BLOCK, 2 * BLOCK)
    return pl.pallas_call(
        _bias_kernel,
        grid=(N_GROUPS, N_PAIRS),
        in_specs=[
            pl.BlockSpec(memory_space=pltpu.SMEM),
            pl.BlockSpec((None,) + tile, lambda g, p: (g, 0, 0)),
        ],
        out_specs=pl.BlockSpec((None, 2, None) + tile, lambda g, p: (g, 0, p, 0, 0)),
        out_shape=jax.ShapeDtypeStruct((N_GROUPS, 2, N_PAIRS) + tile, F32),
        compiler_params=_params("arbitrary", "arbitrary"),
        name="bias_tiles",
    )(rel_bias, jnp.asarray(_bucket_maps()))


ATTN_NB = 4
ATTN_ROWS = ATTN_NB * BLOCK


def _attn_kernel(q_ref, kp_ref, kc_ref, vp_ref, vc_ref, bias_ref, o_ref, lse_ref):
    upb = q_ref.shape[0] // ATTN_NB
    rows_per_unit = q_ref.shape[1]
    not_first = jnp.minimum(pl.program_id(2), 1)
    lane = lax.broadcasted_iota(jnp.int32, (BLOCK, LANES), 1)
    lo = lane < HEAD_DIM
    ones = jnp.ones((2 * BLOCK, LANES), BF16)

    def keys_of(prev_ref, cur_ref, blk, cs):
        if blk == 0:
            return jnp.concatenate([prev_ref[:, :, cs].reshape(BLOCK, LANES),
                                    cur_ref[pl.ds(0, upb), :, cs].reshape(BLOCK, LANES)], axis=0)
        return cur_ref[pl.ds((blk - 1) * upb, 2 * upb), :, cs].reshape(2 * BLOCK, LANES)

    for blk in range(ATTN_NB):
        units = pl.ds(blk * upb, upb)
        lse_all = jnp.zeros((BLOCK, LANES), F32)
        for p in range(N_PAIRS):
            cs = pl.ds(p * LANES, LANES)
            q = q_ref[units, :, cs].reshape(BLOCK, LANES)
            zero = jnp.zeros_like(q)
            q2 = jnp.concatenate([jnp.where(lo, q, zero), jnp.where(lo, zero, q)], axis=0)
            k = keys_of(kp_ref, kc_ref, blk, cs)
            logits = lax.dot_general(q2, k, (((1,), (1,)), ((), ())), preferred_element_type=F32)
            logits = logits + (bias_ref[not_first, p] if blk == 0 else bias_ref[1, p])
            m = jnp.max(logits, axis=-1, keepdims=True)
            prob = jnp.exp2(logits - m).astype(BF16)
            acc = _dot(prob, jnp.concatenate([keys_of(vp_ref, vc_ref, blk, cs), ones], axis=1))
            o = jnp.where(lo, acc[:BLOCK, :LANES], acc[BLOCK:, :LANES])
            s = jnp.where(lo, acc[:BLOCK, LANES:], acc[BLOCK:, LANES:])
            mm = jnp.where(lo, m[:BLOCK], m[BLOCK:])
            o_ref[units, :, cs] = (o / s).astype(BF16).reshape(upb, rows_per_unit, LANES)
            lse = mm * LN2 + jnp.log(s)
            keep = (lane == 2 * p) | (lane == HEAD_DIM + 2 * p + 1)
            lse_all = jnp.where(keep, lse, lse_all)
        lse_ref[units] = lse_all.reshape(upb, rows_per_unit, LANES)


def _attn_group(qkv, bias, g, batch, seq):
    r = DILATION_GROUPS[g][1]
    rows_per_unit = min(UNIT // r, BLOCK)
    n_units = seq // (r * rows_per_unit)
    upb = BLOCK // rows_per_unit
    qkv_cols = N_QKV_TILES * D_MODEL
    view = qkv.reshape(batch, n_units, r, rows_per_unit, qkv_cols)

    def cur(which):
        return lambda b, s, c: (b, c, s, 0, 3 * g + which)

    def prev(which):
        return lambda b, s, c: (b, jnp.maximum(c * ATTN_NB - 1, 0), s, 0, 3 * g + which)

    cur_blk = (None, ATTN_NB * upb, None, rows_per_unit, D_MODEL)
    prev_blk = (None, upb, None, rows_per_unit, D_MODEL)
    o, lse = pl.pallas_call(
        _attn_kernel,
        grid=(batch, r, seq // (r * ATTN_ROWS)),
        in_specs=[
            pl.BlockSpec(cur_blk, cur(0)),
            pl.BlockSpec(prev_blk, prev(1)),
            pl.BlockSpec(cur_blk, cur(1)),
            pl.BlockSpec(prev_blk, prev(2)),
            pl.BlockSpec(cur_blk, cur(2)),
            pl.BlockSpec((None, 2, N_PAIRS, 2 * BLOCK, 2 * BLOCK), lambda b, s, c: (g, 0, 0, 0, 0)),
        ],
        out_specs=[
            pl.BlockSpec(cur_blk, lambda b, s, c: (b, c, s, 0, 0)),
            pl.BlockSpec((None, ATTN_NB * upb, None, rows_per_unit, LANES), lambda b, s, c: (b, c, s, 0, 0)),
        ],
        out_shape=[
            jax.ShapeDtypeStruct((batch, n_units, r, rows_per_unit, D_MODEL), BF16),
            jax.ShapeDtypeStruct((batch, n_units, r, rows_per_unit, LANES), F32),
        ],
        compiler_params=_params("arbitrary", "arbitrary", "arbitrary"),
        name=f"attn_r{r}",
    )(view, view, view, view, view, bias)
    return o.reshape(batch * seq, D_MODEL), lse.reshape(batch * seq, LANES)


MERGE_TM = 512


def _split3(x):
    hi = x.astype(BF16)
    r1 = x - hi.astype(F32)
    mid = r1.astype(BF16)
    lo = (r1 - mid.astype(F32)).astype(BF16)
    return hi, mid, lo


def _merge_kernel(h_ref, o0_ref, o1_ref, o2_ref, l0_ref, l1_ref, l2_ref, pt_ref, e_ref, w_ref, out_ref):
    o_refs = (o0_ref, o1_ref, o2_ref)
    l_refs = (l0_ref, l1_ref, l2_ref)
    n_units = MERGE_TM // UNIT

    def to_token_order(g, rows_bf16):
        return rows_bf16.astype(F32) if g == 0 else _dot(pt_ref[g], rows_bf16)

    lses = []
    for g in range(N_GROUPS):
        parts = []
        for u in range(n_units):
            l = l_refs[g][pl.ds(u * UNIT, UNIT), :]
            if g == 0:
                parts.append(l)
            else:
                t = to_token_order(g, jnp.concatenate(_split3(l), axis=1))
                parts.append(t[:, :LANES] + t[:, LANES:2 * LANES] + t[:, 2 * LANES:])
        lses.append(jnp.concatenate(parts, axis=0))
    m = jnp.maximum(jnp.maximum(lses[0], lses[1]), lses[2])
    es = [jnp.exp(l - m) for l in lses]
    den = es[0] + es[1] + es[2]
    merged = None
    for g in range(N_GROUPS):
        w = es[g] / den
        w_hi = w.astype(BF16)
        w_lo = (w - w_hi.astype(F32)).astype(BF16)
        w_full = _dot(jnp.concatenate([w_hi, w_lo], axis=1), e_ref[...])
        o_tok = jnp.concatenate(
            [to_token_order(g, o_refs[g][pl.ds(u * UNIT, UNIT), :]) for u in range(n_units)], axis=0)
        term = w_full * o_tok
        merged = term if merged is None else merged + term
    out_ref[...] = h_ref[...] + _dot(merged.astype(BF16), w_ref[...])


def _merge_proj(h, outs, lses, mixer, w_out):
    n_tok = h.shape[0]
    expand = np.zeros((2 * LANES, D_MODEL), np.float32)
    for head in range(N_HEADS):
        src = head if head % 2 == 0 else HEAD_DIM + head
        expand[src, head * HEAD_DIM:(head + 1) * HEAD_DIM] = 1.0
    expand[LANES:] = expand[:LANES]
    perms_t = np.transpose(_strand_perms(), (0, 2, 1))
    tok = lambda width: pl.BlockSpec((MERGE_TM, width), lambda i: (i, 0))
    return pl.pallas_call(
        _merge_kernel,
        grid=(n_tok // MERGE_TM,),
        in_specs=[tok(D_MODEL)] + [tok(D_MODEL)] * N_GROUPS + [tok(LANES)] * N_GROUPS
        + [_resident((N_GROUPS, UNIT, UNIT)), _resident((2 * LANES, D_MODEL)), _layer((D_MODEL, D_MODEL), mixer)],
        out_specs=tok(D_MODEL),
        out_shape=jax.ShapeDtypeStruct((n_tok, D_MODEL), F32),
        compiler_params=_params("arbitrary"),
        name="merge_proj",
    )(h, *outs, *lses, jnp.asarray(perms_t, BF16), jnp.asarray(expand, BF16), w_out)


def kernel(x, norm_ffn1, ffn1_w_in, ffn1_w_out, norm_mix, conv_w_in, conv_w, conv_w_out, attn_w_qkv,
           attn_q_gain, attn_k_gain, attn_w_out, rel_bias, norm_ffn2, ffn2_w_in, ffn2_w_out):
    batch, seq, d_model = x.shape
    assert d_model == D_MODEL and seq % max(QKV_TM, DILATION_GROUPS[-1][1] * ATTN_ROWS) == 0
    depth = norm_ffn1.shape[0]
    h = x.reshape(batch * seq, D_MODEL)
    bias = _bias_tiles(rel_bias) if depth > 1 else None
    gains = [a.reshape(depth, 1, D_MODEL) for a in (norm_ffn1, norm_mix, norm_ffn2)]
    ffn1_w_in, ffn1_w_out, ffn2_w_in, ffn2_w_out, conv_w_in, conv_w_out, attn_w_qkv, attn_w_out = (
        a.astype(BF16) for a in
        (ffn1_w_in, ffn1_w_out, ffn2_w_in, ffn2_w_out, conv_w_in, conv_w_out, attn_w_qkv, attn_w_out))
    for i in range(depth):
        h = _ffn(h, i, gains[0], ffn1_w_in, ffn1_w_out)
        j = i // 2
        if i % 2 == 0:
            h = _conv_mixer(h, seq, i, j, gains[1], conv_w_in, conv_w, conv_w_out)
        else:
            qkv = _qkv_proj(h, i, j, gains[1], attn_w_qkv, attn_q_gain[j], attn_k_gain[j])
            parts = [_attn_group(qkv, bias, g, batch, seq) for g in range(N_GROUPS)]
            h = _merge_proj(h, [o for o, _ in parts], [l for _, l in parts], j, attn_w_out)
        h = _ffn(h, i, gains[2], ffn2_w_in, ffn2_w_out)
    return h.reshape(batch, seq, D_MODEL)
```

```python
import functools
import math

import numpy as np
import jax
import jax.numpy as jnp
from jax import lax
from jax.experimental import pallas as pl
from jax.experimental.pallas import tpu as pltpu

D_MODEL = 1024
D_FF = 2816
CONV_WIDTH = 3
HEAD_DIM = 64
N_HEADS = D_MODEL // HEAD_DIM
N_PAIRS = N_HEADS // 2
DILATION_GROUPS = ((128, 1), (512, 4), (2048, 16))
N_GROUPS = len(DILATION_GROUPS)
BLOCK = 128
N_BUCKETS = 32
MAX_EXACT = N_BUCKETS // 2
MAX_DISTANCE = 2048
RMS_EPS = 1e-6
NEG_INF = -1e30
LOG2E = math.log2(math.e)
LN2 = math.log(2.0)

LANES = 128
MXU_DIM = 256
UNIT = MXU_DIM
VMEM_LIMIT = 56 * 1024 * 1024

F32 = jnp.float32
BF16 = jnp.bfloat16


def _rmsnorm_bf16(x, g):
    ms = jnp.mean(x * x, axis=-1, keepdims=True)
    return (x * lax.rsqrt(ms + RMS_EPS) * g).astype(BF16)


def _dot(a, b):
    return jnp.dot(a, b, preferred_element_type=F32)


def _resident(shape):
    return pl.BlockSpec(shape, lambda *_: (0,) * len(shape), pipeline_mode=pl.Buffered(1))


def _layer(shape, layer):
    return pl.BlockSpec((None,) + shape, lambda *_: (layer,) + (0,) * len(shape), pipeline_mode=pl.Buffered(1))


def _params(*sem):
    return pltpu.CompilerParams(dimension_semantics=sem, vmem_limit_bytes=VMEM_LIMIT)


assert DILATION_GROUPS[0][1] == 1


def _strand_perms():
    perms = np.zeros((N_GROUPS, UNIT, UNIT), np.float32)
    for g, (_, r) in enumerate(DILATION_GROUPS):
        tok = np.arange(UNIT)
        perms[g, (tok % r) * (UNIT // r) + tok // r, tok] = 1.0
    return perms


FFN_TM = 512
FFN_CHUNKS = ((0, 5 * MXU_DIM), (5 * MXU_DIM, 6 * MXU_DIM))


def _ffn_kernel(x_ref, g_ref, win_ref, wout_ref, o_ref):
    x = x_ref[...]
    xn = _rmsnorm_bf16(x, g_ref[...])
    acc = None
    for start, size in FFN_CHUNKS:
        gate = _dot(xn, win_ref[:, pl.ds(start, size)])
        up = _dot(xn, win_ref[:, pl.ds(D_FF + start, size)])
        a = (gate * jax.nn.sigmoid(gate) * up).astype(BF16)
        part = _dot(a, wout_ref[pl.ds(start, size), :])
        acc = part if acc is None else acc + part
    o_ref[...] = x + 0.5 * acc


def _ffn(h, layer, g, w_in, w_out):
    n_tok = h.shape[0]
    return pl.pallas_call(
        _ffn_kernel,
        grid=(n_tok // FFN_TM,),
        in_specs=[
            pl.BlockSpec((FFN_TM, D_MODEL), lambda i: (i, 0)),
            _layer((1, D_MODEL), layer),
            _layer((D_MODEL, 2 * D_FF), layer),
            _layer((D_FF, D_MODEL), layer),
        ],
        out_specs=pl.BlockSpec((FFN_TM, D_MODEL), lambda i: (i, 0)),
        out_shape=jax.ShapeDtypeStruct((n_tok, D_MODEL), F32),
        compiler_params=_params("arbitrary"),
        name="ffn",
    )(h, g, w_in, w_out)


CONV_TM = 512
CARRY = 8


def _conv_kernel(tiles_per_seq, x_ref, g_ref, win_ref, cw_ref, wout_ref, o_ref, vp_ref):
    @pl.when(pl.program_id(0) % tiles_per_seq == 0)
    def _():
        vp_ref[pl.ds(0, CARRY), :] = jnp.zeros((CARRY, D_MODEL), F32)

    x = x_ref[...]
    xn = _rmsnorm_bf16(x, g_ref[...])
    b_gate = _dot(xn, win_ref[:, pl.ds(0, D_MODEL)])
    c_gate = _dot(xn, win_ref[:, pl.ds(D_MODEL, D_MODEL)])
    u = _dot(xn, win_ref[:, pl.ds(2 * D_MODEL, D_MODEL)])
    v = c_gate * u
    vp_ref[pl.ds(CARRY, CONV_TM), :] = v
    conv = cw_ref[pl.ds(0, 1), :] * v
    for lag in range(1, CONV_WIDTH):
        conv = conv + cw_ref[pl.ds(lag, 1), :] * vp_ref[pl.ds(CARRY - lag, CONV_TM), :]
    z = (b_gate * conv).astype(BF16)
    o_ref[...] = x + _dot(z, wout_ref[...])
    vp_ref[pl.ds(0, CARRY), :] = vp_ref[pl.ds(CONV_TM, CARRY), :]


def _conv_mixer(h, seq, layer, mixer, g, w_in, w_conv, w_out):
    n_tok = h.shape[0]
    return pl.pallas_call(
        functools.partial(_conv_kernel, seq // CONV_TM),
        grid=(n_tok // CONV_TM,),
        in_specs=[
            pl.BlockSpec((CONV_TM, D_MODEL), lambda i: (i, 0)),
            _layer((1, D_MODEL), layer),
            _layer((D_MODEL, 3 * D_MODEL), mixer),
            _layer((CONV_WIDTH, D_MODEL), mixer),
            _layer((D_MODEL, D_MODEL), mixer),
        ],
        out_specs=pl.BlockSpec((CONV_TM, D_MODEL), lambda i: (i, 0)),
        out_shape=jax.ShapeDtypeStruct((n_tok, D_MODEL), F32),
        scratch_shapes=[pltpu.VMEM((CARRY + CONV_TM, D_MODEL), F32)],
        compiler_params=_params("arbitrary"),
        name="conv_mixer",
    )(h, g, w_in, w_conv, w_out)


QKV_TM = 1024
QKV_TN = 2 * MXU_DIM
N_QKV_TILES = 3 * N_GROUPS


def _qkv_kernel(x_ref, g_ref, w_ref, gain_ref, bd_ref, perm_ref, o_ref, xn_ref, xp_ref):
    @pl.when(pl.program_id(1) == 0)
    def _():
        xn_ref[...] = _rmsnorm_bf16(x_ref[...], g_ref[...])

    @pl.when(pl.program_id(1) == 0)
    def _():
        xp_ref[...] = xn_ref[...]

    @pl.when(pl.program_id(1) > 0)
    def _():
        for u in range(QKV_TM // UNIT):
            rows = pl.ds(u * UNIT, UNIT)
            xp_ref[rows, :] = _dot(perm_ref[...], xn_ref[rows, :]).astype(BF16)

    for c in range(3 * D_MODEL // QKV_TN):
        cols = pl.ds(c * QKV_TN, QKV_TN)
        y = _dot(xp_ref[...], w_ref[:, cols])
        if c * QKV_TN >= 2 * D_MODEL:
            o_ref[:, cols] = y.astype(BF16)
        else:
            y2 = (y * y).astype(BF16)
            ms = jnp.concatenate(
                [_dot(y2[:, t * MXU_DIM:(t + 1) * MXU_DIM], bd_ref[...]) for t in range(QKV_TN // MXU_DIM)],
                axis=1)
            o_ref[:, cols] = (y * lax.rsqrt(ms + RMS_EPS) * gain_ref[:, cols]).astype(BF16)


def _qkv_proj(h, layer, mixer, g, w_qkv, q_gain, k_gain):
    n_tok = h.shape[0]
    gains = jnp.concatenate(
        [jnp.tile(q_gain, (1, N_HEADS)) * (HEAD_DIM ** -0.5 * LOG2E), jnp.tile(k_gain, (1, N_HEADS)),
         jnp.ones((N_GROUPS, D_MODEL), F32)], axis=1).reshape(N_GROUPS, 1, 3 * D_MODEL)
    seg = np.arange(MXU_DIM) // HEAD_DIM
    block_diag = jnp.asarray((seg[:, None] == seg[None, :]) / HEAD_DIM, BF16)
    return pl.pallas_call(
        _qkv_kernel,
        grid=(n_tok // QKV_TM, N_GROUPS),
        in_specs=[
            pl.BlockSpec((QKV_TM, D_MODEL), lambda i, j: (i, 0)),
            _layer((1, D_MODEL), layer),
            pl.BlockSpec((None, D_MODEL, 3 * D_MODEL), lambda i, j: (mixer, 0, j)),
            pl.BlockSpec((None, 1, 3 * D_MODEL), lambda i, j: (j, 0, 0)),
            _resident((MXU_DIM, MXU_DIM)),
            pl.BlockSpec((None, UNIT, UNIT), lambda i, j: (j, 0, 0)),
        ],
        out_specs=pl.BlockSpec((QKV_TM, 3 * D_MODEL), lambda i, j: (i, j)),
        out_shape=jax.ShapeDtypeStruct((n_tok, N_QKV_TILES * D_MODEL), BF16),
        scratch_shapes=[pltpu.VMEM((QKV_TM, D_MODEL), BF16), pltpu.VMEM((QKV_TM, D_MODEL), BF16)],
        compiler_params=_params("arbitrary", "arbitrary"),
        name="qkv_proj",
    )(h, g, w_qkv, gains, block_diag, jnp.asarray(_strand_perms(), BF16))


def _t5_bucket_np(dist):
    nf = np.maximum(dist, 1).astype(np.float32)
    large = MAX_EXACT + (np.log(nf / np.float32(MAX_EXACT)) / np.float32(math.log(MAX_DISTANCE / MAX_EXACT))
                         * np.float32(N_BUCKETS - MAX_EXACT)).astype(np.int32)
    large = np.minimum(large, N_BUCKETS - 1)
    return np.where(dist < MAX_EXACT, dist, large)


def _bucket_maps():
    i = (np.arange(2 * BLOCK) % BLOCK)[:, None]
    j = np.arange(2 * BLOCK)[None, :]
    step = BLOCK + i - j
    maps = []
    for window, dilation in DILATION_GROUPS:
        n_steps = window // dilation
        in_band = (step >= 0) & (step <= n_steps)
        bucket = _t5_bucket_np(np.clip(step, 0, n_steps) * dilation)
        maps.append(np.where(in_band, bucket, -1).astype(np.int32))
    return np.stack(maps)


def _bias_kernel(rel_ref, bm_ref, o_ref):
    g = pl.program_id(0)
    p = pl.program_id(1)
    bm = bm_ref[...]
    shape = bm.shape
    top = lax.broadcasted_iota(jnp.int32, shape, 0) < BLOCK
    col = lax.broadcasted_iota(jnp.int32, shape, 1)
    acc = jnp.zeros(shape, F32)
    for b in range(N_BUCKETS):
        val = jnp.where(top, rel_ref[b, g * N_HEADS + 2 * p], rel_ref[b, g * N_HEADS + 2 * p + 1])
        acc = jnp.where(bm == b, val, acc)
    rest = jnp.where(bm < 0, NEG_INF, acc * LOG2E)
    o_ref[1] = rest
    o_ref[0] = jnp.where(col < BLOCK, NEG_INF, rest)


def _bias_tiles(rel_bias):
    tile = (2 * BLOCK, 2 * BLOCK)
    return pl.pallas_call(
        _bias_kernel,
        grid=(N_GROUPS, N_PAIRS),
        in_specs=[
            pl.BlockSpec(memory_space=pltpu.SMEM),
            pl.BlockSpec((None,) + tile, lambda g, p: (g, 0, 0)),
        ],
        out_specs=pl.BlockSpec((None, 2, None) + tile, lambda g, p: (g, 0, p, 0, 0)),
        out_shape=jax.ShapeDtypeStruct((N_GROUPS, 2, N_PAIRS) + tile, F32),
        compiler_params=_params("arbitrary", "arbitrary"),
        name="bias_tiles",
    )(rel_bias, jnp.asarray(_bucket_maps()))


ATTN_NB = 4
ATTN_ROWS = ATTN_NB * BLOCK


def _attn_kernel(q_ref, kp_ref, kc_ref, vp_ref, vc_ref, bias_ref, o_ref, lse_ref):
    upb = q_ref.shape[0] // ATTN_NB
    rows_per_unit = q_ref.shape[1]
    not_first = jnp.minimum(pl.program_id(2), 1)
    lane = lax.broadcasted_iota(jnp.int32, (BLOCK, LANES), 1)
    lo = lane < HEAD_DIM
    ones = jnp.ones((2 * BLOCK, LANES), BF16)

    def keys_of(prev_ref, cur_ref, blk, cs):
        if blk == 0:
            return jnp.concatenate([prev_ref[:, :, cs].reshape(BLOCK, LANES),
                                    cur_ref[pl.ds(0, upb), :, cs].reshape(BLOCK, LANES)], axis=0)
        return cur_ref[pl.ds((blk - 1) * upb, 2 * upb), :, cs].reshape(2 * BLOCK, LANES)

    for blk in range(ATTN_NB):
        units = pl.ds(blk * upb, upb)
        lse_all = jnp.zeros((BLOCK, LANES), F32)
        for p in range(N_PAIRS):
            cs = pl.ds(p * LANES, LANES)
            q = q_ref[units, :, cs].reshape(BLOCK, LANES)
            zero = jnp.zeros_like(q)
            q2 = jnp.concatenate([jnp.where(lo, q, zero), jnp.where(lo, zero, q)], axis=0)
            k = keys_of(kp_ref, kc_ref, blk, cs)
            logits = lax.dot_general(q2, k, (((1,), (1,)), ((), ())), preferred_element_type=F32)
            logits = logits + (bias_ref[not_first, p] if blk == 0 else bias_ref[1, p])
            m = jnp.max(logits, axis=-1, keepdims=True)
            prob = jnp.exp2(logits - m).astype(BF16)
            acc = _dot(prob, jnp.concatenate([keys_of(vp_ref, vc_ref, blk, cs), ones], axis=1))
            o = jnp.where(lo, acc[:BLOCK, :LANES], acc[BLOCK:, :LANES])
            s = jnp.where(lo, acc[:BLOCK, LANES:], acc[BLOCK:, LANES:])
            mm = jnp.where(lo, m[:BLOCK], m[BLOCK:])
            o_ref[units, :, cs] = (o / s).astype(BF16).reshape(upb, rows_per_unit, LANES)
            lse = mm * LN2 + jnp.log(s)
            keep = (lane == 2 * p) | (lane == HEAD_DIM + 2 * p + 1)
            lse_all = jnp.where(keep, lse, lse_all)
        lse_ref[units] = lse_all.reshape(upb, rows_per_unit, LANES)


def _attn_group(qkv, bias, g, batch, seq):
    r = DILATION_GROUPS[g][1]
    rows_per_unit = min(UNIT // r, BLOCK)
    n_units = seq // (r * rows_per_unit)
    upb = BLOCK // rows_per_unit
    qkv_cols = N_QKV_TILES * D_MODEL
    view = qkv.reshape(batch, n_units, r, rows_per_unit, qkv_cols)

    def cur(which):
        return lambda b, s, c: (b, c, s, 0, 3 * g + which)

    def prev(which):
        return lambda b, s, c: (b, jnp.maximum(c * ATTN_NB - 1, 0), s, 0, 3 * g + which)

    cur_blk = (None, ATTN_NB * upb, None, rows_per_unit, D_MODEL)
    prev_blk = (None, upb, None, rows_per_unit, D_MODEL)
    o, lse = pl.pallas_call(
        _attn_kernel,
        grid=(batch, r, seq // (r * ATTN_ROWS)),
        in_specs=[
            pl.BlockSpec(cur_blk, cur(0)),
            pl.BlockSpec(prev_blk, prev(1)),
            pl.BlockSpec(cur_blk, cur(1)),
            pl.BlockSpec(prev_blk, prev(2)),
            pl.BlockSpec(cur_blk, cur(2)),
            pl.BlockSpec((None, 2, N_PAIRS, 2 * BLOCK, 2 * BLOCK), lambda b, s, c: (g, 0, 0, 0, 0)),
        ],
        out_specs=[
            pl.BlockSpec(cur_blk, lambda b, s, c: (b, c, s, 0, 0)),
            pl.BlockSpec((None, ATTN_NB * upb, None, rows_per_unit, LANES), lambda b, s, c: (b, c, s, 0, 0)),
        ],
        out_shape=[
            jax.ShapeDtypeStruct((batch, n_units, r, rows_per_unit, D_MODEL), BF16),
            jax.ShapeDtypeStruct((batch, n_units, r, rows_per_unit, LANES), F32),
        ],
        compiler_params=_params("arbitrary", "arbitrary", "arbitrary"),
        name=f"attn_r{r}",
    )(view, view, view, view, view, bias)
    return o.reshape(batch * seq, D_MODEL), lse.reshape(batch * seq, LANES)


MERGE_TM = 512


def _split3(x):
    hi = x.astype(BF16)
    r1 = x - hi.astype(F32)
    mid = r1.astype(BF16)
    lo = (r1 - mid.astype(F32)).astype(BF16)
    return hi, mid, lo


def _merge_kernel(h_ref, o0_ref, o1_ref, o2_ref, l0_ref, l1_ref, l2_ref, pt_ref, e_ref, w_ref, out_ref):
    o_refs = (o0_ref, o1_ref, o2_ref)
    l_refs = (l0_ref, l1_ref, l2_ref)
    outs = [o0_ref[...].astype(F32)]
    lses = [l0_ref[...]]
    for g in range(1, N_GROUPS):
        o_parts, l_parts = [], []
        for u in range(MERGE_TM // UNIT):
            rows = pl.ds(u * UNIT, UNIT)
            t = _dot(pt_ref[g], jnp.concatenate([o_refs[g][rows, :], *_split3(l_refs[g][rows, :])], axis=1))
            o_parts.append(t[:, :D_MODEL])
            l_parts.append(t[:, D_MODEL:D_MODEL + LANES] + t[:, D_MODEL + LANES:D_MODEL + 2 * LANES]
                           + t[:, D_MODEL + 2 * LANES:])
        outs.append(jnp.concatenate(o_parts, axis=0))
        lses.append(jnp.concatenate(l_parts, axis=0))
    m = jnp.maximum(jnp.maximum(lses[0], lses[1]), lses[2])
    es = [jnp.exp(l - m) for l in lses]
    den = es[0] + es[1] + es[2]
    merged = outs[0]
    for g in range(1, N_GROUPS):
        w = es[g] / den
        w_hi = w.astype(BF16)
        w_lo = (w - w_hi.astype(F32)).astype(BF16)
        w_full = _dot(jnp.concatenate([w_hi, w_lo], axis=1), e_ref[...])
        merged = merged + w_full * (outs[g] - outs[0])
    out_ref[...] = h_ref[...] + _dot(merged.astype(BF16), w_ref[...])


def _merge_proj(h, outs, lses, mixer, w_out):
    n_tok = h.shape[0]
    expand = np.zeros((2 * LANES, D_MODEL), np.float32)
    for head in range(N_HEADS):
        src = head if head % 2 == 0 else HEAD_DIM + head
        expand[src, head * HEAD_DIM:(head + 1) * HEAD_DIM] = 1.0
    expand[LANES:] = expand[:LANES]
    perms_t = np.transpose(_strand_perms(), (0, 2, 1))
    tok = lambda width: pl.BlockSpec((MERGE_TM, width), lambda i: (i, 0))
    return pl.pallas_call(
        _merge_kernel,
        grid=(n_tok // MERGE_TM,),
        in_specs=[tok(D_MODEL)] + [tok(D_MODEL)] * N_GROUPS + [tok(LANES)] * N_GROUPS
        + [_resident((N_GROUPS, UNIT, UNIT)), _resident((2 * LANES, D_MODEL)), _layer((D_MODEL, D_MODEL), mixer)],
        out_specs=tok(D_MODEL),
        out_shape=jax.ShapeDtypeStruct((n_tok, D_MODEL), F32),
        compiler_params=_params("arbitrary"),
        name="merge_proj",
    )(h, *outs, *lses, jnp.asarray(perms_t, BF16), jnp.asarray(expand, BF16), w_out)


def kernel(x, norm_ffn1, ffn1_w_in, ffn1_w_out, norm_mix, conv_w_in, conv_w, conv_w_out, attn_w_qkv,
           attn_q_gain, attn_k_gain, attn_w_out, rel_bias, norm_ffn2, ffn2_w_in, ffn2_w_out):
    batch, seq, d_model = x.shape
    assert d_model == D_MODEL and seq % max(QKV_TM, DILATION_GROUPS[-1][1] * ATTN_ROWS) == 0
    depth = norm_ffn1.shape[0]
    h = x.reshape(batch * seq, D_MODEL)
    bias = _bias_tiles(rel_bias) if depth > 1 else None
    gains = [a.reshape(depth, 1, D_MODEL) for a in (norm_ffn1, norm_mix, norm_ffn2)]
    ffn1_w_in, ffn1_w_out, ffn2_w_in, ffn2_w_out, conv_w_in, conv_w_out, attn_w_qkv, attn_w_out = (
        a.astype(BF16) for a in
        (ffn1_w_in, ffn1_w_out, ffn2_w_in, ffn2_w_out, conv_w_in, conv_w_out, attn_w_qkv, attn_w_out))
    for i in range(depth):
        h = _ffn(h, i, gains[0], ffn1_w_in, ffn1_w_out)
        j = i // 2
        if i % 2 == 0:
            h = _conv_mixer(h, seq, i, j, gains[1], conv_w_in, conv_w, conv_w_out)
        else:
            qkv = _qkv_proj(h, i, j, gains[1], attn_w_qkv, attn_q_gain[j], attn_k_gain[j])
            parts = [_attn_group(qkv, bias, g, batch, seq) for g in range(N_GROUPS)]
            h = _merge_proj(h, [o for o, _ in parts], [l for _, l in parts], j, attn_w_out)
        h = _ffn(h, i, gains[2], ffn2_w_in, ffn2_w_out)
    return h.reshape(batch, seq, D_MODEL)
```

```python
import functools
import math

import numpy as np
import jax
import jax.numpy as jnp
from jax import lax
from jax.experimental import pallas as pl
from jax.experimental.pallas import tpu as pltpu

D_MODEL = 1024
D_FF = 2816
CONV_WIDTH = 3
HEAD_DIM = 64
N_HEADS = D_MODEL // HEAD_DIM
N_PAIRS = N_HEADS // 2
DILATION_GROUPS = ((128, 1), (512, 4), (2048, 16))
N_GROUPS = len(DILATION_GROUPS)
BLOCK = 128
N_BUCKETS = 32
MAX_EXACT = N_BUCKETS // 2
MAX_DISTANCE = 2048
RMS_EPS = 1e-6
NEG_INF = -1e30
LOG2E = math.log2(math.e)
LN2 = math.log(2.0)

LANES = 128
MXU_DIM = 256
UNIT = MXU_DIM
VMEM_LIMIT = 56 * 1024 * 1024

F32 = jnp.float32
BF16 = jnp.bfloat16


def _rmsnorm_bf16(x, g):
    ms = jnp.mean(x * x, axis=-1, keepdims=True)
    return (x * lax.rsqrt(ms + RMS_EPS) * g).astype(BF16)


def _dot(a, b):
    return jnp.dot(a, b, preferred_element_type=F32)


def _resident(shape):
    return pl.BlockSpec(shape, lambda *_: (0,) * len(shape), pipeline_mode=pl.Buffered(1))


def _layer(shape, layer):
    return pl.BlockSpec((None,) + shape, lambda *_: (layer,) + (0,) * len(shape), pipeline_mode=pl.Buffered(1))


def _params(*sem):
    return pltpu.CompilerParams(dimension_semantics=sem, vmem_limit_bytes=VMEM_LIMIT)


assert DILATION_GROUPS[0][1] == 1


def _strand_perms():
    perms = np.zeros((N_GROUPS, UNIT, UNIT), np.float32)
    for g, (_, r) in enumerate(DILATION_GROUPS):
        tok = np.arange(UNIT)
        perms[g, (tok % r) * (UNIT // r) + tok // r, tok] = 1.0
    return perms


FFN_TM = 1024
FFN_SUB = 512
FFN_CHUNKS = ((0, 5 * MXU_DIM), (5 * MXU_DIM, 6 * MXU_DIM))


def _ffn_kernel(x_ref, g_ref, win_ref, wout_ref, o_ref):
    for t in range(FFN_TM // FFN_SUB):
        rows = pl.ds(t * FFN_SUB, FFN_SUB)
        x = x_ref[rows, :]
        xn = _rmsnorm_bf16(x, g_ref[...])
        acc = None
        for start, size in FFN_CHUNKS:
            gate = _dot(xn, win_ref[:, pl.ds(start, size)])
            up = _dot(xn, win_ref[:, pl.ds(D_FF + start, size)])
            a = (gate * jax.nn.sigmoid(gate) * up).astype(BF16)
            part = _dot(a, wout_ref[pl.ds(start, size), :])
            acc = part if acc is None else acc + part
        o_ref[rows, :] = x + 0.5 * acc


def _ffn(h, layer, g, w_in, w_out):
    n_tok = h.shape[0]
    return pl.pallas_call(
        _ffn_kernel,
        grid=(n_tok // FFN_TM,),
        in_specs=[
            pl.BlockSpec((FFN_TM, D_MODEL), lambda i: (i, 0)),
            _layer((1, D_MODEL), layer),
            _layer((D_MODEL, 2 * D_FF), layer),
            _layer((D_FF, D_MODEL), layer),
        ],
        out_specs=pl.BlockSpec((FFN_TM, D_MODEL), lambda i: (i, 0)),
        out_shape=jax.ShapeDtypeStruct((n_tok, D_MODEL), F32),
        compiler_params=_params("arbitrary"),
        name="ffn",
    )(h, g, w_in, w_out)


CONV_TM = 1024
CONV_SUB = 512
CARRY = 8


def _conv_kernel(tiles_per_seq, x_ref, g_ref, win_ref, cw_ref, wout_ref, o_ref, vp_ref):
    @pl.when(pl.program_id(0) % tiles_per_seq == 0)
    def _():
        vp_ref[pl.ds(0, CARRY), :] = jnp.zeros((CARRY, D_MODEL), F32)

    for t in range(CONV_TM // CONV_SUB):
        rows = pl.ds(t * CONV_SUB, CONV_SUB)
        x = x_ref[rows, :]
        xn = _rmsnorm_bf16(x, g_ref[...])
        b_gate = _dot(xn, win_ref[:, pl.ds(0, D_MODEL)])
        c_gate = _dot(xn, win_ref[:, pl.ds(D_MODEL, D_MODEL)])
        u = _dot(xn, win_ref[:, pl.ds(2 * D_MODEL, D_MODEL)])
        v = c_gate * u
        vp_ref[pl.ds(CARRY, CONV_SUB), :] = v
        conv = cw_ref[pl.ds(0, 1), :] * v
        for lag in range(1, CONV_WIDTH):
            conv = conv + cw_ref[pl.ds(lag, 1), :] * vp_ref[pl.ds(CARRY - lag, CONV_SUB), :]
        z = (b_gate * conv).astype(BF16)
        o_ref[rows, :] = x + _dot(z, wout_ref[...])
        vp_ref[pl.ds(0, CARRY), :] = vp_ref[pl.ds(CONV_SUB, CARRY), :]


def _conv_mixer(h, seq, layer, mixer, g, w_in, w_conv, w_out):
    n_tok = h.shape[0]
    return pl.pallas_call(
        functools.partial(_conv_kernel, seq // CONV_TM),
        grid=(n_tok // CONV_TM,),
        in_specs=[
            pl.BlockSpec((CONV_TM, D_MODEL), lambda i: (i, 0)),
            _layer((1, D_MODEL), layer),
            _layer((D_MODEL, 3 * D_MODEL), mixer),
            _layer((CONV_WIDTH, D_MODEL), mixer),
            _layer((D_MODEL, D_MODEL), mixer),
        ],
        out_specs=pl.BlockSpec((CONV_TM, D_MODEL), lambda i: (i, 0)),
        out_shape=jax.ShapeDtypeStruct((n_tok, D_MODEL), F32),
        scratch_shapes=[pltpu.VMEM((CARRY + CONV_SUB, D_MODEL), F32)],
        compiler_params=_params("arbitrary"),
        name="conv_mixer",
    )(h, g, w_in, w_conv, w_out)


QKV_TM = 1024
QKV_TN = 2 * MXU_DIM
N_QKV_TILES = 3 * N_GROUPS


def _qkv_kernel(x_ref, g_ref, w_ref, gain_ref, bd_ref, perm_ref, o_ref, xn_ref, xp_ref):
    @pl.when(pl.program_id(1) == 0)
    def _():
        xn_ref[...] = _rmsnorm_bf16(x_ref[...], g_ref[...])

    @pl.when(pl.program_id(1) == 0)
    def _():
        xp_ref[...] = xn_ref[...]

    @pl.when(pl.program_id(1) > 0)
    def _():
        for u in range(QKV_TM // UNIT):
            rows = pl.ds(u * UNIT, UNIT)
            xp_ref[rows, :] = _dot(perm_ref[...], xn_ref[rows, :]).astype(BF16)

    for c in range(3 * D_MODEL // QKV_TN):
        cols = pl.ds(c * QKV_TN, QKV_TN)
        y = _dot(xp_ref[...], w_ref[:, cols])
        if c * QKV_TN >= 2 * D_MODEL:
            res = y.astype(BF16)
        else:
            y2 = (y * y).astype(BF16)
            ms = jnp.concatenate(
                [_dot(y2[:, t * MXU_DIM:(t + 1) * MXU_DIM], bd_ref[...]) for t in range(QKV_TN // MXU_DIM)],
                axis=1)
            res = (y * lax.rsqrt(ms + RMS_EPS) * gain_ref[:, cols]).astype(BF16)
        for t in range(QKV_TN // LANES):
            o_ref[c * (QKV_TN // LANES) + t] = res[:, t * LANES:(t + 1) * LANES]


def _qkv_proj(h, layer, mixer, g, w_qkv, q_gain, k_gain):
    n_tok = h.shape[0]
    gains = jnp.concatenate(
        [jnp.tile(q_gain, (1, N_HEADS)) * (HEAD_DIM ** -0.5 * LOG2E), jnp.tile(k_gain, (1, N_HEADS)),
         jnp.ones((N_GROUPS, D_MODEL), F32)], axis=1).reshape(N_GROUPS, 1, 3 * D_MODEL)
    seg = np.arange(MXU_DIM) // HEAD_DIM
    block_diag = jnp.asarray((seg[:, None] == seg[None, :]) / HEAD_DIM, BF16)
    return pl.pallas_call(
        _qkv_kernel,
        grid=(n_tok // QKV_TM, N_GROUPS),
        in_specs=[
            pl.BlockSpec((QKV_TM, D_MODEL), lambda i, j: (i, 0)),
            _layer((1, D_MODEL), layer),
            pl.BlockSpec((None, D_MODEL, 3 * D_MODEL), lambda i, j: (mixer, 0, j)),
            pl.BlockSpec((None, 1, 3 * D_MODEL), lambda i, j: (j, 0, 0)),
            _resident((MXU_DIM, MXU_DIM)),
            pl.BlockSpec((None, UNIT, UNIT), lambda i, j: (j, 0, 0)),
        ],
        out_specs=pl.BlockSpec((3 * N_PAIRS, QKV_TM, LANES), lambda i, j: (j, i, 0)),
        out_shape=jax.ShapeDtypeStruct((N_QKV_TILES * N_PAIRS, n_tok, LANES), BF16),
        scratch_shapes=[pltpu.VMEM((QKV_TM, D_MODEL), BF16), pltpu.VMEM((QKV_TM, D_MODEL), BF16)],
        compiler_params=_params("arbitrary", "arbitrary"),
        name="qkv_proj",
    )(h, g, w_qkv, gains, block_diag, jnp.asarray(_strand_perms(), BF16))


def _t5_bucket_np(dist):
    nf = np.maximum(dist, 1).astype(np.float32)
    large = MAX_EXACT + (np.log(nf / np.float32(MAX_EXACT)) / np.float32(math.log(MAX_DISTANCE / MAX_EXACT))
                         * np.float32(N_BUCKETS - MAX_EXACT)).astype(np.int32)
    large = np.minimum(large, N_BUCKETS - 1)
    return np.where(dist < MAX_EXACT, dist, large)


def _bucket_maps():
    i = (np.arange(2 * BLOCK) % BLOCK)[:, None]
    j = np.arange(2 * BLOCK)[None, :]
    step = BLOCK + i - j
    maps = []
    for window, dilation in DILATION_GROUPS:
        n_steps = window // dilation
        in_band = (step >= 0) & (step <= n_steps)
        bucket = _t5_bucket_np(np.clip(step, 0, n_steps) * dilation)
        maps.append(np.where(in_band, bucket, -1).astype(np.int32))
    return np.stack(maps)


def _bias_kernel(rel_ref, bm_ref, o_ref):
    g = pl.program_id(0)
    p = pl.program_id(1)
    bm = bm_ref[...]
    shape = bm.shape
    top = lax.broadcasted_iota(jnp.int32, shape, 0) < BLOCK
    col = lax.broadcasted_iota(jnp.int32, shape, 1)
    acc = jnp.zeros(shape, F32)
    for b in range(N_BUCKETS):
        val = jnp.where(top, rel_ref[b, g * N_HEADS + 2 * p], rel_ref[b, g * N_HEADS + 2 * p + 1])
        acc = jnp.where(bm == b, val, acc)
    rest = jnp.where(bm < 0, NEG_INF, acc * LOG2E)
    o_ref[1] = rest
    o_ref[0] = jnp.where(col < BLOCK, NEG_INF, rest)


def _bias_tiles(rel_bias):
    tile = (2 * BLOCK, 2 * BLOCK)
    return pl.pallas_call(
        _bias_kernel,
        grid=(N_GROUPS, N_PAIRS),
        in_specs=[
            pl.BlockSpec(memory_space=pltpu.SMEM),
            pl.BlockSpec((None,) + tile, lambda g, p: (g, 0, 0)),
        ],
        out_specs=pl.BlockSpec((None, 2, None) + tile, lambda g, p: (g, 0, p, 0, 0)),
        out_shape=jax.ShapeDtypeStruct((N_GROUPS, 2, N_PAIRS) + tile, F32),
        compiler_params=_params("arbitrary", "arbitrary"),
        name="bias_tiles",
    )(rel_bias, jnp.asarray(_bucket_maps()))


ATTN_BLOCKS = 8


def _attn_kernel(nb, q_ref, kp_ref, kc_ref, vp_ref, vc_ref, bias_ref, o_ref, lse_ref):
    upb = q_ref.shape[1] // nb
    n_strands, rows_per_unit = q_ref.shape[2], q_ref.shape[3]
    not_first = jnp.minimum(pl.program_id(2), 1)
    lane = lax.broadcasted_iota(jnp.int32, (BLOCK, LANES), 1)
    lo = lane < HEAD_DIM
    ones = jnp.ones((2 * BLOCK, LANES), BF16)

    def keys_of(prev_ref, cur_ref, blk, p, j):
        if blk == 0:
            return jnp.concatenate([prev_ref[p, :, j].reshape(BLOCK, LANES),
                                    cur_ref[p, pl.ds(0, upb), j].reshape(BLOCK, LANES)], axis=0)
        return cur_ref[p, pl.ds((blk - 1) * upb, 2 * upb), j].reshape(2 * BLOCK, LANES)

    for j in range(n_strands):
        for blk in range(nb):
            units = pl.ds(blk * upb, upb)
            m_all = jnp.zeros((BLOCK, LANES), F32)
            s_all = jnp.ones((BLOCK, LANES), F32)
            for p in range(N_PAIRS):
                q = q_ref[p, units, j].reshape(BLOCK, LANES)
                zero = jnp.zeros_like(q)
                q2 = jnp.concatenate([jnp.where(lo, q, zero), jnp.where(lo, zero, q)], axis=0)
                k = keys_of(kp_ref, kc_ref, blk, p, j)
                logits = lax.dot_general(q2, k, (((1,), (1,)), ((), ())), preferred_element_type=F32)
                logits = logits + (bias_ref[not_first, p] if blk == 0 else bias_ref[1, p])
                m = jnp.max(logits, axis=-1, keepdims=True)
                prob = jnp.exp2(logits - m).astype(BF16)
                acc = _dot(prob, jnp.concatenate([keys_of(vp_ref, vc_ref, blk, p, j), ones], axis=1))
                o = jnp.where(lo, acc[:BLOCK, :LANES], acc[BLOCK:, :LANES])
                s = jnp.where(lo, acc[:BLOCK, LANES:], acc[BLOCK:, LANES:])
                mm = jnp.where(lo, m[:BLOCK], m[BLOCK:])
                o_ref[p, units, j] = (o / s).astype(BF16).reshape(upb, rows_per_unit, LANES)
                keep = (lane == 2 * p) | (lane == HEAD_DIM + 2 * p + 1)
                m_all = jnp.where(keep, mm, m_all)
                s_all = jnp.where(keep, s, s_all)
            lse_ref[units, j] = (m_all * LN2 + jnp.log(s_all)).reshape(upb, rows_per_unit, LANES)


def _attn_group(qkv, bias, g, batch, seq):
    r = DILATION_GROUPS[g][1]
    rows_per_unit = min(UNIT // r, BLOCK)
    n_units = seq // (r * rows_per_unit)
    upb = BLOCK // rows_per_unit
    nb = min(ATTN_BLOCKS, seq // (r * BLOCK))
    n_strands = ATTN_BLOCKS // nb
    view = qkv.reshape(N_QKV_TILES * N_PAIRS, batch, n_units, r, rows_per_unit, LANES)

    def cur(which):
        return lambda b, s, c: (3 * g + which, b, c, s, 0, 0)

    def prev(which):
        return lambda b, s, c: (3 * g + which, b, jnp.maximum(c * nb - 1, 0), s, 0, 0)

    cur_blk = (N_PAIRS, None, nb * upb, n_strands, rows_per_unit, LANES)
    prev_blk = (N_PAIRS, None, upb, n_strands, rows_per_unit, LANES)
    o, lse = pl.pallas_call(
        functools.partial(_attn_kernel, nb),
        grid=(batch, r // n_strands, seq // (r * nb * BLOCK)),
        in_specs=[
            pl.BlockSpec(cur_blk, cur(0)),
            pl.BlockSpec(prev_blk, prev(1)),
            pl.BlockSpec(cur_blk, cur(1)),
            pl.BlockSpec(prev_blk, prev(2)),
            pl.BlockSpec(cur_blk, cur(2)),
            pl.BlockSpec((None, 2, N_PAIRS, 2 * BLOCK, 2 * BLOCK), lambda b, s, c: (g, 0, 0, 0, 0)),
        ],
        out_specs=[
            pl.BlockSpec(cur_blk, lambda b, s, c: (0, b, c, s, 0, 0)),
            pl.BlockSpec((None, nb * upb, n_strands, rows_per_unit, LANES), lambda b, s, c: (b, c, s, 0, 0)),
        ],
        out_shape=[
            jax.ShapeDtypeStruct((N_PAIRS, batch, n_units, r, rows_per_unit, LANES), BF16),
            jax.ShapeDtypeStruct((batch, n_units, r, rows_per_unit, LANES), F32),
        ],
        compiler_params=_params("arbitrary", "arbitrary", "arbitrary"),
        name=f"attn_r{r}",
    )(view, view, view, view, view, bias)
    return o.reshape(N_PAIRS, batch * seq, LANES), lse.reshape(batch * seq, LANES)


MERGE_TM = 512


def _split3(x):
    hi = x.astype(BF16)
    r1 = x - hi.astype(F32)
    mid = r1.astype(BF16)
    lo = (r1 - mid.astype(F32)).astype(BF16)
    return hi, mid, lo


def _merge_kernel(h_ref, o0_ref, o1_ref, o2_ref, l0_ref, l1_ref, l2_ref, pt_ref, e_ref, w_ref, out_ref):
    o_refs = (o0_ref, o1_ref, o2_ref)
    l_refs = (l0_ref, l1_ref, l2_ref)
    def slabs(o_ref, rows):
        return [o_ref[p, rows, :] for p in range(N_PAIRS)]

    outs = [jnp.concatenate(slabs(o0_ref, slice(None)), axis=1).astype(F32)]
    lses = [l0_ref[...]]
    for g in range(1, N_GROUPS):
        o_parts, l_parts = [], []
        for u in range(MERGE_TM // UNIT):
            rows = pl.ds(u * UNIT, UNIT)
            t = _dot(pt_ref[g], jnp.concatenate(slabs(o_refs[g], rows) + list(_split3(l_refs[g][rows, :])), axis=1))
            o_parts.append(t[:, :D_MODEL])
            l_parts.append(t[:, D_MODEL:D_MODEL + LANES] + t[:, D_MODEL + LANES:D_MODEL + 2 * LANES]
                           + t[:, D_MODEL + 2 * LANES:])
        outs.append(jnp.concatenate(o_parts, axis=0))
        lses.append(jnp.concatenate(l_parts, axis=0))
    m = jnp.maximum(jnp.maximum(lses[0], lses[1]), lses[2])
    es = [jnp.exp(l - m) for l in lses]
    den = es[0] + es[1] + es[2]
    merged = outs[0]
    for g in range(1, N_GROUPS):
        w = es[g] / den
        w_hi = w.astype(BF16)
        w_lo = (w - w_hi.astype(F32)).astype(BF16)
        w_full = _dot(jnp.concatenate([w_hi, w_lo], axis=1), e_ref[...])
        merged = merged + w_full * (outs[g] - outs[0])
    out_ref[...] = h_ref[...] + _dot(merged.astype(BF16), w_ref[...])


def _merge_proj(h, outs, lses, mixer, w_out):
    n_tok = h.shape[0]
    expand = np.zeros((2 * LANES, D_MODEL), np.float32)
    for head in range(N_HEADS):
        src = head if head % 2 == 0 else HEAD_DIM + head
        expand[src, head * HEAD_DIM:(head + 1) * HEAD_DIM] = 1.0
    expand[LANES:] = expand[:LANES]
    perms_t = np.transpose(_strand_perms(), (0, 2, 1))
    tok = lambda width: pl.BlockSpec((MERGE_TM, width), lambda i: (i, 0))
    slab = pl.BlockSpec((N_PAIRS, MERGE_TM, LANES), lambda i: (0, i, 0))
    return pl.pallas_call(
        _merge_kernel,
        grid=(n_tok // MERGE_TM,),
        in_specs=[tok(D_MODEL)] + [slab] * N_GROUPS + [tok(LANES)] * N_GROUPS
        + [_resident((N_GROUPS, UNIT, UNIT)), _resident((2 * LANES, D_MODEL)), _layer((D_MODEL, D_MODEL), mixer)],
        out_specs=tok(D_MODEL),
        out_shape=jax.ShapeDtypeStruct((n_tok, D_MODEL), F32),
        compiler_params=_params("arbitrary"),
        name="merge_proj",
    )(h, *outs, *lses, jnp.asarray(perms_t, BF16), jnp.asarray(expand, BF16), w_out)


def kernel(x, norm_ffn1, ffn1_w_in, ffn1_w_out, norm_mix, conv_w_in, conv_w, conv_w_out, attn_w_qkv,
           attn_q_gain, attn_k_gain, attn_w_out, rel_bias, norm_ffn2, ffn2_w_in, ffn2_w_out):
    batch, seq, d_model = x.shape
    assert d_model == D_MODEL and seq % max(QKV_TM, CONV_TM, DILATION_GROUPS[-1][1] * BLOCK) == 0
    depth = norm_ffn1.shape[0]
    h = x.reshape(batch * seq, D_MODEL)
    bias = _bias_tiles(rel_bias) if depth > 1 else None
    gains = [a.reshape(depth, 1, D_MODEL) for a in (norm_ffn1, norm_mix, norm_ffn2)]
    ffn1_w_in, ffn1_w_out, ffn2_w_in, ffn2_w_out, conv_w_in, conv_w_out, attn_w_qkv, attn_w_out = (
        a.astype(BF16) for a in
        (ffn1_w_in, ffn1_w_out, ffn2_w_in, ffn2_w_out, conv_w_in, conv_w_out, attn_w_qkv, attn_w_out))
    for i in range(depth):
        h = _ffn(h, i, gains[0], ffn1_w_in, ffn1_w_out)
        j = i // 2
        if i % 2 == 0:
            h = _conv_mixer(h, seq, i, j, gains[1], conv_w_in, conv_w, conv_w_out)
        else:
            qkv = _qkv_proj(h, i, j, gains[1], attn_w_qkv, attn_q_gain[j], attn_k_gain[j])
            parts = [_attn_group(qkv, bias, g, batch, seq) for g in range(N_GROUPS)]
            h = _merge_proj(h, [o for o, _ in parts], [l for _, l in parts], j, attn_w_out)
        h = _ffn(h, i, gains[2], ffn2_w_in, ffn2_w_out)
    return h.reshape(batch, seq, D_MODEL)
```

```python
import functools
import math

import numpy as np
import jax
import jax.numpy as jnp
from jax import lax
from jax.experimental import pallas as pl
from jax.experimental.pallas import tpu as pltpu

D_MODEL = 1024
D_FF = 2816
CONV_WIDTH = 3
HEAD_DIM = 64
N_HEADS = D_MODEL // HEAD_DIM
N_PAIRS = N_HEADS // 2
DILATION_GROUPS = ((128, 1), (512, 4), (2048, 16))
N_GROUPS = len(DILATION_GROUPS)
BLOCK = 128
N_BUCKETS = 32
MAX_EXACT = N_BUCKETS // 2
MAX_DISTANCE = 2048
RMS_EPS = 1e-6
NEG_INF = -1e30
LOG2E = math.log2(math.e)
LN2 = math.log(2.0)

LANES = 128
MXU_DIM = 256
UNIT = MXU_DIM
VMEM_LIMIT = 56 * 1024 * 1024

F32 = jnp.float32
BF16 = jnp.bfloat16


def _rmsnorm_bf16(x, g):
    ms = jnp.mean(x * x, axis=-1, keepdims=True)
    return (x * lax.rsqrt(ms + RMS_EPS) * g).astype(BF16)


def _dot(a, b):
    return jnp.dot(a, b, preferred_element_type=F32)


def _resident(shape):
    return pl.BlockSpec(shape, lambda *_: (0,) * len(shape), pipeline_mode=pl.Buffered(1))


def _layer(shape, layer):
    return pl.BlockSpec((None,) + shape, lambda *_: (layer,) + (0,) * len(shape), pipeline_mode=pl.Buffered(1))


def _params(*sem):
    return pltpu.CompilerParams(dimension_semantics=sem, vmem_limit_bytes=VMEM_LIMIT)


assert DILATION_GROUPS[0][1] == 1


def _strand_perms():
    perms = np.zeros((N_GROUPS, UNIT, UNIT), np.float32)
    for g, (_, r) in enumerate(DILATION_GROUPS):
        tok = np.arange(UNIT)
        perms[g, (tok % r) * (UNIT // r) + tok // r, tok] = 1.0
    return perms


FFN_TM = 1024
FFN_SUB = 512
FFN_CHUNKS = ((0, 5 * MXU_DIM), (5 * MXU_DIM, 6 * MXU_DIM))


def _ffn_kernel(x_ref, g_ref, win_ref, wout_ref, o_ref):
    for t in range(FFN_TM // FFN_SUB):
        rows = pl.ds(t * FFN_SUB, FFN_SUB)
        x = x_ref[rows, :]
        xn = _rmsnorm_bf16(x, g_ref[...])
        acc = None
        for start, size in FFN_CHUNKS:
            gate = _dot(xn, win_ref[:, pl.ds(start, size)])
            up = _dot(xn, win_ref[:, pl.ds(D_FF + start, size)])
            a = (gate * jax.nn.sigmoid(gate) * up).astype(BF16)
            part = _dot(a, wout_ref[pl.ds(start, size), :])
            acc = part if acc is None else acc + part
        o_ref[rows, :] = x + 0.5 * acc


def _ffn(h, layer, g, w_in, w_out):
    n_tok = h.shape[0]
    return pl.pallas_call(
        _ffn_kernel,
        grid=(n_tok // FFN_TM,),
        in_specs=[
            pl.BlockSpec((FFN_TM, D_MODEL), lambda i: (i, 0)),
            _layer((1, D_MODEL), layer),
            _layer((D_MODEL, 2 * D_FF), layer),
            _layer((D_FF, D_MODEL), layer),
        ],
        out_specs=pl.BlockSpec((FFN_TM, D_MODEL), lambda i: (i, 0)),
        out_shape=jax.ShapeDtypeStruct((n_tok, D_MODEL), F32),
        compiler_params=_params("arbitrary"),
        name="ffn",
    )(h, g, w_in, w_out)


CONV_TM = 1024
CONV_SUB = 512
CARRY = 8


def _conv_kernel(tiles_per_seq, x_ref, g_ref, win_ref, cw_ref, wout_ref, o_ref, vp_ref):
    @pl.when(pl.program_id(0) % tiles_per_seq == 0)
    def _():
        vp_ref[pl.ds(0, CARRY), :] = jnp.zeros((CARRY, D_MODEL), F32)

    for t in range(CONV_TM // CONV_SUB):
        rows = pl.ds(t * CONV_SUB, CONV_SUB)
        x = x_ref[rows, :]
        xn = _rmsnorm_bf16(x, g_ref[...])
        b_gate = _dot(xn, win_ref[:, pl.ds(0, D_MODEL)])
        c_gate = _dot(xn, win_ref[:, pl.ds(D_MODEL, D_MODEL)])
        u = _dot(xn, win_ref[:, pl.ds(2 * D_MODEL, D_MODEL)])
        v = c_gate * u
        vp_ref[pl.ds(CARRY, CONV_SUB), :] = v
        conv = cw_ref[pl.ds(0, 1), :] * v
        for lag in range(1, CONV_WIDTH):
            conv = conv + cw_ref[pl.ds(lag, 1), :] * vp_ref[pl.ds(CARRY - lag, CONV_SUB), :]
        z = (b_gate * conv).astype(BF16)
        o_ref[rows, :] = x + _dot(z, wout_ref[...])
        vp_ref[pl.ds(0, CARRY), :] = vp_ref[pl.ds(CONV_SUB, CARRY), :]


def _conv_mixer(h, seq, layer, mixer, g, w_in, w_conv, w_out):
    n_tok = h.shape[0]
    return pl.pallas_call(
        functools.partial(_conv_kernel, seq // CONV_TM),
        grid=(n_tok // CONV_TM,),
        in_specs=[
            pl.BlockSpec((CONV_TM, D_MODEL), lambda i: (i, 0)),
            _layer((1, D_MODEL), layer),
            _layer((D_MODEL, 3 * D_MODEL), mixer),
            _layer((CONV_WIDTH, D_MODEL), mixer),
            _layer((D_MODEL, D_MODEL), mixer),
        ],
        out_specs=pl.BlockSpec((CONV_TM, D_MODEL), lambda i: (i, 0)),
        out_shape=jax.ShapeDtypeStruct((n_tok, D_MODEL), F32),
        scratch_shapes=[pltpu.VMEM((CARRY + CONV_SUB, D_MODEL), F32)],
        compiler_params=_params("arbitrary"),
        name="conv_mixer",
    )(h, g, w_in, w_conv, w_out)


QKV_TM = 1024
QKV_SUB = 512
QKV_TN = 2 * MXU_DIM
N_QKV_TILES = 3 * N_GROUPS


def _qkv_kernel(permute, x_ref, g_ref, w_ref, gain_ref, bd_ref, perm_ref, o_ref):
    for t in range(QKV_TM // QKV_SUB):
        rows = pl.ds(t * QKV_SUB, QKV_SUB)
        xn = _rmsnorm_bf16(x_ref[rows, :], g_ref[...])
        if permute:
            xn = jnp.concatenate(
                [_dot(perm_ref[...], xn[u * UNIT:(u + 1) * UNIT]).astype(BF16) for u in range(QKV_SUB // UNIT)],
                axis=0)
        for c in range(3 * D_MODEL // QKV_TN):
            cols = pl.ds(c * QKV_TN, QKV_TN)
            y = _dot(xn, w_ref[:, cols])
            if c * QKV_TN >= 2 * D_MODEL:
                res = y.astype(BF16)
            else:
                y2 = (y * y).astype(BF16)
                ms = jnp.concatenate(
                    [_dot(y2[:, s * MXU_DIM:(s + 1) * MXU_DIM], bd_ref[...]) for s in range(QKV_TN // MXU_DIM)],
                    axis=1)
                res = (y * lax.rsqrt(ms + RMS_EPS) * gain_ref[:, cols]).astype(BF16)
            for s in range(QKV_TN // LANES):
                o_ref[c * (QKV_TN // LANES) + s, rows, :] = res[:, s * LANES:(s + 1) * LANES]


def _qkv_proj(h, layer, mixer, group, g, w_qkv, q_gain, k_gain):
    n_tok = h.shape[0]
    gains = jnp.concatenate(
        [jnp.tile(q_gain[group], N_HEADS) * (HEAD_DIM ** -0.5 * LOG2E), jnp.tile(k_gain[group], N_HEADS),
         jnp.ones((D_MODEL,), F32)]).reshape(1, 3 * D_MODEL)
    seg = np.arange(MXU_DIM) // HEAD_DIM
    block_diag = jnp.asarray((seg[:, None] == seg[None, :]) / HEAD_DIM, BF16)
    return pl.pallas_call(
        functools.partial(_qkv_kernel, DILATION_GROUPS[group][1] > 1),
        grid=(n_tok // QKV_TM,),
        in_specs=[
            pl.BlockSpec((QKV_TM, D_MODEL), lambda i: (i, 0)),
            _layer((1, D_MODEL), layer),
            pl.BlockSpec((None, D_MODEL, 3 * D_MODEL), lambda i: (mixer, 0, group), pipeline_mode=pl.Buffered(1)),
            _resident((1, 3 * D_MODEL)),
            _resident((MXU_DIM, MXU_DIM)),
            _resident((UNIT, UNIT)),
        ],
        out_specs=pl.BlockSpec((3 * N_PAIRS, QKV_TM, LANES), lambda i: (0, i, 0)),
        out_shape=jax.ShapeDtypeStruct((3 * N_PAIRS, n_tok, LANES), BF16),
        compiler_params=_params("arbitrary"),
        name=f"qkv_proj_g{group}",
    )(h, g, w_qkv, gains, block_diag, jnp.asarray(_strand_perms()[group], BF16))


def _t5_bucket_np(dist):
    nf = np.maximum(dist, 1).astype(np.float32)
    large = MAX_EXACT + (np.log(nf / np.float32(MAX_EXACT)) / np.float32(math.log(MAX_DISTANCE / MAX_EXACT))
                         * np.float32(N_BUCKETS - MAX_EXACT)).astype(np.int32)
    large = np.minimum(large, N_BUCKETS - 1)
    return np.where(dist < MAX_EXACT, dist, large)


def _bucket_maps():
    i = (np.arange(2 * BLOCK) % BLOCK)[:, None]
    j = np.arange(2 * BLOCK)[None, :]
    step = BLOCK + i - j
    maps = []
    for window, dilation in DILATION_GROUPS:
        n_steps = window // dilation
        in_band = (step >= 0) & (step <= n_steps)
        bucket = _t5_bucket_np(np.clip(step, 0, n_steps) * dilation)
        maps.append(np.where(in_band, bucket, -1).astype(np.int32))
    return np.stack(maps)


def _bias_kernel(rel_ref, bm_ref, o_ref):
    g = pl.program_id(0)
    p = pl.program_id(1)
    bm = bm_ref[...]
    shape = bm.shape
    top = lax.broadcasted_iota(jnp.int32, shape, 0) < BLOCK
    col = lax.broadcasted_iota(jnp.int32, shape, 1)
    acc = jnp.zeros(shape, F32)
    for b in range(N_BUCKETS):
        val = jnp.where(top, rel_ref[b, g * N_HEADS + 2 * p], rel_ref[b, g * N_HEADS + 2 * p + 1])
        acc = jnp.where(bm == b, val, acc)
    rest = jnp.where(bm < 0, NEG_INF, acc * LOG2E)
    o_ref[1] = rest
    o_ref[0] = jnp.where(col < BLOCK, NEG_INF, rest)


def _bias_tiles(rel_bias):
    tile = (2 * BLOCK, 2 * BLOCK)
    return pl.pallas_call(
        _bias_kernel,
        grid=(N_GROUPS, N_PAIRS),
        in_specs=[
            pl.BlockSpec(memory_space=pltpu.SMEM),
            pl.BlockSpec((None,) + tile, lambda g, p: (g, 0, 0)),
        ],
        out_specs=pl.BlockSpec((None, 2, None) + tile, lambda g, p: (g, 0, p, 0, 0)),
        out_shape=jax.ShapeDtypeStruct((N_GROUPS, 2, N_PAIRS) + tile, F32),
        compiler_params=_params("arbitrary", "arbitrary"),
        name="bias_tiles",
    )(rel_bias, jnp.asarray(_bucket_maps()))


ATTN_BLOCKS = 8


def _attn_kernel(nb, q_ref, kp_ref, kc_ref, vp_ref, vc_ref, bias_ref, o_ref, lse_ref):
    upb = q_ref.shape[1] // nb
    n_strands, rows_per_unit = q_ref.shape[2], q_ref.shape[3]
    not_first = jnp.minimum(pl.program_id(2), 1)
    lane = lax.broadcasted_iota(jnp.int32, (BLOCK, LANES), 1)
    lo = lane < HEAD_DIM
    ones = jnp.ones((2 * BLOCK, LANES), BF16)

    def keys_of(prev_ref, cur_ref, blk, p, j):
        if blk == 0:
            return jnp.concatenate([prev_ref[p, :, j].reshape(BLOCK, LANES),
                                    cur_ref[p, pl.ds(0, upb), j].reshape(BLOCK, LANES)], axis=0)
        return cur_ref[p, pl.ds((blk - 1) * upb, 2 * upb), j].reshape(2 * BLOCK, LANES)

    for j in range(n_strands):
        for blk in range(nb):
            units = pl.ds(blk * upb, upb)
            m_all = jnp.zeros((BLOCK, LANES), F32)
            s_all = jnp.ones((BLOCK, LANES), F32)
            for p in range(N_PAIRS):
                q = q_ref[p, units, j].reshape(BLOCK, LANES)
                zero = jnp.zeros_like(q)
                q2 = jnp.concatenate([jnp.where(lo, q, zero), jnp.where(lo, zero, q)], axis=0)
                k = keys_of(kp_ref, kc_ref, blk, p, j)
                logits = lax.dot_general(q2, k, (((1,), (1,)), ((), ())), preferred_element_type=F32)
                logits = logits + (bias_ref[not_first, p] if blk == 0 else bias_ref[1, p])
                m = jnp.max(logits, axis=-1, keepdims=True)
                prob = jnp.exp2(logits - m).astype(BF16)
                acc = _dot(prob, jnp.concatenate([keys_of(vp_ref, vc_ref, blk, p, j), ones], axis=1))
                o = jnp.where(lo, acc[:BLOCK, :LANES], acc[BLOCK:, :LANES])
                s = jnp.where(lo, acc[:BLOCK, LANES:], acc[BLOCK:, LANES:])
                mm = jnp.where(lo, m[:BLOCK], m[BLOCK:])
                o_ref[p, units, j] = (o / s).astype(BF16).reshape(upb, rows_per_unit, LANES)
                keep = (lane == 2 * p) | (lane == HEAD_DIM + 2 * p + 1)
                m_all = jnp.where(keep, mm, m_all)
                s_all = jnp.where(keep, s, s_all)
            lse_ref[units, j] = (m_all * LN2 + jnp.log(s_all)).reshape(upb, rows_per_unit, LANES)


def _attn_group(qkv, bias, g, batch, seq):
    r = DILATION_GROUPS[g][1]
    rows_per_unit = min(UNIT // r, BLOCK)
    n_units = seq // (r * rows_per_unit)
    upb = BLOCK // rows_per_unit
    nb = min(ATTN_BLOCKS, seq // (r * BLOCK))
    n_strands = ATTN_BLOCKS // nb
    view = qkv.reshape(3 * N_PAIRS, batch, n_units, r, rows_per_unit, LANES)

    def cur(which):
        return lambda b, s, c: (which, b, c, s, 0, 0)

    def prev(which):
        return lambda b, s, c: (which, b, jnp.maximum(c * nb - 1, 0), s, 0, 0)

    cur_blk = (N_PAIRS, None, nb * upb, n_strands, rows_per_unit, LANES)
    prev_blk = (N_PAIRS, None, upb, n_strands, rows_per_unit, LANES)
    o, lse = pl.pallas_call(
        functools.partial(_attn_kernel, nb),
        grid=(batch, r // n_strands, seq // (r * nb * BLOCK)),
        in_specs=[
            pl.BlockSpec(cur_blk, cur(0)),
            pl.BlockSpec(prev_blk, prev(1)),
            pl.BlockSpec(cur_blk, cur(1)),
            pl.BlockSpec(prev_blk, prev(2)),
            pl.BlockSpec(cur_blk, cur(2)),
            pl.BlockSpec((None, 2, N_PAIRS, 2 * BLOCK, 2 * BLOCK), lambda b, s, c: (g, 0, 0, 0, 0)),
        ],
        out_specs=[
            pl.BlockSpec(cur_blk, lambda b, s, c: (0, b, c, s, 0, 0)),
            pl.BlockSpec((None, nb * upb, n_strands, rows_per_unit, LANES), lambda b, s, c: (b, c, s, 0, 0)),
        ],
        out_shape=[
            jax.ShapeDtypeStruct((N_PAIRS, batch, n_units, r, rows_per_unit, LANES), BF16),
            jax.ShapeDtypeStruct((batch, n_units, r, rows_per_unit, LANES), F32),
        ],
        compiler_params=_params("arbitrary", "arbitrary", "arbitrary"),
        name=f"attn_r{r}",
    )(view, view, view, view, view, bias)
    return o.reshape(N_PAIRS, batch * seq, LANES), lse.reshape(batch * seq, LANES)


MERGE_TM = 512


def _split3(x):
    hi = x.astype(BF16)
    r1 = x - hi.astype(F32)
    mid = r1.astype(BF16)
    lo = (r1 - mid.astype(F32)).astype(BF16)
    return hi, mid, lo


def _merge_kernel(h_ref, o0_ref, o1_ref, o2_ref, l0_ref, l1_ref, l2_ref, pt_ref, e_ref, w_ref, out_ref):
    o_refs = (o0_ref, o1_ref, o2_ref)
    l_refs = (l0_ref, l1_ref, l2_ref)
    def slabs(o_ref, rows):
        return [o_ref[p, rows, :] for p in range(N_PAIRS)]

    outs = [jnp.concatenate(slabs(o0_ref, slice(None)), axis=1).astype(F32)]
    lses = [l0_ref[...]]
    for g in range(1, N_GROUPS):
        o_parts, l_parts = [], []
        for u in range(MERGE_TM // UNIT):
            rows = pl.ds(u * UNIT, UNIT)
            t = _dot(pt_ref[g], jnp.concatenate(slabs(o_refs[g], rows) + list(_split3(l_refs[g][rows, :])), axis=1))
            o_parts.append(t[:, :D_MODEL])
            l_parts.append(t[:, D_MODEL:D_MODEL + LANES] + t[:, D_MODEL + LANES:D_MODEL + 2 * LANES]
                           + t[:, D_MODEL + 2 * LANES:])
        outs.append(jnp.concatenate(o_parts, axis=0))
        lses.append(jnp.concatenate(l_parts, axis=0))
    m = jnp.maximum(jnp.maximum(lses[0], lses[1]), lses[2])
    es = [jnp.exp(l - m) for l in lses]
    den = es[0] + es[1] + es[2]
    merged = outs[0]
    for g in range(1, N_GROUPS):
        w = es[g] / den
        w_hi = w.astype(BF16)
        w_lo = (w - w_hi.astype(F32)).astype(BF16)
        w_full = _dot(jnp.concatenate([w_hi, w_lo], axis=1), e_ref[...])
        merged = merged + w_full * (outs[g] - outs[0])
    out_ref[...] = h_ref[...] + _dot(merged.astype(BF16), w_ref[...])


def _merge_proj(h, outs, lses, mixer, w_out):
    n_tok = h.shape[0]
    expand = np.zeros((2 * LANES, D_MODEL), np.float32)
    for head in range(N_HEADS):
        src = head if head % 2 == 0 else HEAD_DIM + head
        expand[src, head * HEAD_DIM:(head + 1) * HEAD_DIM] = 1.0
    expand[LANES:] = expand[:LANES]
    perms_t = np.transpose(_strand_perms(), (0, 2, 1))
    tok = lambda width: pl.BlockSpec((MERGE_TM, width), lambda i: (i, 0))
    slab = pl.BlockSpec((N_PAIRS, MERGE_TM, LANES), lambda i: (0, i, 0))
    return pl.pallas_call(
        _merge_kernel,
        grid=(n_tok // MERGE_TM,),
        in_specs=[tok(D_MODEL)] + [slab] * N_GROUPS + [tok(LANES)] * N_GROUPS
        + [_resident((N_GROUPS, UNIT, UNIT)), _resident((2 * LANES, D_MODEL)), _layer((D_MODEL, D_MODEL), mixer)],
        out_specs=tok(D_MODEL),
        out_shape=jax.ShapeDtypeStruct((n_tok, D_MODEL), F32),
        compiler_params=_params("arbitrary"),
        name="merge_proj",
    )(h, *outs, *lses, jnp.asarray(perms_t, BF16), jnp.asarray(expand, BF16), w_out)


def kernel(x, norm_ffn1, ffn1_w_in, ffn1_w_out, norm_mix, conv_w_in, conv_w, conv_w_out, attn_w_qkv,
           attn_q_gain, attn_k_gain, attn_w_out, rel_bias, norm_ffn2, ffn2_w_in, ffn2_w_out):
    batch, seq, d_model = x.shape
    assert d_model == D_MODEL and seq % max(QKV_TM, CONV_TM, DILATION_GROUPS[-1][1] * BLOCK) == 0
    depth = norm_ffn1.shape[0]
    h = x.reshape(batch * seq, D_MODEL)
    bias = _bias_tiles(rel_bias) if depth > 1 else None
    gains = [a.reshape(depth, 1, D_MODEL) for a in (norm_ffn1, norm_mix, norm_ffn2)]
    ffn1_w_in, ffn1_w_out, ffn2_w_in, ffn2_w_out, conv_w_in, conv_w_out, attn_w_qkv, attn_w_out = (
        a.astype(BF16) for a in
        (ffn1_w_in, ffn1_w_out, ffn2_w_in, ffn2_w_out, conv_w_in, conv_w_out, attn_w_qkv, attn_w_out))
    for i in range(depth):
        h = _ffn(h, i, gains[0], ffn1_w_in, ffn1_w_out)
        j = i // 2
        if i % 2 == 0:
            h = _conv_mixer(h, seq, i, j, gains[1], conv_w_in, conv_w, conv_w_out)
        else:
            parts = [_attn_group(_qkv_proj(h, i, j, g, gains[1], attn_w_qkv, attn_q_gain[j], attn_k_gain[j]),
                                 bias, g, batch, seq) for g in range(N_GROUPS)]
            h = _merge_proj(h, [o for o, _ in parts], [l for _, l in parts], j, attn_w_out)
        h = _ffn(h, i, gains[2], ffn2_w_in, ffn2_w_out)
    return h.reshape(batch, seq, D_MODEL)
```

```python
import functools
import math

import numpy as np
import jax
import jax.numpy as jnp
from jax import lax
from jax.experimental import pallas as pl
from jax.experimental.pallas import tpu as pltpu

D_MODEL = 1024
D_FF = 2816
CONV_WIDTH = 3
HEAD_DIM = 64
N_HEADS = D_MODEL // HEAD_DIM
N_PAIRS = N_HEADS // 2
DILATION_GROUPS = ((128, 1), (512, 4), (2048, 16))
N_GROUPS = len(DILATION_GROUPS)
BLOCK = 128
N_BUCKETS = 32
MAX_EXACT = N_BUCKETS // 2
MAX_DISTANCE = 2048
RMS_EPS = 1e-6
NEG_INF = -1e30
LOG2E = math.log2(math.e)
LN2 = math.log(2.0)

LANES = 128
MXU_DIM = 256
UNIT = MXU_DIM
VMEM_LIMIT = 56 * 1024 * 1024

F32 = jnp.float32
BF16 = jnp.bfloat16


def _rmsnorm_bf16(x, g):
    ms = jnp.mean(x * x, axis=-1, keepdims=True)
    return (x * lax.rsqrt(ms + RMS_EPS) * g).astype(BF16)


def _dot(a, b):
    return jnp.dot(a, b, preferred_element_type=F32)


def _resident(shape):
    return pl.BlockSpec(shape, lambda *_: (0,) * len(shape), pipeline_mode=pl.Buffered(1))


def _layer(shape, layer):
    return pl.BlockSpec((None,) + shape, lambda *_: (layer,) + (0,) * len(shape), pipeline_mode=pl.Buffered(1))


def _load_weights(src, dst_ref, stage_ref, sem):
    rows = stage_ref.shape[1]
    n_chunks = dst_ref.shape[0] // rows

    def copy(c):
        return pltpu.make_async_copy(src.at[pl.ds(c * rows, rows)], stage_ref.at[c % 2], sem.at[c % 2])

    copy(0).start()
    for c in range(n_chunks):
        if c + 1 < n_chunks:
            copy(c + 1).start()
        copy(c).wait()
        dst_ref[pl.ds(c * rows, rows), :] = stage_ref[c % 2].astype(BF16)


def _weight_scratch(rows, cols, stage_rows):
    assert rows % stage_rows == 0
    return [pltpu.VMEM((rows, cols), BF16), pltpu.VMEM((2, stage_rows, cols), F32)]


HBM = pl.BlockSpec(memory_space=pl.ANY)
DMA_SEMS = pltpu.SemaphoreType.DMA((2,))


def _params(*sem):
    return pltpu.CompilerParams(dimension_semantics=sem, vmem_limit_bytes=VMEM_LIMIT)


assert DILATION_GROUPS[0][1] == 1


def _strand_perms():
    perms = np.zeros((N_GROUPS, UNIT, UNIT), np.float32)
    for g, (_, r) in enumerate(DILATION_GROUPS):
        tok = np.arange(UNIT)
        perms[g, (tok % r) * (UNIT // r) + tok // r, tok] = 1.0
    return perms


FFN_TM = 1024
FFN_SUB = 512
FFN_CHUNKS = ((0, 5 * MXU_DIM), (5 * MXU_DIM, 6 * MXU_DIM))


def _ffn_kernel(layer, x_ref, g_ref, win_hbm, wout_hbm, o_ref, win_ref, win_stage, wout_ref, wout_stage, sem):
    @pl.when(pl.program_id(0) == 0)
    def _():
        _load_weights(win_hbm.at[layer], win_ref, win_stage, sem)
        _load_weights(wout_hbm.at[layer], wout_ref, wout_stage, sem)

    for t in range(FFN_TM // FFN_SUB):
        rows = pl.ds(t * FFN_SUB, FFN_SUB)
        x = x_ref[rows, :]
        xn = _rmsnorm_bf16(x, g_ref[...])
        acc = None
        for start, size in FFN_CHUNKS:
            gate = _dot(xn, win_ref[:, pl.ds(start, size)])
            up = _dot(xn, win_ref[:, pl.ds(D_FF + start, size)])
            a = (gate * jax.nn.sigmoid(gate) * up).astype(BF16)
            part = _dot(a, wout_ref[pl.ds(start, size), :])
            acc = part if acc is None else acc + part
        o_ref[rows, :] = x + 0.5 * acc


def _ffn(h, layer, g, w_in, w_out):
    n_tok = h.shape[0]
    return pl.pallas_call(
        functools.partial(_ffn_kernel, layer),
        grid=(n_tok // FFN_TM,),
        in_specs=[pl.BlockSpec((FFN_TM, D_MODEL), lambda i: (i, 0)), _layer((1, D_MODEL), layer), HBM, HBM],
        out_specs=pl.BlockSpec((FFN_TM, D_MODEL), lambda i: (i, 0)),
        out_shape=jax.ShapeDtypeStruct((n_tok, D_MODEL), F32),
        scratch_shapes=_weight_scratch(D_MODEL, 2 * D_FF, LANES) + _weight_scratch(D_FF, D_MODEL, D_FF // 8)
        + [DMA_SEMS],
        compiler_params=_params("arbitrary"),
        name="ffn",
    )(h, g, w_in, w_out)


CONV_TM = 1024
CONV_SUB = 512
CARRY = 8


def _conv_kernel(tiles_per_seq, mixer, x_ref, g_ref, win_hbm, cw_ref, wout_hbm, o_ref, vp_ref,
                 win_ref, win_stage, wout_ref, wout_stage, sem):
    @pl.when(pl.program_id(0) == 0)
    def _():
        _load_weights(win_hbm.at[mixer], win_ref, win_stage, sem)
        _load_weights(wout_hbm.at[mixer], wout_ref, wout_stage, sem)

    @pl.when(pl.program_id(0) % tiles_per_seq == 0)
    def _():
        vp_ref[pl.ds(0, CARRY), :] = jnp.zeros((CARRY, D_MODEL), F32)

    for t in range(CONV_TM // CONV_SUB):
        rows = pl.ds(t * CONV_SUB, CONV_SUB)
        x = x_ref[rows, :]
        xn = _rmsnorm_bf16(x, g_ref[...])
        b_gate = _dot(xn, win_ref[:, pl.ds(0, D_MODEL)])
        c_gate = _dot(xn, win_ref[:, pl.ds(D_MODEL, D_MODEL)])
        u = _dot(xn, win_ref[:, pl.ds(2 * D_MODEL, D_MODEL)])
        v = c_gate * u
        vp_ref[pl.ds(CARRY, CONV_SUB), :] = v
        conv = cw_ref[pl.ds(0, 1), :] * v
        for lag in range(1, CONV_WIDTH):
            conv = conv + cw_ref[pl.ds(lag, 1), :] * vp_ref[pl.ds(CARRY - lag, CONV_SUB), :]
        z = (b_gate * conv).astype(BF16)
        o_ref[rows, :] = x + _dot(z, wout_ref[...])
        vp_ref[pl.ds(0, CARRY), :] = vp_ref[pl.ds(CONV_SUB, CARRY), :]


def _conv_mixer(h, seq, layer, mixer, g, w_in, w_conv, w_out):
    n_tok = h.shape[0]
    return pl.pallas_call(
        functools.partial(_conv_kernel, seq // CONV_TM, mixer),
        grid=(n_tok // CONV_TM,),
        in_specs=[
            pl.BlockSpec((CONV_TM, D_MODEL), lambda i: (i, 0)),
            _layer((1, D_MODEL), layer),
            HBM,
            _layer((CONV_WIDTH, D_MODEL), mixer),
            HBM,
        ],
        out_specs=pl.BlockSpec((CONV_TM, D_MODEL), lambda i: (i, 0)),
        out_shape=jax.ShapeDtypeStruct((n_tok, D_MODEL), F32),
        scratch_shapes=[pltpu.VMEM((CARRY + CONV_SUB, D_MODEL), F32)]
        + _weight_scratch(D_MODEL, 3 * D_MODEL, LANES) + _weight_scratch(D_MODEL, D_MODEL, MXU_DIM) + [DMA_SEMS],
        compiler_params=_params("arbitrary"),
        name="conv_mixer",
    )(h, g, w_in, w_conv, w_out)


QKV_TM = 1024
QKV_SUB = 512
QKV_TN = 2 * MXU_DIM
N_QKV_TILES = 3 * N_GROUPS


def _qkv_kernel(permute, mixer, group, x_ref, g_ref, w_hbm, gain_ref, bd_ref, perm_ref, o_ref, w_ref, w_stage, sem):
    @pl.when(pl.program_id(0) == 0)
    def _():
        _load_weights(w_hbm.at[mixer, :, pl.ds(group * 3 * D_MODEL, 3 * D_MODEL)], w_ref, w_stage, sem)

    for t in range(QKV_TM // QKV_SUB):
        rows = pl.ds(t * QKV_SUB, QKV_SUB)
        xn = _rmsnorm_bf16(x_ref[rows, :], g_ref[...])
        if permute:
            xn = jnp.concatenate(
                [_dot(perm_ref[...], xn[u * UNIT:(u + 1) * UNIT]).astype(BF16) for u in range(QKV_SUB // UNIT)],
                axis=0)
        for c in range(3 * D_MODEL // QKV_TN):
            cols = pl.ds(c * QKV_TN, QKV_TN)
            y = _dot(xn, w_ref[:, cols])
            if c * QKV_TN >= 2 * D_MODEL:
                res = y.astype(BF16)
            else:
                y2 = (y * y).astype(BF16)
                ms = jnp.concatenate(
                    [_dot(y2[:, s * MXU_DIM:(s + 1) * MXU_DIM], bd_ref[...]) for s in range(QKV_TN // MXU_DIM)],
                    axis=1)
                res = (y * lax.rsqrt(ms + RMS_EPS) * gain_ref[:, cols]).astype(BF16)
            for s in range(QKV_TN // LANES):
                o_ref[c * (QKV_TN // LANES) + s, rows, :] = res[:, s * LANES:(s + 1) * LANES]


def _qkv_proj(h, layer, mixer, group, g, w_qkv, q_gain, k_gain):
    n_tok = h.shape[0]
    gains = jnp.concatenate(
        [jnp.tile(q_gain[group], N_HEADS) * (HEAD_DIM ** -0.5 * LOG2E), jnp.tile(k_gain[group], N_HEADS),
         jnp.ones((D_MODEL,), F32)]).reshape(1, 3 * D_MODEL)
    seg = np.arange(MXU_DIM) // HEAD_DIM
    block_diag = jnp.asarray((seg[:, None] == seg[None, :]) / HEAD_DIM, BF16)
    return pl.pallas_call(
        functools.partial(_qkv_kernel, DILATION_GROUPS[group][1] > 1, mixer, group),
        grid=(n_tok // QKV_TM,),
        in_specs=[
            pl.BlockSpec((QKV_TM, D_MODEL), lambda i: (i, 0)),
            _layer((1, D_MODEL), layer),
            HBM,
            _resident((1, 3 * D_MODEL)),
            _resident((MXU_DIM, MXU_DIM)),
            _resident((UNIT, UNIT)),
        ],
        out_specs=pl.BlockSpec((3 * N_PAIRS, QKV_TM, LANES), lambda i: (0, i, 0)),
        out_shape=jax.ShapeDtypeStruct((3 * N_PAIRS, n_tok, LANES), BF16),
        scratch_shapes=_weight_scratch(D_MODEL, 3 * D_MODEL, LANES) + [DMA_SEMS],
        compiler_params=_params("arbitrary"),
        name=f"qkv_proj_g{group}",
    )(h, g, w_qkv, gains, block_diag, jnp.asarray(_strand_perms()[group], BF16))


def _t5_bucket_np(dist):
    nf = np.maximum(dist, 1).astype(np.float32)
    large = MAX_EXACT + (np.log(nf / np.float32(MAX_EXACT)) / np.float32(math.log(MAX_DISTANCE / MAX_EXACT))
                         * np.float32(N_BUCKETS - MAX_EXACT)).astype(np.int32)
    large = np.minimum(large, N_BUCKETS - 1)
    return np.where(dist < MAX_EXACT, dist, large)


def _bucket_maps():
    i = (np.arange(2 * BLOCK) % BLOCK)[:, None]
    j = np.arange(2 * BLOCK)[None, :]
    step = BLOCK + i - j
    maps = []
    for window, dilation in DILATION_GROUPS:
        n_steps = window // dilation
        in_band = (step >= 0) & (step <= n_steps)
        bucket = _t5_bucket_np(np.clip(step, 0, n_steps) * dilation)
        maps.append(np.where(in_band, bucket, -1).astype(np.int32))
    return np.stack(maps)


def _bias_kernel(buckets, rel_ref, bm_ref, o_ref):
    p = pl.program_id(0)
    shape = bm_ref.shape[1:]
    top = lax.broadcasted_iota(jnp.int32, shape, 0) < BLOCK
    col = lax.broadcasted_iota(jnp.int32, shape, 1)
    for g in range(N_GROUPS):
        bm = bm_ref[g]
        acc = jnp.zeros(shape, F32)
        for b in buckets[g]:
            val = jnp.where(top, rel_ref[b, g * N_HEADS + 2 * p], rel_ref[b, g * N_HEADS + 2 * p + 1])
            acc = jnp.where(bm == b, val, acc)
        rest = jnp.where(bm < 0, NEG_INF, acc * LOG2E)
        o_ref[g, 1] = rest
        o_ref[g, 0] = jnp.where(col < BLOCK, NEG_INF, rest)


def _bias_tiles(rel_bias):
    tile = (2 * BLOCK, 2 * BLOCK)
    maps = _bucket_maps()
    buckets = tuple(tuple(int(b) for b in np.unique(m[m >= 0])) for m in maps)
    return pl.pallas_call(
        functools.partial(_bias_kernel, buckets),
        grid=(N_PAIRS,),
        in_specs=[pl.BlockSpec(memory_space=pltpu.SMEM), _resident((N_GROUPS,) + tile)],
        out_specs=pl.BlockSpec((N_GROUPS, 2, None) + tile, lambda p: (0, 0, p, 0, 0)),
        out_shape=jax.ShapeDtypeStruct((N_GROUPS, 2, N_PAIRS) + tile, F32),
        compiler_params=_params("arbitrary"),
        name="bias_tiles",
    )(rel_bias, jnp.asarray(maps))


ATTN_BLOCKS = 8


def _attn_kernel(nb, q_ref, kp_ref, kc_ref, vp_ref, vc_ref, bias_ref, o_ref, lse_ref):
    upb = q_ref.shape[1] // nb
    n_strands, rows_per_unit = q_ref.shape[2], q_ref.shape[3]
    not_first = jnp.minimum(pl.program_id(2), 1)
    lane = lax.broadcasted_iota(jnp.int32, (BLOCK, LANES), 1)
    lo = lane < HEAD_DIM
    ones = jnp.ones((2 * BLOCK, LANES), BF16)

    def keys_of(prev_ref, cur_ref, blk, p, j):
        if blk == 0:
            return jnp.concatenate([prev_ref[p, :, j].reshape(BLOCK, LANES),
                                    cur_ref[p, pl.ds(0, upb), j].reshape(BLOCK, LANES)], axis=0)
        return cur_ref[p, pl.ds((blk - 1) * upb, 2 * upb), j].reshape(2 * BLOCK, LANES)

    for j in range(n_strands):
        for blk in range(nb):
            units = pl.ds(blk * upb, upb)
            m_all = jnp.zeros((BLOCK, LANES), F32)
            s_all = jnp.ones((BLOCK, LANES), F32)
            for p in range(N_PAIRS):
                q = q_ref[p, units, j].reshape(BLOCK, LANES)
                zero = jnp.zeros_like(q)
                q2 = jnp.concatenate([jnp.where(lo, q, zero), jnp.where(lo, zero, q)], axis=0)
                k = keys_of(kp_ref, kc_ref, blk, p, j)
                logits = lax.dot_general(q2, k, (((1,), (1,)), ((), ())), preferred_element_type=F32)
                logits = logits + (bias_ref[not_first, p] if blk == 0 else bias_ref[1, p])
                m = jnp.max(logits, axis=-1, keepdims=True)
                prob = jnp.exp2(logits - m).astype(BF16)
                acc = _dot(prob, jnp.concatenate([keys_of(vp_ref, vc_ref, blk, p, j), ones], axis=1))
                o = jnp.where(lo, acc[:BLOCK, :LANES], acc[BLOCK:, :LANES])
                s = jnp.where(lo, acc[:BLOCK, LANES:], acc[BLOCK:, LANES:])
                mm = jnp.where(lo, m[:BLOCK], m[BLOCK:])
                o_ref[p, units, j] = (o / s).astype(BF16).reshape(upb, rows_per_unit, LANES)
                keep = (lane == 2 * p) | (lane == HEAD_DIM + 2 * p + 1)
                m_all = jnp.where(keep, mm, m_all)
                s_all = jnp.where(keep, s, s_all)
            lse_ref[units, j] = (m_all * LN2 + jnp.log(s_all)).reshape(upb, rows_per_unit, LANES)


def _attn_group(qkv, bias, g, batch, seq):
    r = DILATION_GROUPS[g][1]
    rows_per_unit = min(UNIT // r, BLOCK)
    n_units = seq // (r * rows_per_unit)
    upb = BLOCK // rows_per_unit
    nb = min(ATTN_BLOCKS, seq // (r * BLOCK))
    n_strands = ATTN_BLOCKS // nb
    view = qkv.reshape(3 * N_PAIRS, batch, n_units, r, rows_per_unit, LANES)

    def cur(which):
        return lambda b, s, c: (which, b, c, s, 0, 0)

    def prev(which):
        return lambda b, s, c: (which, b, jnp.maximum(c * nb - 1, 0), s, 0, 0)

    cur_blk = (N_PAIRS, None, nb * upb, n_strands, rows_per_unit, LANES)
    prev_blk = (N_PAIRS, None, upb, n_strands, rows_per_unit, LANES)
    o, lse = pl.pallas_call(
        functools.partial(_attn_kernel, nb),
        grid=(batch, r // n_strands, seq // (r * nb * BLOCK)),
        in_specs=[
            pl.BlockSpec(cur_blk, cur(0)),
            pl.BlockSpec(prev_blk, prev(1)),
            pl.BlockSpec(cur_blk, cur(1)),
            pl.BlockSpec(prev_blk, prev(2)),
            pl.BlockSpec(cur_blk, cur(2)),
            pl.BlockSpec((None, 2, N_PAIRS, 2 * BLOCK, 2 * BLOCK), lambda b, s, c: (g, 0, 0, 0, 0)),
        ],
        out_specs=[
            pl.BlockSpec(cur_blk, lambda b, s, c: (0, b, c, s, 0, 0)),
            pl.BlockSpec((None, nb * upb, n_strands, rows_per_unit, LANES), lambda b, s, c: (b, c, s, 0, 0)),
        ],
        out_shape=[
            jax.ShapeDtypeStruct((N_PAIRS, batch, n_units, r, rows_per_unit, LANES), BF16),
            jax.ShapeDtypeStruct((batch, n_units, r, rows_per_unit, LANES), F32),
        ],
        compiler_params=_params("arbitrary", "arbitrary", "arbitrary"),
        name=f"attn_r{r}",
    )(view, view, view, view, view, bias)
    return o.reshape(N_PAIRS, batch * seq, LANES), lse.reshape(batch * seq, LANES)


MERGE_TM = 512


def _split3(x):
    hi = x.astype(BF16)
    r1 = x - hi.astype(F32)
    mid = r1.astype(BF16)
    lo = (r1 - mid.astype(F32)).astype(BF16)
    return hi, mid, lo


def _merge_kernel(mixer, h_ref, o0_ref, o1_ref, o2_ref, l0_ref, l1_ref, l2_ref, pt_ref, e_ref, w_hbm, out_ref,
                  w_ref, w_stage, sem):
    @pl.when(pl.program_id(0) == 0)
    def _():
        _load_weights(w_hbm.at[mixer], w_ref, w_stage, sem)

    o_refs = (o0_ref, o1_ref, o2_ref)
    l_refs = (l0_ref, l1_ref, l2_ref)
    def slabs(o_ref, rows):
        return [o_ref[p, rows, :] for p in range(N_PAIRS)]

    outs = [jnp.concatenate(slabs(o0_ref, slice(None)), axis=1).astype(F32)]
    lses = [l0_ref[...]]
    for g in range(1, N_GROUPS):
        o_parts, l_parts = [], []
        for u in range(MERGE_TM // UNIT):
            rows = pl.ds(u * UNIT, UNIT)
            t = _dot(pt_ref[g], jnp.concatenate(slabs(o_refs[g], rows) + list(_split3(l_refs[g][rows, :])), axis=1))
            o_parts.append(t[:, :D_MODEL])
            l_parts.append(t[:, D_MODEL:D_MODEL + LANES] + t[:, D_MODEL + LANES:D_MODEL + 2 * LANES]
                           + t[:, D_MODEL + 2 * LANES:])
        outs.append(jnp.concatenate(o_parts, axis=0))
        lses.append(jnp.concatenate(l_parts, axis=0))
    m = jnp.maximum(jnp.maximum(lses[0], lses[1]), lses[2])
    es = [jnp.exp(l - m) for l in lses]
    den = es[0] + es[1] + es[2]
    merged = outs[0]
    for g in range(1, N_GROUPS):
        w = es[g] / den
        w_hi = w.astype(BF16)
        w_lo = (w - w_hi.astype(F32)).astype(BF16)
        w_full = _dot(jnp.concatenate([w_hi, w_lo], axis=1), e_ref[...])
        merged = merged + w_full * (outs[g] - outs[0])
    out_ref[...] = h_ref[...] + _dot(merged.astype(BF16), w_ref[...])


def _merge_proj(h, outs, lses, mixer, w_out):
    n_tok = h.shape[0]
    expand = np.zeros((2 * LANES, D_MODEL), np.float32)
    for head in range(N_HEADS):
        src = head if head % 2 == 0 else HEAD_DIM + head
        expand[src, head * HEAD_DIM:(head + 1) * HEAD_DIM] = 1.0
    expand[LANES:] = expand[:LANES]
    perms_t = np.transpose(_strand_perms(), (0, 2, 1))
    tok = lambda width: pl.BlockSpec((MERGE_TM, width), lambda i: (i, 0))
    slab = pl.BlockSpec((N_PAIRS, MERGE_TM, LANES), lambda i: (0, i, 0))
    return pl.pallas_call(
        functools.partial(_merge_kernel, mixer),
        grid=(n_tok // MERGE_TM,),
        in_specs=[tok(D_MODEL)] + [slab] * N_GROUPS + [tok(LANES)] * N_GROUPS
        + [_resident((N_GROUPS, UNIT, UNIT)), _resident((2 * LANES, D_MODEL)), HBM],
        out_specs=tok(D_MODEL),
        out_shape=jax.ShapeDtypeStruct((n_tok, D_MODEL), F32),
        scratch_shapes=_weight_scratch(D_MODEL, D_MODEL, MXU_DIM) + [DMA_SEMS],
        compiler_params=_params("arbitrary"),
        name="merge_proj",
    )(h, *outs, *lses, jnp.asarray(perms_t, BF16), jnp.asarray(expand, BF16), w_out)


def kernel(x, norm_ffn1, ffn1_w_in, ffn1_w_out, norm_mix, conv_w_in, conv_w, conv_w_out, attn_w_qkv,
           attn_q_gain, attn_k_gain, attn_w_out, rel_bias, norm_ffn2, ffn2_w_in, ffn2_w_out):
    batch, seq, d_model = x.shape
    assert d_model == D_MODEL and seq % max(QKV_TM, CONV_TM, DILATION_GROUPS[-1][1] * BLOCK) == 0
    depth = norm_ffn1.shape[0]
    h = x.reshape(batch * seq, D_MODEL)
    bias = _bias_tiles(rel_bias) if depth > 1 else None
    gains = [a.reshape(depth, 1, D_MODEL) for a in (norm_ffn1, norm_mix, norm_ffn2)]
    for i in range(depth):
        h = _ffn(h, i, gains[0], ffn1_w_in, ffn1_w_out)
        j = i // 2
        if i % 2 == 0:
            h = _conv_mixer(h, seq, i, j, gains[1], conv_w_in, conv_w, conv_w_out)
        else:
            parts = [_attn_group(_qkv_proj(h, i, j, g, gains[1], attn_w_qkv, attn_q_gain[j], attn_k_gain[j]),
                                 bias, g, batch, seq) for g in range(N_GROUPS)]
            h = _merge_proj(h, [o for o, _ in parts], [l for _, l in parts], j, attn_w_out)
        h = _ffn(h, i, gains[2], ffn2_w_in, ffn2_w_out)
    return h.reshape(batch, seq, D_MODEL)
```

```python
import functools
import math

import numpy as np
import jax
import jax.numpy as jnp
from jax import lax
from jax.experimental import pallas as pl
from jax.experimental.pallas import tpu as pltpu

D_MODEL = 1024
D_FF = 2816
CONV_WIDTH = 3
HEAD_DIM = 64
N_HEADS = D_MODEL // HEAD_DIM
N_PAIRS = N_HEADS // 2
DILATION_GROUPS = ((128, 1), (512, 4), (2048, 16))
N_GROUPS = len(DILATION_GROUPS)
BLOCK = 128
N_BUCKETS = 32
MAX_EXACT = N_BUCKETS // 2
MAX_DISTANCE = 2048
RMS_EPS = 1e-6
NEG_INF = -1e30
LOG2E = math.log2(math.e)
LN2 = math.log(2.0)

LANES = 128
MXU_DIM = 256
UNIT = MXU_DIM
VMEM_LIMIT = 56 * 1024 * 1024

F32 = jnp.float32
BF16 = jnp.bfloat16


def _rmsnorm_bf16(x, g):
    ms = jnp.mean(x * x, axis=-1, keepdims=True)
    return (x * lax.rsqrt(ms + RMS_EPS) * g).astype(BF16)


def _dot(a, b):
    return jnp.dot(a, b, preferred_element_type=F32)


def _resident(shape):
    return pl.BlockSpec(shape, lambda *_: (0,) * len(shape), pipeline_mode=pl.Buffered(1))


def _layer(shape, layer):
    return pl.BlockSpec((None,) + shape, lambda *_: (layer,) + (0,) * len(shape), pipeline_mode=pl.Buffered(1))


def _load_weights(src, dst_ref, stage_ref, sem):
    rows = stage_ref.shape[1]
    n_chunks = dst_ref.shape[0] // rows

    def copy(c):
        return pltpu.make_async_copy(src.at[pl.ds(c * rows, rows)], stage_ref.at[c % 2], sem.at[c % 2])

    copy(0).start()
    for c in range(n_chunks):
        if c + 1 < n_chunks:
            copy(c + 1).start()
        copy(c).wait()
        dst_ref[pl.ds(c * rows, rows), :] = stage_ref[c % 2].astype(BF16)


def _weight_scratch(rows, cols, stage_rows):
    assert rows % stage_rows == 0
    return [pltpu.VMEM((rows, cols), BF16), pltpu.VMEM((2, stage_rows, cols), F32)]


HBM = pl.BlockSpec(memory_space=pl.ANY)
DMA_SEMS = pltpu.SemaphoreType.DMA((2,))


def _params(*sem):
    return pltpu.CompilerParams(dimension_semantics=sem, vmem_limit_bytes=VMEM_LIMIT)


assert DILATION_GROUPS[0][1] == 1

ROW_STRIDE = 4


def _stride_passes(r):
    assert r in (1, ROW_STRIDE, ROW_STRIDE ** 2)
    return {1: 1, ROW_STRIDE: 1, ROW_STRIDE ** 2: 2}[r]


FFN_TM = 1024
FFN_SUB = 512
FFN_CHUNKS = ((0, 5 * MXU_DIM), (5 * MXU_DIM, 6 * MXU_DIM))


def _ffn_kernel(layer, x_ref, g_ref, win_hbm, wout_hbm, o_ref, win_ref, win_stage, wout_ref, wout_stage, sem):
    @pl.when(pl.program_id(0) == 0)
    def _():
        _load_weights(win_hbm.at[layer], win_ref, win_stage, sem)
        _load_weights(wout_hbm.at[layer], wout_ref, wout_stage, sem)

    for t in range(FFN_TM // FFN_SUB):
        rows = pl.ds(t * FFN_SUB, FFN_SUB)
        x = x_ref[rows, :]
        xn = _rmsnorm_bf16(x, g_ref[...])
        acc = None
        for start, size in FFN_CHUNKS:
            gate = _dot(xn, win_ref[:, pl.ds(start, size)])
            up = _dot(xn, win_ref[:, pl.ds(D_FF + start, size)])
            a = (gate * jax.nn.sigmoid(gate) * up).astype(BF16)
            part = _dot(a, wout_ref[pl.ds(start, size), :])
            acc = part if acc is None else acc + part
        o_ref[rows, :] = x + 0.5 * acc


def _ffn(h, layer, g, w_in, w_out):
    n_tok = h.shape[0]
    return pl.pallas_call(
        functools.partial(_ffn_kernel, layer),
        grid=(n_tok // FFN_TM,),
        in_specs=[pl.BlockSpec((FFN_TM, D_MODEL), lambda i: (i, 0)), _layer((1, D_MODEL), layer), HBM, HBM],
        out_specs=pl.BlockSpec((FFN_TM, D_MODEL), lambda i: (i, 0)),
        out_shape=jax.ShapeDtypeStruct((n_tok, D_MODEL), F32),
        scratch_shapes=_weight_scratch(D_MODEL, 2 * D_FF, LANES) + _weight_scratch(D_FF, D_MODEL, D_FF // 8)
        + [DMA_SEMS],
        compiler_params=_params("arbitrary"),
        name="ffn",
    )(h, g, w_in, w_out)


CONV_TM = 1024
CONV_SUB = 512
CARRY = 8


def _conv_kernel(tiles_per_seq, mixer, x_ref, g_ref, win_hbm, cw_ref, wout_hbm, o_ref, vp_ref,
                 win_ref, win_stage, wout_ref, wout_stage, sem):
    @pl.when(pl.program_id(0) == 0)
    def _():
        _load_weights(win_hbm.at[mixer], win_ref, win_stage, sem)
        _load_weights(wout_hbm.at[mixer], wout_ref, wout_stage, sem)

    @pl.when(pl.program_id(0) % tiles_per_seq == 0)
    def _():
        vp_ref[pl.ds(0, CARRY), :] = jnp.zeros((CARRY, D_MODEL), F32)

    for t in range(CONV_TM // CONV_SUB):
        rows = pl.ds(t * CONV_SUB, CONV_SUB)
        x = x_ref[rows, :]
        xn = _rmsnorm_bf16(x, g_ref[...])
        b_gate = _dot(xn, win_ref[:, pl.ds(0, D_MODEL)])
        c_gate = _dot(xn, win_ref[:, pl.ds(D_MODEL, D_MODEL)])
        u = _dot(xn, win_ref[:, pl.ds(2 * D_MODEL, D_MODEL)])
        v = c_gate * u
        vp_ref[pl.ds(CARRY, CONV_SUB), :] = v
        conv = cw_ref[pl.ds(0, 1), :] * v
        for lag in range(1, CONV_WIDTH):
            conv = conv + cw_ref[pl.ds(lag, 1), :] * vp_ref[pl.ds(CARRY - lag, CONV_SUB), :]
        z = (b_gate * conv).astype(BF16)
        o_ref[rows, :] = x + _dot(z, wout_ref[...])
        vp_ref[pl.ds(0, CARRY), :] = vp_ref[pl.ds(CONV_SUB, CARRY), :]


def _conv_mixer(h, seq, layer, mixer, g, w_in, w_conv, w_out):
    n_tok = h.shape[0]
    return pl.pallas_call(
        functools.partial(_conv_kernel, seq // CONV_TM, mixer),
        grid=(n_tok // CONV_TM,),
        in_specs=[
            pl.BlockSpec((CONV_TM, D_MODEL), lambda i: (i, 0)),
            _layer((1, D_MODEL), layer),
            HBM,
            _layer((CONV_WIDTH, D_MODEL), mixer),
            HBM,
        ],
        out_specs=pl.BlockSpec((CONV_TM, D_MODEL), lambda i: (i, 0)),
        out_shape=jax.ShapeDtypeStruct((n_tok, D_MODEL), F32),
        scratch_shapes=[pltpu.VMEM((CARRY + CONV_SUB, D_MODEL), F32)]
        + _weight_scratch(D_MODEL, 3 * D_MODEL, LANES) + _weight_scratch(D_MODEL, D_MODEL, MXU_DIM) + [DMA_SEMS],
        compiler_params=_params("arbitrary"),
        name="conv_mixer",
    )(h, g, w_in, w_conv, w_out)


QKV_TM = 1024
QKV_SUB = 512
QKV_TN = 2 * MXU_DIM
N_QKV_TILES = 3 * N_GROUPS


def _qkv_kernel(mixer, group, x_ref, g_ref, w_hbm, gain_ref, bd_ref, o_ref, w_ref, w_stage, sem, xs_ref):
    @pl.when(pl.program_id(0) == 0)
    def _():
        _load_weights(w_hbm.at[mixer, :, pl.ds(group * 3 * D_MODEL, 3 * D_MODEL)], w_ref, w_stage, sem)

    r = DILATION_GROUPS[group][1]
    rows_per_strand = UNIT // r
    for t in range(QKV_TM // QKV_SUB):
        rows = pl.ds(t * QKV_SUB, QKV_SUB)
        x = x_ref[rows, :]
        xn = x * lax.rsqrt(jnp.mean(x * x, axis=-1, keepdims=True) + RMS_EPS) * g_ref[...]
        if r > 1:
            slabs = range(D_MODEL // LANES)
            units = range(QKV_SUB // UNIT)
            for p in slabs:
                xs_ref[0, t, p] = xn[:, p * LANES:(p + 1) * LANES]
            if r == ROW_STRIDE:
                pieces = [[xs_ref[0, t, p, pl.ds(u * UNIT + s, rows_per_strand, stride=r), :] for p in slabs]
                          for u in units for s in range(r)]
            else:
                quarter = UNIT // ROW_STRIDE
                for u in units:
                    for b in range(ROW_STRIDE):
                        for p in slabs:
                            xs_ref[1, t, p, pl.ds(u * UNIT + b * quarter, quarter), :] = (
                                xs_ref[0, t, p, pl.ds(u * UNIT + b, quarter, stride=ROW_STRIDE), :])
                pieces = [[xs_ref[1, t, p, pl.ds(u * UNIT + b * quarter + a, rows_per_strand, stride=ROW_STRIDE), :]
                           for p in slabs]
                          for u in units for a in range(ROW_STRIDE) for b in range(ROW_STRIDE)]
            xn = jnp.concatenate([jnp.concatenate(piece, axis=1) for piece in pieces], axis=0)
        xn = xn.astype(BF16)
        for c in range(3 * D_MODEL // QKV_TN):
            cols = pl.ds(c * QKV_TN, QKV_TN)
            y = _dot(xn, w_ref[:, cols])
            if c * QKV_TN >= 2 * D_MODEL:
                res = y.astype(BF16)
            else:
                y2 = (y * y).astype(BF16)
                ms = jnp.concatenate(
                    [_dot(y2[:, s * MXU_DIM:(s + 1) * MXU_DIM], bd_ref[...]) for s in range(QKV_TN // MXU_DIM)],
                    axis=1)
                res = (y * lax.rsqrt(ms + RMS_EPS) * gain_ref[:, cols]).astype(BF16)
            for s in range(QKV_TN // LANES):
                o_ref[c * (QKV_TN // LANES) + s, rows, :] = res[:, s * LANES:(s + 1) * LANES]


def _qkv_proj(h, layer, mixer, group, g, w_qkv, q_gain, k_gain):
    n_tok = h.shape[0]
    gains = jnp.concatenate(
        [jnp.tile(q_gain[group], N_HEADS) * (HEAD_DIM ** -0.5 * LOG2E), jnp.tile(k_gain[group], N_HEADS),
         jnp.ones((D_MODEL,), F32)]).reshape(1, 3 * D_MODEL)
    seg = np.arange(MXU_DIM) // HEAD_DIM
    block_diag = jnp.asarray((seg[:, None] == seg[None, :]) / HEAD_DIM, BF16)
    return pl.pallas_call(
        functools.partial(_qkv_kernel, mixer, group),
        grid=(n_tok // QKV_TM,),
        in_specs=[
            pl.BlockSpec((QKV_TM, D_MODEL), lambda i: (i, 0)),
            _layer((1, D_MODEL), layer),
            HBM,
            _resident((1, 3 * D_MODEL)),
            _resident((MXU_DIM, MXU_DIM)),
        ],
        out_specs=pl.BlockSpec((3 * N_PAIRS, QKV_TM, LANES), lambda i: (0, i, 0)),
        out_shape=jax.ShapeDtypeStruct((3 * N_PAIRS, n_tok, LANES), BF16),
        scratch_shapes=_weight_scratch(D_MODEL, 3 * D_MODEL, LANES) + [
            DMA_SEMS,
            pltpu.VMEM((_stride_passes(DILATION_GROUPS[group][1]), QKV_TM // QKV_SUB, D_MODEL // LANES, QKV_SUB,
                        LANES), F32)],
        compiler_params=_params("arbitrary"),
        name=f"qkv_proj_g{group}",
    )(h, g, w_qkv, gains, block_diag)


def _t5_bucket_np(dist):
    nf = np.maximum(dist, 1).astype(np.float32)
    large = MAX_EXACT + (np.log(nf / np.float32(MAX_EXACT)) / np.float32(math.log(MAX_DISTANCE / MAX_EXACT))
                         * np.float32(N_BUCKETS - MAX_EXACT)).astype(np.int32)
    large = np.minimum(large, N_BUCKETS - 1)
    return np.where(dist < MAX_EXACT, dist, large)


def _bucket_maps():
    i = (np.arange(2 * BLOCK) % BLOCK)[:, None]
    j = np.arange(2 * BLOCK)[None, :]
    step = BLOCK + i - j
    maps = []
    for window, dilation in DILATION_GROUPS:
        n_steps = window // dilation
        in_band = (step >= 0) & (step <= n_steps)
        bucket = _t5_bucket_np(np.clip(step, 0, n_steps) * dilation)
        maps.append(np.where(in_band, bucket, -1).astype(np.int32))
    return np.stack(maps)


def _bias_kernel(buckets, rel_ref, bm_ref, o_ref):
    p = pl.program_id(0)
    shape = bm_ref.shape[1:]
    top = lax.broadcasted_iota(jnp.int32, shape, 0) < BLOCK
    col = lax.broadcasted_iota(jnp.int32, shape, 1)
    for g in range(N_GROUPS):
        bm = bm_ref[g]
        acc = jnp.zeros(shape, F32)
        for b in buckets[g]:
            val = jnp.where(top, rel_ref[b, g * N_HEADS + 2 * p], rel_ref[b, g * N_HEADS + 2 * p + 1])
            acc = jnp.where(bm == b, val, acc)
        rest = jnp.where(bm < 0, NEG_INF, acc * LOG2E)
        o_ref[g, 1] = rest
        o_ref[g, 0] = jnp.where(col < BLOCK, NEG_INF, rest)


def _bias_tiles(rel_bias):
    tile = (2 * BLOCK, 2 * BLOCK)
    maps = _bucket_maps()
    buckets = tuple(tuple(int(b) for b in np.unique(m[m >= 0])) for m in maps)
    return pl.pallas_call(
        functools.partial(_bias_kernel, buckets),
        grid=(N_PAIRS,),
        in_specs=[pl.BlockSpec(memory_space=pltpu.SMEM), _resident((N_GROUPS,) + tile)],
        out_specs=pl.BlockSpec((N_GROUPS, 2, None) + tile, lambda p: (0, 0, p, 0, 0)),
        out_shape=jax.ShapeDtypeStruct((N_GROUPS, 2, N_PAIRS) + tile, F32),
        compiler_params=_params("arbitrary"),
        name="bias_tiles",
    )(rel_bias, jnp.asarray(maps))


ATTN_BLOCKS = 8


def _attn_kernel(nb, q_ref, kp_ref, kc_ref, vp_ref, vc_ref, bias_ref, o_ref, lse_ref):
    upb = q_ref.shape[1] // nb
    n_strands, rows_per_unit = q_ref.shape[2], q_ref.shape[3]
    not_first = jnp.minimum(pl.program_id(2), 1)
    lane = lax.broadcasted_iota(jnp.int32, (BLOCK, LANES), 1)
    lo = lane < HEAD_DIM
    ones = jnp.ones((2 * BLOCK, LANES), BF16)

    def keys_of(prev_ref, cur_ref, blk, p, j):
        if blk == 0:
            return jnp.concatenate([prev_ref[p, :, j].reshape(BLOCK, LANES),
                                    cur_ref[p, pl.ds(0, upb), j].reshape(BLOCK, LANES)], axis=0)
        return cur_ref[p, pl.ds((blk - 1) * upb, 2 * upb), j].reshape(2 * BLOCK, LANES)

    for j in range(n_strands):
        for blk in range(nb):
            units = pl.ds(blk * upb, upb)
            m_all = jnp.zeros((BLOCK, LANES), F32)
            s_all = jnp.ones((BLOCK, LANES), F32)
            for p in range(N_PAIRS):
                q = q_ref[p, units, j].reshape(BLOCK, LANES)
                zero = jnp.zeros_like(q)
                q2 = jnp.concatenate([jnp.where(lo, q, zero), jnp.where(lo, zero, q)], axis=0)
                k = keys_of(kp_ref, kc_ref, blk, p, j)
                logits = lax.dot_general(q2, k, (((1,), (1,)), ((), ())), preferred_element_type=F32)
                logits = logits + (bias_ref[not_first, p] if blk == 0 else bias_ref[1, p])
                m = jnp.max(logits, axis=-1, keepdims=True)
                prob = jnp.exp2(logits - m).astype(BF16)
                acc = _dot(prob, jnp.concatenate([keys_of(vp_ref, vc_ref, blk, p, j), ones], axis=1))
                o = jnp.where(lo, acc[:BLOCK, :LANES], acc[BLOCK:, :LANES])
                s = jnp.where(lo, acc[:BLOCK, LANES:], acc[BLOCK:, LANES:])
                mm = jnp.where(lo, m[:BLOCK], m[BLOCK:])
                o_ref[p, units, j] = (o / s).astype(BF16).reshape(upb, rows_per_unit, LANES)
                keep = (lane == 2 * p) | (lane == HEAD_DIM + 2 * p + 1)
                m_all = jnp.where(keep, mm, m_all)
                s_all = jnp.where(keep, s, s_all)
            lse_ref[units, j] = (m_all * LN2 + jnp.log(s_all)).reshape(upb, rows_per_unit, LANES)


def _attn_group(qkv, bias, g, batch, seq):
    r = DILATION_GROUPS[g][1]
    rows_per_unit = min(UNIT // r, BLOCK)
    n_units = seq // (r * rows_per_unit)
    upb = BLOCK // rows_per_unit
    nb = min(ATTN_BLOCKS, seq // (r * BLOCK))
    n_strands = ATTN_BLOCKS // nb
    view = qkv.reshape(3 * N_PAIRS, batch, n_units, r, rows_per_unit, LANES)

    def cur(which):
        return lambda b, s, c: (which, b, c, s, 0, 0)

    def prev(which):
        return lambda b, s, c: (which, b, jnp.maximum(c * nb - 1, 0), s, 0, 0)

    cur_blk = (N_PAIRS, None, nb * upb, n_strands, rows_per_unit, LANES)
    prev_blk = (N_PAIRS, None, upb, n_strands, rows_per_unit, LANES)
    o, lse = pl.pallas_call(
        functools.partial(_attn_kernel, nb),
        grid=(batch, r // n_strands, seq // (r * nb * BLOCK)),
        in_specs=[
            pl.BlockSpec(cur_blk, cur(0)),
            pl.BlockSpec(prev_blk, prev(1)),
            pl.BlockSpec(cur_blk, cur(1)),
            pl.BlockSpec(prev_blk, prev(2)),
            pl.BlockSpec(cur_blk, cur(2)),
            pl.BlockSpec((None, 2, N_PAIRS, 2 * BLOCK, 2 * BLOCK), lambda b, s, c: (g, 0, 0, 0, 0)),
        ],
        out_specs=[
            pl.BlockSpec(cur_blk, lambda b, s, c: (0, b, c, s, 0, 0)),
            pl.BlockSpec((None, nb * upb, n_strands, rows_per_unit, LANES), lambda b, s, c: (b, c, s, 0, 0)),
        ],
        out_shape=[
            jax.ShapeDtypeStruct((N_PAIRS, batch, n_units, r, rows_per_unit, LANES), BF16),
            jax.ShapeDtypeStruct((batch, n_units, r, rows_per_unit, LANES), F32),
        ],
        compiler_params=_params("arbitrary", "arbitrary", "arbitrary"),
        name=f"attn_r{r}",
    )(view, view, view, view, view, bias)
    return o.reshape(N_PAIRS, batch * seq, LANES), lse.reshape(batch * seq, LANES)


MERGE_TM = 1024
MERGE_SUB = 512


def _merge_kernel(mixer, h_ref, o0_ref, o1_ref, o2_ref, l0_ref, l1_ref, l2_ref, e_ref, w_hbm, out_ref,
                  w_ref, w_stage, sem, tok_ref, ltok_ref, tmp_ref, ltmp_ref, merged_ref):
    @pl.when(pl.program_id(0) == 0)
    def _():
        _load_weights(w_hbm.at[mixer], w_ref, w_stage, sem)

    o_refs = (o0_ref, o1_ref, o2_ref)
    l_refs = (l0_ref, l1_ref, l2_ref)
    for t in range(MERGE_TM // MERGE_SUB):
        first = t * MERGE_SUB
        rows = pl.ds(first, MERGE_SUB)
        lses = [l0_ref[rows, :]]
        for g in range(1, N_GROUPS):
            r = DILATION_GROUPS[g][1]
            rows_per_strand = UNIT // r
            for u in range(MERGE_SUB // UNIT):
                unit = first + u * UNIT
                if r == ROW_STRIDE:
                    for s in range(r):
                        src_rows = pl.ds(unit + s * rows_per_strand, rows_per_strand)
                        dst_rows = pl.ds(unit + s, rows_per_strand, stride=r)
                        ltok_ref[g - 1, dst_rows, :] = l_refs[g][src_rows, :]
                        for p in range(N_PAIRS):
                            tok_ref[g - 1, p, dst_rows, :] = o_refs[g][p, src_rows, :].astype(F32)
                else:
                    assert _stride_passes(r) == 2
                    quarter = UNIT // ROW_STRIDE
                    for a in range(ROW_STRIDE):
                        for b in range(ROW_STRIDE):
                            src_rows = pl.ds(unit + (ROW_STRIDE * a + b) * rows_per_strand, rows_per_strand)
                            dst_rows = pl.ds(unit + b * quarter + a, rows_per_strand, stride=ROW_STRIDE)
                            ltmp_ref[dst_rows, :] = l_refs[g][src_rows, :]
                            for p in range(N_PAIRS):
                                tmp_ref[p, dst_rows, :] = o_refs[g][p, src_rows, :].astype(F32)
                    for b in range(ROW_STRIDE):
                        src_rows = pl.ds(unit + b * quarter, quarter)
                        dst_rows = pl.ds(unit + b, quarter, stride=ROW_STRIDE)
                        ltok_ref[g - 1, dst_rows, :] = ltmp_ref[src_rows, :]
                        for p in range(N_PAIRS):
                            tok_ref[g - 1, p, dst_rows, :] = tmp_ref[p, src_rows, :]
            lses.append(ltok_ref[g - 1, rows, :])
        m = jnp.maximum(jnp.maximum(lses[0], lses[1]), lses[2])
        es = [jnp.exp(l - m) for l in lses]
        den = es[0] + es[1] + es[2]
        w_cat = []
        for g in range(1, N_GROUPS):
            w = es[g] / den
            w_hi = w.astype(BF16)
            w_cat.append(jnp.concatenate([w_hi, (w - w_hi.astype(F32)).astype(BF16)], axis=1))
        for c in range(D_MODEL // MXU_DIM):
            pairs = range(c * MXU_DIM // LANES, (c + 1) * MXU_DIM // LANES)
            base = jnp.concatenate([o0_ref[p, rows, :] for p in pairs], axis=1).astype(F32)
            acc = base
            for g in range(1, N_GROUPS):
                w_full = _dot(w_cat[g - 1], e_ref[:, pl.ds(c * MXU_DIM, MXU_DIM)])
                acc = acc + w_full * (jnp.concatenate([tok_ref[g - 1, p, rows, :] for p in pairs], axis=1) - base)
            merged_ref[rows, pl.ds(c * MXU_DIM, MXU_DIM)] = acc.astype(BF16)
        out_ref[rows, :] = h_ref[rows, :] + _dot(merged_ref[rows, :], w_ref[...])


def _merge_proj(h, outs, lses, mixer, w_out):
    n_tok = h.shape[0]
    expand = np.zeros((2 * LANES, D_MODEL), np.float32)
    for head in range(N_HEADS):
        src = head if head % 2 == 0 else HEAD_DIM + head
        expand[src, head * HEAD_DIM:(head + 1) * HEAD_DIM] = 1.0
    expand[LANES:] = expand[:LANES]
    tok = lambda width: pl.BlockSpec((MERGE_TM, width), lambda i: (i, 0))
    slab = pl.BlockSpec((N_PAIRS, MERGE_TM, LANES), lambda i: (0, i, 0))
    return pl.pallas_call(
        functools.partial(_merge_kernel, mixer),
        grid=(n_tok // MERGE_TM,),
        in_specs=[tok(D_MODEL)] + [slab] * N_GROUPS + [tok(LANES)] * N_GROUPS
        + [_resident((2 * LANES, D_MODEL)), HBM],
        out_specs=tok(D_MODEL),
        out_shape=jax.ShapeDtypeStruct((n_tok, D_MODEL), F32),
        scratch_shapes=_weight_scratch(D_MODEL, D_MODEL, MXU_DIM) + [
            DMA_SEMS,
            pltpu.VMEM((N_GROUPS - 1, N_PAIRS, MERGE_TM, LANES), F32),
            pltpu.VMEM((N_GROUPS - 1, MERGE_TM, LANES), F32),
            pltpu.VMEM((N_PAIRS, MERGE_TM, LANES), F32),
            pltpu.VMEM((MERGE_TM, LANES), F32),
            pltpu.VMEM((MERGE_TM, D_MODEL), BF16),
        ],
        compiler_params=_params("arbitrary"),
        name="merge_proj",
    )(h, *outs, *lses, jnp.asarray(expand, BF16), w_out)


def kernel(x, norm_ffn1, ffn1_w_in, ffn1_w_out, norm_mix, conv_w_in, conv_w, conv_w_out, attn_w_qkv,
           attn_q_gain, attn_k_gain, attn_w_out, rel_bias, norm_ffn2, ffn2_w_in, ffn2_w_out):
    batch, seq, d_model = x.shape
    assert d_model == D_MODEL and seq % max(QKV_TM, CONV_TM, DILATION_GROUPS[-1][1] * BLOCK) == 0
    depth = norm_ffn1.shape[0]
    h = x.reshape(batch * seq, D_MODEL)
    bias = _bias_tiles(rel_bias) if depth > 1 else None
    gains = [a.reshape(depth, 1, D_MODEL) for a in (norm_ffn1, norm_mix, norm_ffn2)]
    for i in range(depth):
        h = _ffn(h, i, gains[0], ffn1_w_in, ffn1_w_out)
        j = i // 2
        if i % 2 == 0:
            h = _conv_mixer(h, seq, i, j, gains[1], conv_w_in, conv_w, conv_w_out)
        else:
            parts = [_attn_group(_qkv_proj(h, i, j, g, gains[1], attn_w_qkv, attn_q_gain[j], attn_k_gain[j]),
                                 bias, g, batch, seq) for g in range(N_GROUPS)]
            h = _merge_proj(h, [o for o, _ in parts], [l for _, l in parts], j, attn_w_out)
        h = _ffn(h, i, gains[2], ffn2_w_in, ffn2_w_out)
    return h.reshape(batch, seq, D_MODEL)
```

```python
import functools
import math

import numpy as np
import jax
import jax.numpy as jnp
from jax import lax
from jax.experimental import pallas as pl
from jax.experimental.pallas import tpu as pltpu

D_MODEL = 1024
D_FF = 2816
CONV_WIDTH = 3
HEAD_DIM = 64
N_HEADS = D_MODEL // HEAD_DIM
N_PAIRS = N_HEADS // 2
DILATION_GROUPS = ((128, 1), (512, 4), (2048, 16))
N_GROUPS = len(DILATION_GROUPS)
BLOCK = 128
N_BUCKETS = 32
MAX_EXACT = N_BUCKETS // 2
MAX_DISTANCE = 2048
RMS_EPS = 1e-6
NEG_INF = -1e30
LOG2E = math.log2(math.e)
LN2 = math.log(2.0)

LANES = 128
MXU_DIM = 256
UNIT = MXU_DIM
VMEM_LIMIT = 56 * 1024 * 1024

F32 = jnp.float32
BF16 = jnp.bfloat16


def _rmsnorm_bf16(x, g):
    ms = jnp.mean(x * x, axis=-1, keepdims=True)
    return (x * lax.rsqrt(ms + RMS_EPS) * g).astype(BF16)


def _dot(a, b):
    return jnp.dot(a, b, preferred_element_type=F32)


def _resident(shape):
    return pl.BlockSpec(shape, lambda *_: (0,) * len(shape), pipeline_mode=pl.Buffered(1))


def _layer(shape, layer):
    return pl.BlockSpec((None,) + shape, lambda *_: (layer,) + (0,) * len(shape), pipeline_mode=pl.Buffered(1))


def _load_weights(src, dst_ref, stage_ref, sem):
    rows = stage_ref.shape[1]
    n_chunks = dst_ref.shape[0] // rows

    def copy(c):
        return pltpu.make_async_copy(src.at[pl.ds(c * rows, rows)], stage_ref.at[c % 2], sem.at[c % 2])

    copy(0).start()
    for c in range(n_chunks):
        if c + 1 < n_chunks:
            copy(c + 1).start()
        copy(c).wait()
        dst_ref[pl.ds(c * rows, rows), :] = stage_ref[c % 2].astype(BF16)


def _weight_scratch(rows, cols, stage_rows):
    assert rows % stage_rows == 0
    return [pltpu.VMEM((rows, cols), BF16), pltpu.VMEM((2, stage_rows, cols), F32)]


HBM = pl.BlockSpec(memory_space=pl.ANY)
DMA_SEMS = pltpu.SemaphoreType.DMA((2,))


def _params(*sem):
    return pltpu.CompilerParams(dimension_semantics=sem, vmem_limit_bytes=VMEM_LIMIT)


assert DILATION_GROUPS[0][1] == 1

ROW_STRIDE = 4


def _stride_passes(r):
    assert r in (1, ROW_STRIDE, ROW_STRIDE ** 2)
    return {1: 1, ROW_STRIDE: 1, ROW_STRIDE ** 2: 2}[r]


FFN_TM = 1024
FFN_SUB = 512
FFN_CHUNKS = ((0, 5 * MXU_DIM), (5 * MXU_DIM, 6 * MXU_DIM))


def _ffn_kernel(layer, x_ref, g_ref, win_hbm, wout_hbm, o_ref, win_ref, win_stage, wout_ref, wout_stage, sem):
    @pl.when(pl.program_id(0) == 0)
    def _():
        _load_weights(win_hbm.at[layer], win_ref, win_stage, sem)
        _load_weights(wout_hbm.at[layer], wout_ref, wout_stage, sem)

    for t in range(FFN_TM // FFN_SUB):
        rows = pl.ds(t * FFN_SUB, FFN_SUB)
        x = x_ref[rows, :]
        xn = _rmsnorm_bf16(x, g_ref[...])
        acc = None
        for start, size in FFN_CHUNKS:
            gate = _dot(xn, win_ref[:, pl.ds(start, size)])
            up = _dot(xn, win_ref[:, pl.ds(D_FF + start, size)])
            a = (gate * jax.nn.sigmoid(gate) * up).astype(BF16)
            part = _dot(a, wout_ref[pl.ds(start, size), :])
            acc = part if acc is None else acc + part
        o_ref[rows, :] = x + 0.5 * acc


def _ffn(h, layer, g, w_in, w_out):
    n_tok = h.shape[0]
    return pl.pallas_call(
        functools.partial(_ffn_kernel, layer),
        grid=(n_tok // FFN_TM,),
        in_specs=[pl.BlockSpec((FFN_TM, D_MODEL), lambda i: (i, 0)), _layer((1, D_MODEL), layer), HBM, HBM],
        out_specs=pl.BlockSpec((FFN_TM, D_MODEL), lambda i: (i, 0)),
        out_shape=jax.ShapeDtypeStruct((n_tok, D_MODEL), F32),
        scratch_shapes=_weight_scratch(D_MODEL, 2 * D_FF, LANES) + _weight_scratch(D_FF, D_MODEL, D_FF // 8)
        + [DMA_SEMS],
        compiler_params=_params("arbitrary"),
        name="ffn",
    )(h, g, w_in, w_out)


CONV_TM = 1024
CONV_SUB = 512
CARRY = 8


def _conv_kernel(tiles_per_seq, mixer, x_ref, g_ref, win_hbm, cw_ref, wout_hbm, o_ref, vp_ref,
                 win_ref, win_stage, wout_ref, wout_stage, sem):
    @pl.when(pl.program_id(0) == 0)
    def _():
        _load_weights(win_hbm.at[mixer], win_ref, win_stage, sem)
        _load_weights(wout_hbm.at[mixer], wout_ref, wout_stage, sem)

    @pl.when(pl.program_id(0) % tiles_per_seq == 0)
    def _():
        vp_ref[pl.ds(0, CARRY), :] = jnp.zeros((CARRY, D_MODEL), F32)

    for t in range(CONV_TM // CONV_SUB):
        rows = pl.ds(t * CONV_SUB, CONV_SUB)
        x = x_ref[rows, :]
        xn = _rmsnorm_bf16(x, g_ref[...])
        b_gate = _dot(xn, win_ref[:, pl.ds(0, D_MODEL)])
        c_gate = _dot(xn, win_ref[:, pl.ds(D_MODEL, D_MODEL)])
        u = _dot(xn, win_ref[:, pl.ds(2 * D_MODEL, D_MODEL)])
        v = c_gate * u
        vp_ref[pl.ds(CARRY, CONV_SUB), :] = v
        conv = cw_ref[pl.ds(0, 1), :] * v
        for lag in range(1, CONV_WIDTH):
            conv = conv + cw_ref[pl.ds(lag, 1), :] * vp_ref[pl.ds(CARRY - lag, CONV_SUB), :]
        z = (b_gate * conv).astype(BF16)
        o_ref[rows, :] = x + _dot(z, wout_ref[...])
        vp_ref[pl.ds(0, CARRY), :] = vp_ref[pl.ds(CONV_SUB, CARRY), :]


def _conv_mixer(h, seq, layer, mixer, g, w_in, w_conv, w_out):
    n_tok = h.shape[0]
    return pl.pallas_call(
        functools.partial(_conv_kernel, seq // CONV_TM, mixer),
        grid=(n_tok // CONV_TM,),
        in_specs=[
            pl.BlockSpec((CONV_TM, D_MODEL), lambda i: (i, 0)),
            _layer((1, D_MODEL), layer),
            HBM,
            _layer((CONV_WIDTH, D_MODEL), mixer),
            HBM,
        ],
        out_specs=pl.BlockSpec((CONV_TM, D_MODEL), lambda i: (i, 0)),
        out_shape=jax.ShapeDtypeStruct((n_tok, D_MODEL), F32),
        scratch_shapes=[pltpu.VMEM((CARRY + CONV_SUB, D_MODEL), F32)]
        + _weight_scratch(D_MODEL, 3 * D_MODEL, LANES) + _weight_scratch(D_MODEL, D_MODEL, MXU_DIM) + [DMA_SEMS],
        compiler_params=_params("arbitrary"),
        name="conv_mixer",
    )(h, g, w_in, w_conv, w_out)


QKV_TM = 1024
QKV_SUB = 512
QKV_TN = 2 * MXU_DIM
N_QKV_TILES = 3 * N_GROUPS


def _qkv_kernel(mixer, group, x_ref, g_ref, w_hbm, gain_ref, bd_ref, perm_ref, o_ref, w_ref, w_stage, sem, xs_ref):
    @pl.when(pl.program_id(0) == 0)
    def _():
        _load_weights(w_hbm.at[mixer, :, pl.ds(group * 3 * D_MODEL, 3 * D_MODEL)], w_ref, w_stage, sem)

    r = DILATION_GROUPS[group][1]
    rows_per_strand = UNIT // r
    for t in range(QKV_TM // QKV_SUB):
        rows = pl.ds(t * QKV_SUB, QKV_SUB)
        x = x_ref[rows, :]
        xn = x * lax.rsqrt(jnp.mean(x * x, axis=-1, keepdims=True) + RMS_EPS) * g_ref[...]
        if r == ROW_STRIDE:
            slabs = range(D_MODEL // LANES)
            for p in slabs:
                xs_ref[t, p] = xn[:, p * LANES:(p + 1) * LANES]
            xn = jnp.concatenate(
                [jnp.concatenate([xs_ref[t, p, pl.ds(u * UNIT + s, rows_per_strand, stride=r), :] for p in slabs],
                                 axis=1)
                 for u in range(QKV_SUB // UNIT) for s in range(r)], axis=0)
        xn = xn.astype(BF16)
        if r > ROW_STRIDE:
            xn = jnp.concatenate(
                [_dot(perm_ref[...], xn[u * UNIT:(u + 1) * UNIT]).astype(BF16) for u in range(QKV_SUB // UNIT)],
                axis=0)
        for c in range(3 * D_MODEL // QKV_TN):
            cols = pl.ds(c * QKV_TN, QKV_TN)
            y = _dot(xn, w_ref[:, cols])
            if c * QKV_TN >= 2 * D_MODEL:
                res = y.astype(BF16)
            else:
                y2 = (y * y).astype(BF16)
                ms = jnp.concatenate(
                    [_dot(y2[:, s * MXU_DIM:(s + 1) * MXU_DIM], bd_ref[...]) for s in range(QKV_TN // MXU_DIM)],
                    axis=1)
                res = (y * lax.rsqrt(ms + RMS_EPS) * gain_ref[:, cols]).astype(BF16)
            for s in range(QKV_TN // LANES):
                o_ref[c * (QKV_TN // LANES) + s, rows, :] = res[:, s * LANES:(s + 1) * LANES]


def _qkv_proj(h, layer, mixer, group, g, w_qkv, q_gain, k_gain):
    n_tok = h.shape[0]
    gains = jnp.concatenate(
        [jnp.tile(q_gain[group], N_HEADS) * (HEAD_DIM ** -0.5 * LOG2E), jnp.tile(k_gain[group], N_HEADS),
         jnp.ones((D_MODEL,), F32)]).reshape(1, 3 * D_MODEL)
    seg = np.arange(MXU_DIM) // HEAD_DIM
    block_diag = jnp.asarray((seg[:, None] == seg[None, :]) / HEAD_DIM, BF16)
    r = DILATION_GROUPS[group][1]
    tok = np.arange(UNIT)
    perm = np.zeros((UNIT, UNIT), np.float32)
    perm[(tok % r) * (UNIT // r) + tok // r, tok] = 1.0
    return pl.pallas_call(
        functools.partial(_qkv_kernel, mixer, group),
        grid=(n_tok // QKV_TM,),
        in_specs=[
            pl.BlockSpec((QKV_TM, D_MODEL), lambda i: (i, 0)),
            _layer((1, D_MODEL), layer),
            HBM,
            _resident((1, 3 * D_MODEL)),
            _resident((MXU_DIM, MXU_DIM)),
            _resident((UNIT, UNIT)),
        ],
        out_specs=pl.BlockSpec((3 * N_PAIRS, QKV_TM, LANES), lambda i: (0, i, 0)),
        out_shape=jax.ShapeDtypeStruct((3 * N_PAIRS, n_tok, LANES), BF16),
        scratch_shapes=_weight_scratch(D_MODEL, 3 * D_MODEL, LANES) + [
            DMA_SEMS, pltpu.VMEM((QKV_TM // QKV_SUB, D_MODEL // LANES, QKV_SUB, LANES), F32)],
        compiler_params=_params("arbitrary"),
        name=f"qkv_proj_g{group}",
    )(h, g, w_qkv, gains, block_diag, jnp.asarray(perm, BF16))


def _t5_bucket_np(dist):
    nf = np.maximum(dist, 1).astype(np.float32)
    large = MAX_EXACT + (np.log(nf / np.float32(MAX_EXACT)) / np.float32(math.log(MAX_DISTANCE / MAX_EXACT))
                         * np.float32(N_BUCKETS - MAX_EXACT)).astype(np.int32)
    large = np.minimum(large, N_BUCKETS - 1)
    return np.where(dist < MAX_EXACT, dist, large)


def _bucket_maps():
    i = (np.arange(2 * BLOCK) % BLOCK)[:, None]
    j = np.arange(2 * BLOCK)[None, :]
    step = BLOCK + i - j
    maps = []
    for window, dilation in DILATION_GROUPS:
        n_steps = window // dilation
        in_band = (step >= 0) & (step <= n_steps)
        bucket = _t5_bucket_np(np.clip(step, 0, n_steps) * dilation)
        maps.append(np.where(in_band, bucket, -1).astype(np.int32))
    return np.stack(maps)


def _bias_kernel(buckets, rel_ref, bm_ref, o_ref):
    p = pl.program_id(0)
    shape = bm_ref.shape[1:]
    top = lax.broadcasted_iota(jnp.int32, shape, 0) < BLOCK
    col = lax.broadcasted_iota(jnp.int32, shape, 1)
    for g in range(N_GROUPS):
        bm = bm_ref[g]
        acc = jnp.zeros(shape, F32)
        for b in buckets[g]:
            val = jnp.where(top, rel_ref[b, g * N_HEADS + 2 * p], rel_ref[b, g * N_HEADS + 2 * p + 1])
            acc = jnp.where(bm == b, val, acc)
        rest = jnp.where(bm < 0, NEG_INF, acc * LOG2E)
        o_ref[g, 1] = rest
        o_ref[g, 0] = jnp.where(col < BLOCK, NEG_INF, rest)


def _bias_tiles(rel_bias):
    tile = (2 * BLOCK, 2 * BLOCK)
    maps = _bucket_maps()
    buckets = tuple(tuple(int(b) for b in np.unique(m[m >= 0])) for m in maps)
    return pl.pallas_call(
        functools.partial(_bias_kernel, buckets),
        grid=(N_PAIRS,),
        in_specs=[pl.BlockSpec(memory_space=pltpu.SMEM), _resident((N_GROUPS,) + tile)],
        out_specs=pl.BlockSpec((N_GROUPS, 2, None) + tile, lambda p: (0, 0, p, 0, 0)),
        out_shape=jax.ShapeDtypeStruct((N_GROUPS, 2, N_PAIRS) + tile, F32),
        compiler_params=_params("arbitrary"),
        name="bias_tiles",
    )(rel_bias, jnp.asarray(maps))


ATTN_BLOCKS = 16


def _attn_kernel(nb, q_ref, kp_ref, kc_ref, vp_ref, vc_ref, bias_ref, o_ref, lse_ref):
    upb = q_ref.shape[1] // nb
    n_strands, rows_per_unit = q_ref.shape[2], q_ref.shape[3]
    not_first = jnp.minimum(pl.program_id(2), 1)
    lane = lax.broadcasted_iota(jnp.int32, (BLOCK, LANES), 1)
    lo = lane < HEAD_DIM
    ones = jnp.ones((2 * BLOCK, LANES), BF16)

    def keys_of(prev_ref, cur_ref, blk, p, j):
        if blk == 0:
            return jnp.concatenate([prev_ref[p, :, j].reshape(BLOCK, LANES),
                                    cur_ref[p, pl.ds(0, upb), j].reshape(BLOCK, LANES)], axis=0)
        return cur_ref[p, pl.ds((blk - 1) * upb, 2 * upb), j].reshape(2 * BLOCK, LANES)

    for j in range(n_strands):
        for blk in range(nb):
            units = pl.ds(blk * upb, upb)
            m_all = jnp.zeros((BLOCK, LANES), F32)
            s_all = jnp.ones((BLOCK, LANES), F32)
            for p in range(N_PAIRS):
                q = q_ref[p, units, j].reshape(BLOCK, LANES)
                zero = jnp.zeros_like(q)
                q2 = jnp.concatenate([jnp.where(lo, q, zero), jnp.where(lo, zero, q)], axis=0)
                k = keys_of(kp_ref, kc_ref, blk, p, j)
                logits = lax.dot_general(q2, k, (((1,), (1,)), ((), ())), preferred_element_type=F32)
                logits = logits + (bias_ref[not_first, p] if blk == 0 else bias_ref[1, p])
                m = jnp.max(logits, axis=-1, keepdims=True)
                prob = jnp.exp2(logits - m).astype(BF16)
                acc = _dot(prob, jnp.concatenate([keys_of(vp_ref, vc_ref, blk, p, j), ones], axis=1))
                o = jnp.where(lo, acc[:BLOCK, :LANES], acc[BLOCK:, :LANES])
                s = jnp.where(lo, acc[:BLOCK, LANES:], acc[BLOCK:, LANES:])
                mm = jnp.where(lo, m[:BLOCK], m[BLOCK:])
                o_ref[p, units, j] = (o / s).astype(BF16).reshape(upb, rows_per_unit, LANES)
                keep = (lane == 2 * p) | (lane == HEAD_DIM + 2 * p + 1)
                m_all = jnp.where(keep, mm, m_all)
                s_all = jnp.where(keep, s, s_all)
            lse_ref[units, j] = (m_all * LN2 + jnp.log(s_all)).reshape(upb, rows_per_unit, LANES)


def _attn_group(qkv, bias, g, batch, seq):
    r = DILATION_GROUPS[g][1]
    rows_per_unit = min(UNIT // r, BLOCK)
    n_units = seq // (r * rows_per_unit)
    upb = BLOCK // rows_per_unit
    nb = min(ATTN_BLOCKS, seq // (r * BLOCK))
    n_strands = ATTN_BLOCKS // nb
    view = qkv.reshape(3 * N_PAIRS, batch, n_units, r, rows_per_unit, LANES)

    def cur(which):
        return lambda b, s, c: (which, b, c, s, 0, 0)

    def prev(which):
        return lambda b, s, c: (which, b, jnp.maximum(c * nb - 1, 0), s, 0, 0)

    cur_blk = (N_PAIRS, None, nb * upb, n_strands, rows_per_unit, LANES)
    prev_blk = (N_PAIRS, None, upb, n_strands, rows_per_unit, LANES)
    o, lse = pl.pallas_call(
        functools.partial(_attn_kernel, nb),
        grid=(batch, r // n_strands, seq // (r * nb * BLOCK)),
        in_specs=[
            pl.BlockSpec(cur_blk, cur(0)),
            pl.BlockSpec(prev_blk, prev(1)),
            pl.BlockSpec(cur_blk, cur(1)),
            pl.BlockSpec(prev_blk, prev(2)),
            pl.BlockSpec(cur_blk, cur(2)),
            pl.BlockSpec((None, 2, N_PAIRS, 2 * BLOCK, 2 * BLOCK), lambda b, s, c: (g, 0, 0, 0, 0)),
        ],
        out_specs=[
            pl.BlockSpec(cur_blk, lambda b, s, c: (0, b, c, s, 0, 0)),
            pl.BlockSpec((None, nb * upb, n_strands, rows_per_unit, LANES), lambda b, s, c: (b, c, s, 0, 0)),
        ],
        out_shape=[
            jax.ShapeDtypeStruct((N_PAIRS, batch, n_units, r, rows_per_unit, LANES), BF16),
            jax.ShapeDtypeStruct((batch, n_units, r, rows_per_unit, LANES), F32),
        ],
        compiler_params=_params("arbitrary", "arbitrary", "arbitrary"),
        name=f"attn_r{r}",
    )(view, view, view, view, view, bias)
    return o.reshape(N_PAIRS, batch * seq, LANES), lse.reshape(batch * seq, LANES)


MERGE_TM = 1024
MERGE_SUB = 512


def _merge_kernel(mixer, h_ref, o0_ref, o1_ref, o2_ref, l0_ref, l1_ref, l2_ref, e_ref, w_hbm, out_ref,
                  w_ref, w_stage, sem, tok_ref, ltok_ref, tmp_ref, ltmp_ref, merged_ref):
    @pl.when(pl.program_id(0) == 0)
    def _():
        _load_weights(w_hbm.at[mixer], w_ref, w_stage, sem)

    o_refs = (o0_ref, o1_ref, o2_ref)
    l_refs = (l0_ref, l1_ref, l2_ref)
    for t in range(MERGE_TM // MERGE_SUB):
        first = t * MERGE_SUB
        rows = pl.ds(first, MERGE_SUB)
        lses = [l0_ref[rows, :]]
        for g in range(1, N_GROUPS):
            r = DILATION_GROUPS[g][1]
            rows_per_strand = UNIT // r
            for u in range(MERGE_SUB // UNIT):
                unit = first + u * UNIT
                if r == ROW_STRIDE:
                    for s in range(r):
                        src_rows = pl.ds(unit + s * rows_per_strand, rows_per_strand)
                        dst_rows = pl.ds(unit + s, rows_per_strand, stride=r)
                        ltok_ref[g - 1, dst_rows, :] = l_refs[g][src_rows, :]
                        for p in range(N_PAIRS):
                            tok_ref[g - 1, p, dst_rows, :] = o_refs[g][p, src_rows, :].astype(F32)
                else:
                    assert _stride_passes(r) == 2
                    quarter = UNIT // ROW_STRIDE
                    for a in range(ROW_STRIDE):
                        for b in range(ROW_STRIDE):
                            src_rows = pl.ds(unit + (ROW_STRIDE * a + b) * rows_per_strand, rows_per_strand)
                            dst_rows = pl.ds(unit + b * quarter + a, rows_per_strand, stride=ROW_STRIDE)
                            ltmp_ref[dst_rows, :] = l_refs[g][src_rows, :]
                            for p in range(N_PAIRS):
                                tmp_ref[p, dst_rows, :] = o_refs[g][p, src_rows, :].astype(F32)
                    for b in range(ROW_STRIDE):
                        src_rows = pl.ds(unit + b * quarter, quarter)
                        dst_rows = pl.ds(unit + b, quarter, stride=ROW_STRIDE)
                        ltok_ref[g - 1, dst_rows, :] = ltmp_ref[src_rows, :]
                        for p in range(N_PAIRS):
                            tok_ref[g - 1, p, dst_rows, :] = tmp_ref[p, src_rows, :]
            lses.append(ltok_ref[g - 1, rows, :])
        m = jnp.maximum(jnp.maximum(lses[0], lses[1]), lses[2])
        es = [jnp.exp(l - m) for l in lses]
        den = es[0] + es[1] + es[2]
        w_cat = []
        for g in range(1, N_GROUPS):
            w = es[g] / den
            w_hi = w.astype(BF16)
            w_cat.append(jnp.concatenate([w_hi, (w - w_hi.astype(F32)).astype(BF16)], axis=1))
        for c in range(D_MODEL // MXU_DIM):
            pairs = range(c * MXU_DIM // LANES, (c + 1) * MXU_DIM // LANES)
            base = jnp.concatenate([o0_ref[p, rows, :] for p in pairs], axis=1).astype(F32)
            acc = base
            for g in range(1, N_GROUPS):
                w_full = _dot(w_cat[g - 1], e_ref[:, pl.ds(c * MXU_DIM, MXU_DIM)])
                acc = acc + w_full * (jnp.concatenate([tok_ref[g - 1, p, rows, :] for p in pairs], axis=1) - base)
            merged_ref[rows, pl.ds(c * MXU_DIM, MXU_DIM)] = acc.astype(BF16)
        out_ref[rows, :] = h_ref[rows, :] + _dot(merged_ref[rows, :], w_ref[...])


def _merge_proj(h, outs, lses, mixer, w_out):
    n_tok = h.shape[0]
    expand = np.zeros((2 * LANES, D_MODEL), np.float32)
    for head in range(N_HEADS):
        src = head if head % 2 == 0 else HEAD_DIM + head
        expand[src, head * HEAD_DIM:(head + 1) * HEAD_DIM] = 1.0
    expand[LANES:] = expand[:LANES]
    tok = lambda width: pl.BlockSpec((MERGE_TM, width), lambda i: (i, 0))
    slab = pl.BlockSpec((N_PAIRS, MERGE_TM, LANES), lambda i: (0, i, 0))
    return pl.pallas_call(
        functools.partial(_merge_kernel, mixer),
        grid=(n_tok // MERGE_TM,),
        in_specs=[tok(D_MODEL)] + [slab] * N_GROUPS + [tok(LANES)] * N_GROUPS
        + [_resident((2 * LANES, D_MODEL)), HBM],
        out_specs=tok(D_MODEL),
        out_shape=jax.ShapeDtypeStruct((n_tok, D_MODEL), F32),
        scratch_shapes=_weight_scratch(D_MODEL, D_MODEL, MXU_DIM) + [
            DMA_SEMS,
            pltpu.VMEM((N_GROUPS - 1, N_PAIRS, MERGE_TM, LANES), F32),
            pltpu.VMEM((N_GROUPS - 1, MERGE_TM, LANES), F32),
            pltpu.VMEM((N_PAIRS, MERGE_TM, LANES), F32),
            pltpu.VMEM((MERGE_TM, LANES), F32),
            pltpu.VMEM((MERGE_TM, D_MODEL), BF16),
        ],
        compiler_params=_params("arbitrary"),
        name="merge_proj",
    )(h, *outs, *lses, jnp.asarray(expand, BF16), w_out)


def kernel(x, norm_ffn1, ffn1_w_in, ffn1_w_out, norm_mix, conv_w_in, conv_w, conv_w_out, attn_w_qkv,
           attn_q_gain, attn_k_gain, attn_w_out, rel_bias, norm_ffn2, ffn2_w_in, ffn2_w_out):
    batch, seq, d_model = x.shape
    assert d_model == D_MODEL and seq % max(QKV_TM, CONV_TM, DILATION_GROUPS[-1][1] * BLOCK) == 0
    depth = norm_ffn1.shape[0]
    h = x.reshape(batch * seq, D_MODEL)
    bias = _bias_tiles(rel_bias) if depth > 1 else None
    gains = [a.reshape(depth, 1, D_MODEL) for a in (norm_ffn1, norm_mix, norm_ffn2)]
    for i in range(depth):
        h = _ffn(h, i, gains[0], ffn1_w_in, ffn1_w_out)
        j = i // 2
        if i % 2 == 0:
            h = _conv_mixer(h, seq, i, j, gains[1], conv_w_in, conv_w, conv_w_out)
        else:
            parts = [_attn_group(_qkv_proj(h, i, j, g, gains[1], attn_w_qkv, attn_q_gain[j], attn_k_gain[j]),
                                 bias, g, batch, seq) for g in range(N_GROUPS)]
            h = _merge_proj(h, [o for o, _ in parts], [l for _, l in parts], j, attn_w_out)
        h = _ffn(h, i, gains[2], ffn2_w_in, ffn2_w_out)
    return h.reshape(batch, seq, D_MODEL)
```

```python
import functools
import math

import numpy as np
import jax
import jax.numpy as jnp
from jax import lax
from jax.experimental import pallas as pl
from jax.experimental.pallas import tpu as pltpu

D_MODEL = 1024
D_FF = 2816
CONV_WIDTH = 3
HEAD_DIM = 64
N_HEADS = D_MODEL // HEAD_DIM
N_PAIRS = N_HEADS // 2
DILATION_GROUPS = ((128, 1), (512, 4), (2048, 16))
N_GROUPS = len(DILATION_GROUPS)
BLOCK = 128
N_BUCKETS = 32
MAX_EXACT = N_BUCKETS // 2
MAX_DISTANCE = 2048
RMS_EPS = 1e-6
NEG_INF = -1e30
LOG2E = math.log2(math.e)
LN2 = math.log(2.0)

LANES = 128
MXU_DIM = 256
UNIT = MXU_DIM
VMEM_LIMIT = 56 * 1024 * 1024

F32 = jnp.float32
BF16 = jnp.bfloat16


def _rmsnorm_bf16(x, g):
    ms = jnp.mean(x * x, axis=-1, keepdims=True)
    return (x * lax.rsqrt(ms + RMS_EPS) * g).astype(BF16)


def _dot(a, b):
    return jnp.dot(a, b, preferred_element_type=F32)


def _resident(shape):
    return pl.BlockSpec(shape, lambda *_: (0,) * len(shape), pipeline_mode=pl.Buffered(1))


def _layer(shape, layer):
    return pl.BlockSpec((None,) + shape, lambda *_: (layer,) + (0,) * len(shape), pipeline_mode=pl.Buffered(1))


def _load_weights(src, dst_ref, stage_ref, sem):
    rows = stage_ref.shape[1]
    n_chunks = dst_ref.shape[0] // rows

    def copy(c):
        return pltpu.make_async_copy(src.at[pl.ds(c * rows, rows)], stage_ref.at[c % 2], sem.at[c % 2])

    copy(0).start()
    for c in range(n_chunks):
        if c + 1 < n_chunks:
            copy(c + 1).start()
        copy(c).wait()
        dst_ref[pl.ds(c * rows, rows), :] = stage_ref[c % 2].astype(BF16)


def _weight_scratch(rows, cols, stage_rows):
    assert rows % stage_rows == 0
    return [pltpu.VMEM((rows, cols), BF16), pltpu.VMEM((2, stage_rows, cols), F32)]


HBM = pl.BlockSpec(memory_space=pl.ANY)
DMA_SEMS = pltpu.SemaphoreType.DMA((2,))


def _params(*sem):
    return pltpu.CompilerParams(dimension_semantics=sem, vmem_limit_bytes=VMEM_LIMIT)


assert DILATION_GROUPS[0][1] == 1

ROW_STRIDE = 4


def _stride_passes(r):
    assert r in (1, ROW_STRIDE, ROW_STRIDE ** 2)
    return {1: 1, ROW_STRIDE: 1, ROW_STRIDE ** 2: 2}[r]


FFN_TM = 1024
FFN_SUB = 512
FFN_CHUNKS = ((0, 5 * MXU_DIM), (5 * MXU_DIM, 6 * MXU_DIM))


def _ffn_kernel(layer, n_inputs, *refs):
    x_refs = refs[:n_inputs]
    g_ref, win_hbm, wout_hbm, o_ref, win_ref, win_stage, wout_ref, wout_stage, sem = refs[n_inputs:]

    @pl.when(pl.program_id(0) == 0)
    def _():
        _load_weights(win_hbm.at[layer], win_ref, win_stage, sem)
        _load_weights(wout_hbm.at[layer], wout_ref, wout_stage, sem)

    for t in range(FFN_TM // FFN_SUB):
        rows = pl.ds(t * FFN_SUB, FFN_SUB)
        x = x_refs[0][rows, :]
        for extra_ref in x_refs[1:]:
            x = x + extra_ref[rows, :]
        xn = _rmsnorm_bf16(x, g_ref[...])
        acc = None
        for start, size in FFN_CHUNKS:
            gate = _dot(xn, win_ref[:, pl.ds(start, size)])
            up = _dot(xn, win_ref[:, pl.ds(D_FF + start, size)])
            a = (gate * jax.nn.sigmoid(gate) * up).astype(BF16)
            part = _dot(a, wout_ref[pl.ds(start, size), :])
            acc = part if acc is None else acc + part
        o_ref[rows, :] = x + 0.5 * acc


def _ffn(hs, layer, g, w_in, w_out):
    n_tok = hs[0].shape[0]
    tile = pl.BlockSpec((FFN_TM, D_MODEL), lambda i: (i, 0))
    return pl.pallas_call(
        functools.partial(_ffn_kernel, layer, len(hs)),
        grid=(n_tok // FFN_TM,),
        in_specs=[tile] * len(hs) + [_layer((1, D_MODEL), layer), HBM, HBM],
        out_specs=tile,
        out_shape=jax.ShapeDtypeStruct((n_tok, D_MODEL), F32),
        scratch_shapes=_weight_scratch(D_MODEL, 2 * D_FF, LANES // 2) + _weight_scratch(D_FF, D_MODEL, D_FF // 8)
        + [DMA_SEMS],
        compiler_params=_params("arbitrary"),
        name="ffn",
    )(*hs, g, w_in, w_out)


CONV_TM = 1024
CONV_SUB = 512
CARRY = 8


def _conv_kernel(tiles_per_seq, mixer, x_ref, g_ref, win_hbm, cw_ref, wout_hbm, o_ref, vp_ref,
                 win_ref, win_stage, wout_ref, wout_stage, sem):
    @pl.when(pl.program_id(0) == 0)
    def _():
        _load_weights(win_hbm.at[mixer], win_ref, win_stage, sem)
        _load_weights(wout_hbm.at[mixer], wout_ref, wout_stage, sem)

    @pl.when(pl.program_id(0) % tiles_per_seq == 0)
    def _():
        vp_ref[pl.ds(0, CARRY), :] = jnp.zeros((CARRY, D_MODEL), F32)

    for t in range(CONV_TM // CONV_SUB):
        rows = pl.ds(t * CONV_SUB, CONV_SUB)
        x = x_ref[rows, :]
        xn = _rmsnorm_bf16(x, g_ref[...])
        b_gate = _dot(xn, win_ref[:, pl.ds(0, D_MODEL)])
        c_gate = _dot(xn, win_ref[:, pl.ds(D_MODEL, D_MODEL)])
        u = _dot(xn, win_ref[:, pl.ds(2 * D_MODEL, D_MODEL)])
        v = c_gate * u
        vp_ref[pl.ds(CARRY, CONV_SUB), :] = v
        conv = cw_ref[pl.ds(0, 1), :] * v
        for lag in range(1, CONV_WIDTH):
            conv = conv + cw_ref[pl.ds(lag, 1), :] * vp_ref[pl.ds(CARRY - lag, CONV_SUB), :]
        z = (b_gate * conv).astype(BF16)
        o_ref[rows, :] = x + _dot(z, wout_ref[...])
        vp_ref[pl.ds(0, CARRY), :] = vp_ref[pl.ds(CONV_SUB, CARRY), :]


def _conv_mixer(h, seq, layer, mixer, g, w_in, w_conv, w_out):
    n_tok = h.shape[0]
    return pl.pallas_call(
        functools.partial(_conv_kernel, seq // CONV_TM, mixer),
        grid=(n_tok // CONV_TM,),
        in_specs=[
            pl.BlockSpec((CONV_TM, D_MODEL), lambda i: (i, 0)),
            _layer((1, D_MODEL), layer),
            HBM,
            _layer((CONV_WIDTH, D_MODEL), mixer),
            HBM,
        ],
        out_specs=pl.BlockSpec((CONV_TM, D_MODEL), lambda i: (i, 0)),
        out_shape=jax.ShapeDtypeStruct((n_tok, D_MODEL), F32),
        scratch_shapes=[pltpu.VMEM((CARRY + CONV_SUB, D_MODEL), F32)]
        + _weight_scratch(D_MODEL, 3 * D_MODEL, LANES) + _weight_scratch(D_MODEL, D_MODEL, MXU_DIM) + [DMA_SEMS],
        compiler_params=_params("arbitrary"),
        name="conv_mixer",
    )(h, g, w_in, w_conv, w_out)


QKV_TM = 1024
QKV_SUB = 512
QKV_TN = 2 * MXU_DIM
N_QKV_TILES = 3 * N_GROUPS


def _qkv_kernel(mixer, group, x_ref, g_ref, w_hbm, gain_ref, bd_ref, perm_ref, o_ref, w_ref, w_stage, sem, xs_ref):
    @pl.when(pl.program_id(0) == 0)
    def _():
        _load_weights(w_hbm.at[mixer, :, pl.ds(group * 3 * D_MODEL, 3 * D_MODEL)], w_ref, w_stage, sem)

    r = DILATION_GROUPS[group][1]
    rows_per_strand = UNIT // r
    for t in range(QKV_TM // QKV_SUB):
        rows = pl.ds(t * QKV_SUB, QKV_SUB)
        x = x_ref[rows, :]
        xn = x * lax.rsqrt(jnp.mean(x * x, axis=-1, keepdims=True) + RMS_EPS) * g_ref[...]
        if r == ROW_STRIDE:
            slabs = range(D_MODEL // LANES)
            for p in slabs:
                xs_ref[t, p] = xn[:, p * LANES:(p + 1) * LANES]
            xn = jnp.concatenate(
                [jnp.concatenate([xs_ref[t, p, pl.ds(u * UNIT + s, rows_per_strand, stride=r), :] for p in slabs],
                                 axis=1)
                 for u in range(QKV_SUB // UNIT) for s in range(r)], axis=0)
        xn = xn.astype(BF16)
        if r > ROW_STRIDE:
            xn = jnp.concatenate(
                [_dot(perm_ref[...], xn[u * UNIT:(u + 1) * UNIT]).astype(BF16) for u in range(QKV_SUB // UNIT)],
                axis=0)
        for c in range(3 * D_MODEL // QKV_TN):
            cols = pl.ds(c * QKV_TN, QKV_TN)
            y = _dot(xn, w_ref[:, cols])
            if c * QKV_TN >= 2 * D_MODEL:
                res = y.astype(BF16)
            else:
                y2 = (y * y).astype(BF16)
                ms = jnp.concatenate(
                    [_dot(y2[:, s * MXU_DIM:(s + 1) * MXU_DIM], bd_ref[...]) for s in range(QKV_TN // MXU_DIM)],
                    axis=1)
                res = (y * lax.rsqrt(ms + RMS_EPS) * gain_ref[:, cols]).astype(BF16)
            for s in range(QKV_TN // LANES):
                o_ref[c * (QKV_TN // LANES) + s, rows, :] = res[:, s * LANES:(s + 1) * LANES]


def _qkv_proj(h, layer, mixer, group, g, w_qkv, q_gain, k_gain):
    n_tok = h.shape[0]
    gains = jnp.concatenate(
        [jnp.tile(q_gain[group], N_HEADS) * (HEAD_DIM ** -0.5 * LOG2E), jnp.tile(k_gain[group], N_HEADS),
         jnp.ones((D_MODEL,), F32)]).reshape(1, 3 * D_MODEL)
    seg = np.arange(MXU_DIM) // HEAD_DIM
    block_diag = jnp.asarray((seg[:, None] == seg[None, :]) / HEAD_DIM, BF16)
    r = DILATION_GROUPS[group][1]
    tok = np.arange(UNIT)
    perm = np.zeros((UNIT, UNIT), np.float32)
    perm[(tok % r) * (UNIT // r) + tok // r, tok] = 1.0
    return pl.pallas_call(
        functools.partial(_qkv_kernel, mixer, group),
        grid=(n_tok // QKV_TM,),
        in_specs=[
            pl.BlockSpec((QKV_TM, D_MODEL), lambda i: (i, 0)),
            _layer((1, D_MODEL), layer),
            HBM,
            _resident((1, 3 * D_MODEL)),
            _resident((MXU_DIM, MXU_DIM)),
            _resident((UNIT, UNIT)),
        ],
        out_specs=pl.BlockSpec((3 * N_PAIRS, QKV_TM, LANES), lambda i: (0, i, 0)),
        out_shape=jax.ShapeDtypeStruct((3 * N_PAIRS, n_tok, LANES), BF16),
        scratch_shapes=_weight_scratch(D_MODEL, 3 * D_MODEL, LANES) + [
            DMA_SEMS, pltpu.VMEM((QKV_TM // QKV_SUB, D_MODEL // LANES, QKV_SUB, LANES), F32)],
        compiler_params=_params("arbitrary"),
        name=f"qkv_proj_g{group}",
    )(h, g, w_qkv, gains, block_diag, jnp.asarray(perm, BF16))


def _t5_bucket_np(dist):
    nf = np.maximum(dist, 1).astype(np.float32)
    large = MAX_EXACT + (np.log(nf / np.float32(MAX_EXACT)) / np.float32(math.log(MAX_DISTANCE / MAX_EXACT))
                         * np.float32(N_BUCKETS - MAX_EXACT)).astype(np.int32)
    large = np.minimum(large, N_BUCKETS - 1)
    return np.where(dist < MAX_EXACT, dist, large)


def _bucket_maps():
    i = (np.arange(2 * BLOCK) % BLOCK)[:, None]
    j = np.arange(2 * BLOCK)[None, :]
    step = BLOCK + i - j
    maps = []
    for window, dilation in DILATION_GROUPS:
        n_steps = window // dilation
        in_band = (step >= 0) & (step <= n_steps)
        bucket = _t5_bucket_np(np.clip(step, 0, n_steps) * dilation)
        maps.append(np.where(in_band, bucket, -1).astype(np.int32))
    return np.stack(maps)


def _bias_kernel(buckets, rel_ref, bm_ref, o_ref):
    p = pl.program_id(0)
    shape = bm_ref.shape[1:]
    top = lax.broadcasted_iota(jnp.int32, shape, 0) < BLOCK
    col = lax.broadcasted_iota(jnp.int32, shape, 1)
    for g in range(N_GROUPS):
        bm = bm_ref[g]
        acc = jnp.zeros(shape, F32)
        for b in buckets[g]:
            val = jnp.where(top, rel_ref[b, g * N_HEADS + 2 * p], rel_ref[b, g * N_HEADS + 2 * p + 1])
            acc = jnp.where(bm == b, val, acc)
        rest = jnp.where(bm < 0, NEG_INF, acc * LOG2E)
        o_ref[g, 1] = rest
        o_ref[g, 0] = jnp.where(col < BLOCK, NEG_INF, rest)


def _bias_tiles(rel_bias):
    tile = (2 * BLOCK, 2 * BLOCK)
    maps = _bucket_maps()
    buckets = tuple(tuple(int(b) for b in np.unique(m[m >= 0])) for m in maps)
    return pl.pallas_call(
        functools.partial(_bias_kernel, buckets),
        grid=(N_PAIRS,),
        in_specs=[pl.BlockSpec(memory_space=pltpu.SMEM), _resident((N_GROUPS,) + tile)],
        out_specs=pl.BlockSpec((N_GROUPS, 2, None) + tile, lambda p: (0, 0, p, 0, 0)),
        out_shape=jax.ShapeDtypeStruct((N_GROUPS, 2, N_PAIRS) + tile, F32),
        compiler_params=_params("arbitrary"),
        name="bias_tiles",
    )(rel_bias, jnp.asarray(maps))


ATTN_BLOCKS = 16


def _attn_kernel(nb, q_ref, kp_ref, kc_ref, vp_ref, vc_ref, bias_ref, o_ref, lse_ref):
    upb = q_ref.shape[1] // nb
    n_strands, rows_per_unit = q_ref.shape[2], q_ref.shape[3]
    not_first = jnp.minimum(pl.program_id(2), 1)
    lane = lax.broadcasted_iota(jnp.int32, (BLOCK, LANES), 1)
    lo = lane < HEAD_DIM
    ones = jnp.ones((2 * BLOCK, LANES), BF16)

    def keys_of(prev_ref, cur_ref, blk, p, j):
        if blk == 0:
            return jnp.concatenate([prev_ref[p, :, j].reshape(BLOCK, LANES),
                                    cur_ref[p, pl.ds(0, upb), j].reshape(BLOCK, LANES)], axis=0)
        return cur_ref[p, pl.ds((blk - 1) * upb, 2 * upb), j].reshape(2 * BLOCK, LANES)

    for j in range(n_strands):
        for blk in range(nb):
            units = pl.ds(blk * upb, upb)
            m_all = jnp.zeros((BLOCK, LANES), F32)
            s_all = jnp.ones((BLOCK, LANES), F32)
            for p in range(N_PAIRS):
                q = q_ref[p, units, j].reshape(BLOCK, LANES)
                zero = jnp.zeros_like(q)
                q2 = jnp.concatenate([jnp.where(lo, q, zero), jnp.where(lo, zero, q)], axis=0)
                k = keys_of(kp_ref, kc_ref, blk, p, j)
                logits = lax.dot_general(q2, k, (((1,), (1,)), ((), ())), preferred_element_type=F32)
                logits = logits + (bias_ref[not_first, p] if blk == 0 else bias_ref[1, p])
                m = jnp.max(logits, axis=-1, keepdims=True)
                prob = jnp.exp2(logits - m).astype(BF16)
                acc = _dot(prob, jnp.concatenate([keys_of(vp_ref, vc_ref, blk, p, j), ones], axis=1))
                o = jnp.where(lo, acc[:BLOCK, :LANES], acc[BLOCK:, :LANES])
                s = jnp.where(lo, acc[:BLOCK, LANES:], acc[BLOCK:, LANES:])
                mm = jnp.where(lo, m[:BLOCK], m[BLOCK:])
                o_ref[p, units, j] = (o / s).astype(BF16).reshape(upb, rows_per_unit, LANES)
                keep = (lane == 2 * p) | (lane == HEAD_DIM + 2 * p + 1)
                m_all = jnp.where(keep, mm, m_all)
                s_all = jnp.where(keep, s, s_all)
            lse_ref[units, j] = (m_all * LN2 + jnp.log(s_all)).reshape(upb, rows_per_unit, LANES)


def _attn_group(qkv, bias, g, batch, seq):
    r = DILATION_GROUPS[g][1]
    rows_per_unit = min(UNIT // r, BLOCK)
    n_units = seq // (r * rows_per_unit)
    upb = BLOCK // rows_per_unit
    nb = min(ATTN_BLOCKS, seq // (r * BLOCK))
    n_strands = ATTN_BLOCKS // nb
    view = qkv.reshape(3 * N_PAIRS, batch, n_units, r, rows_per_unit, LANES)

    def cur(which):
        return lambda b, s, c: (which, b, c, s, 0, 0)

    def prev(which):
        return lambda b, s, c: (which, b, jnp.maximum(c * nb - 1, 0), s, 0, 0)

    cur_blk = (N_PAIRS, None, nb * upb, n_strands, rows_per_unit, LANES)
    prev_blk = (N_PAIRS, None, upb, n_strands, rows_per_unit, LANES)
    o, lse = pl.pallas_call(
        functools.partial(_attn_kernel, nb),
        grid=(batch, r // n_strands, seq // (r * nb * BLOCK)),
        in_specs=[
            pl.BlockSpec(cur_blk, cur(0)),
            pl.BlockSpec(prev_blk, prev(1)),
            pl.BlockSpec(cur_blk, cur(1)),
            pl.BlockSpec(prev_blk, prev(2)),
            pl.BlockSpec(cur_blk, cur(2)),
            pl.BlockSpec((None, 2, N_PAIRS, 2 * BLOCK, 2 * BLOCK), lambda b, s, c: (g, 0, 0, 0, 0)),
        ],
        out_specs=[
            pl.BlockSpec(cur_blk, lambda b, s, c: (0, b, c, s, 0, 0)),
            pl.BlockSpec((None, nb * upb, n_strands, rows_per_unit, LANES), lambda b, s, c: (b, c, s, 0, 0)),
        ],
        out_shape=[
            jax.ShapeDtypeStruct((N_PAIRS, batch, n_units, r, rows_per_unit, LANES), BF16),
            jax.ShapeDtypeStruct((batch, n_units, r, rows_per_unit, LANES), F32),
        ],
        compiler_params=_params("arbitrary", "arbitrary", "arbitrary"),
        name=f"attn_r{r}",
    )(view, view, view, view, view, bias)
    return o.reshape(N_PAIRS, batch * seq, LANES), lse.reshape(batch * seq, LANES)


MERGE_TM = 1024
MERGE_SUB = 512


def _merge_kernel(mixer, o0_ref, o1_ref, o2_ref, l0_ref, l1_ref, l2_ref, e_ref, w_hbm, out_ref,
                  w_ref, w_stage, sem, tok_ref, ltok_ref, tmp_ref, ltmp_ref, merged_ref):
    @pl.when(pl.program_id(0) == 0)
    def _():
        _load_weights(w_hbm.at[mixer], w_ref, w_stage, sem)

    o_refs = (o0_ref, o1_ref, o2_ref)
    l_refs = (l0_ref, l1_ref, l2_ref)
    for t in range(MERGE_TM // MERGE_SUB):
        first = t * MERGE_SUB
        rows = pl.ds(first, MERGE_SUB)
        lses = [l0_ref[rows, :]]
        for g in range(1, N_GROUPS):
            r = DILATION_GROUPS[g][1]
            rows_per_strand = UNIT // r
            for u in range(MERGE_SUB // UNIT):
                unit = first + u * UNIT
                if r == ROW_STRIDE:
                    for s in range(r):
                        src_rows = pl.ds(unit + s * rows_per_strand, rows_per_strand)
                        dst_rows = pl.ds(unit + s, rows_per_strand, stride=r)
                        ltok_ref[g - 1, dst_rows, :] = l_refs[g][src_rows, :]
                        for p in range(N_PAIRS):
                            tok_ref[g - 1, p, dst_rows, :] = o_refs[g][p, src_rows, :].astype(F32)
                else:
                    assert _stride_passes(r) == 2
                    quarter = UNIT // ROW_STRIDE
                    for a in range(ROW_STRIDE):
                        for b in range(ROW_STRIDE):
                            src_rows = pl.ds(unit + (ROW_STRIDE * a + b) * rows_per_strand, rows_per_strand)
                            dst_rows = pl.ds(unit + b * quarter + a, rows_per_strand, stride=ROW_STRIDE)
                            ltmp_ref[dst_rows, :] = l_refs[g][src_rows, :]
                            for p in range(N_PAIRS):
                                tmp_ref[p, dst_rows, :] = o_refs[g][p, src_rows, :].astype(F32)
                    for b in range(ROW_STRIDE):
                        src_rows = pl.ds(unit + b * quarter, quarter)
                        dst_rows = pl.ds(unit + b, quarter, stride=ROW_STRIDE)
                        ltok_ref[g - 1, dst_rows, :] = ltmp_ref[src_rows, :]
                        for p in range(N_PAIRS):
                            tok_ref[g - 1, p, dst_rows, :] = tmp_ref[p, src_rows, :]
            lses.append(ltok_ref[g - 1, rows, :])
        m = jnp.maximum(jnp.maximum(lses[0], lses[1]), lses[2])
        es = [jnp.exp(l - m) for l in lses]
        den = es[0] + es[1] + es[2]
        w_cat = []
        for g in range(1, N_GROUPS):
            w = es[g] / den
            w_hi = w.astype(BF16)
            w_cat.append(jnp.concatenate([w_hi, (w - w_hi.astype(F32)).astype(BF16)], axis=1))
        for c in range(D_MODEL // MXU_DIM):
            pairs = range(c * MXU_DIM // LANES, (c + 1) * MXU_DIM // LANES)
            base = jnp.concatenate([o0_ref[p, rows, :] for p in pairs], axis=1).astype(F32)
            acc = base
            for g in range(1, N_GROUPS):
                w_full = _dot(w_cat[g - 1], e_ref[:, pl.ds(c * MXU_DIM, MXU_DIM)])
                acc = acc + w_full * (jnp.concatenate([tok_ref[g - 1, p, rows, :] for p in pairs], axis=1) - base)
            merged_ref[rows, pl.ds(c * MXU_DIM, MXU_DIM)] = acc.astype(BF16)
        out_ref[rows, :] = _dot(merged_ref[rows, :], w_ref[...])


def _merge_proj(outs, lses, mixer, w_out):
    n_tok = lses[0].shape[0]
    expand = np.zeros((2 * LANES, D_MODEL), np.float32)
    for head in range(N_HEADS):
        src = head if head % 2 == 0 else HEAD_DIM + head
        expand[src, head * HEAD_DIM:(head + 1) * HEAD_DIM] = 1.0
    expand[LANES:] = expand[:LANES]
    tok = lambda width: pl.BlockSpec((MERGE_TM, width), lambda i: (i, 0))
    slab = pl.BlockSpec((N_PAIRS, MERGE_TM, LANES), lambda i: (0, i, 0))
    return pl.pallas_call(
        functools.partial(_merge_kernel, mixer),
        grid=(n_tok // MERGE_TM,),
        in_specs=[slab] * N_GROUPS + [tok(LANES)] * N_GROUPS
        + [_resident((2 * LANES, D_MODEL)), HBM],
        out_specs=tok(D_MODEL),
        out_shape=jax.ShapeDtypeStruct((n_tok, D_MODEL), F32),
        scratch_shapes=_weight_scratch(D_MODEL, D_MODEL, MXU_DIM) + [
            DMA_SEMS,
            pltpu.VMEM((N_GROUPS - 1, N_PAIRS, MERGE_TM, LANES), F32),
            pltpu.VMEM((N_GROUPS - 1, MERGE_TM, LANES), F32),
            pltpu.VMEM((N_PAIRS, MERGE_TM, LANES), F32),
            pltpu.VMEM((MERGE_TM, LANES), F32),
            pltpu.VMEM((MERGE_TM, D_MODEL), BF16),
        ],
        compiler_params=_params("arbitrary"),
        name="merge_proj",
    )(*outs, *lses, jnp.asarray(expand, BF16), w_out)


def kernel(x, norm_ffn1, ffn1_w_in, ffn1_w_out, norm_mix, conv_w_in, conv_w, conv_w_out, attn_w_qkv,
           attn_q_gain, attn_k_gain, attn_w_out, rel_bias, norm_ffn2, ffn2_w_in, ffn2_w_out):
    batch, seq, d_model = x.shape
    assert d_model == D_MODEL and seq % max(QKV_TM, CONV_TM, DILATION_GROUPS[-1][1] * BLOCK) == 0
    depth = norm_ffn1.shape[0]
    h = x.reshape(batch * seq, D_MODEL)
    bias = _bias_tiles(rel_bias) if depth > 1 else None
    gains = [a.reshape(depth, 1, D_MODEL) for a in (norm_ffn1, norm_mix, norm_ffn2)]
    for i in range(depth):
        h = _ffn([h], i, gains[0], ffn1_w_in, ffn1_w_out)
        j = i // 2
        if i % 2 == 0:
            mixed = [_conv_mixer(h, seq, i, j, gains[1], conv_w_in, conv_w, conv_w_out)]
        else:
            parts = [_attn_group(_qkv_proj(h, i, j, g, gains[1], attn_w_qkv, attn_q_gain[j], attn_k_gain[j]),
                                 bias, g, batch, seq) for g in range(N_GROUPS)]
            mixed = [h, _merge_proj([o for o, _ in parts], [l for _, l in parts], j, attn_w_out)]
        h = _ffn(mixed, i, gains[2], ffn2_w_in, ffn2_w_out)
    return h.reshape(batch, seq, D_MODEL)
```

```python
import functools
import math

import numpy as np
import jax
import jax.numpy as jnp
from jax import lax
from jax.experimental import pallas as pl
from jax.experimental.pallas import tpu as pltpu

D_MODEL = 1024
D_FF = 2816
CONV_WIDTH = 3
HEAD_DIM = 64
N_HEADS = D_MODEL // HEAD_DIM
N_PAIRS = N_HEADS // 2
DILATION_GROUPS = ((128, 1), (512, 4), (2048, 16))
N_GROUPS = len(DILATION_GROUPS)
BLOCK = 128
N_BUCKETS = 32
MAX_EXACT = N_BUCKETS // 2
MAX_DISTANCE = 2048
RMS_EPS = 1e-6
NEG_INF = -1e30
LOG2E = math.log2(math.e)
LN2 = math.log(2.0)

LANES = 128
MXU_DIM = 256
UNIT = MXU_DIM
VMEM_LIMIT = 56 * 1024 * 1024

F32 = jnp.float32
BF16 = jnp.bfloat16


def _rmsnorm_bf16(x, g):
    ms = jnp.mean(x * x, axis=-1, keepdims=True)
    return (x * lax.rsqrt(ms + RMS_EPS) * g).astype(BF16)


def _dot(a, b):
    return jnp.dot(a, b, preferred_element_type=F32)


def _resident(shape):
    return pl.BlockSpec(shape, lambda *_: (0,) * len(shape), pipeline_mode=pl.Buffered(1))


def _layer(shape, layer):
    return pl.BlockSpec((None,) + shape, lambda *_: (layer,) + (0,) * len(shape), pipeline_mode=pl.Buffered(1))


def _load_weights(src, dst_ref, stage_ref, sem):
    rows = stage_ref.shape[1]
    n_chunks = dst_ref.shape[0] // rows

    def copy(c):
        return pltpu.make_async_copy(src.at[pl.ds(c * rows, rows)], stage_ref.at[c % 2], sem.at[c % 2])

    copy(0).start()
    for c in range(n_chunks):
        if c + 1 < n_chunks:
            copy(c + 1).start()
        copy(c).wait()
        dst_ref[pl.ds(c * rows, rows), :] = stage_ref[c % 2].astype(BF16)


def _weight_scratch(rows, cols, stage_rows):
    assert rows % stage_rows == 0
    return [pltpu.VMEM((rows, cols), BF16), pltpu.VMEM((2, stage_rows, cols), F32)]


HBM = pl.BlockSpec(memory_space=pl.ANY)
DMA_SEMS = pltpu.SemaphoreType.DMA((2,))


def _params(*sem):
    return pltpu.CompilerParams(dimension_semantics=sem, vmem_limit_bytes=VMEM_LIMIT)


assert DILATION_GROUPS[0][1] == 1

ROW_STRIDE = 4


def _stride_passes(r):
    assert r in (1, ROW_STRIDE, ROW_STRIDE ** 2)
    return {1: 1, ROW_STRIDE: 1, ROW_STRIDE ** 2: 2}[r]


FFN_TM = 1024
FFN_SUB = 512
FFN_CHUNKS = ((0, 5 * MXU_DIM), (5 * MXU_DIM, 6 * MXU_DIM))


def _ffn_kernel(layer, n_inputs, *refs):
    x_refs = refs[:n_inputs]
    g_ref, win_hbm, wout_hbm, o_ref, win_ref, win_stage, wout_ref, wout_stage, sem = refs[n_inputs:]

    @pl.when(pl.program_id(0) == 0)
    def _():
        _load_weights(win_hbm.at[layer], win_ref, win_stage, sem)
        _load_weights(wout_hbm.at[layer], wout_ref, wout_stage, sem)

    for t in range(FFN_TM // FFN_SUB):
        rows = pl.ds(t * FFN_SUB, FFN_SUB)
        x = x_refs[0][rows, :]
        for extra_ref in x_refs[1:]:
            x = x + extra_ref[rows, :]
        xn = _rmsnorm_bf16(x, g_ref[...])
        acc = None
        for start, size in FFN_CHUNKS:
            gate = _dot(xn, win_ref[:, pl.ds(start, size)])
            up = _dot(xn, win_ref[:, pl.ds(D_FF + start, size)])
            a = (gate * jax.nn.sigmoid(gate) * up).astype(BF16)
            part = _dot(a, wout_ref[pl.ds(start, size), :])
            acc = part if acc is None else acc + part
        o_ref[rows, :] = x + 0.5 * acc


def _ffn(hs, layer, g, w_in, w_out):
    n_tok = hs[0].shape[0]
    tile = pl.BlockSpec((FFN_TM, D_MODEL), lambda i: (i, 0))
    return pl.pallas_call(
        functools.partial(_ffn_kernel, layer, len(hs)),
        grid=(n_tok // FFN_TM,),
        in_specs=[tile] * len(hs) + [_layer((1, D_MODEL), layer), HBM, HBM],
        out_specs=tile,
        out_shape=jax.ShapeDtypeStruct((n_tok, D_MODEL), F32),
        scratch_shapes=_weight_scratch(D_MODEL, 2 * D_FF, LANES // len(hs)) + _weight_scratch(D_FF, D_MODEL, D_FF // 8)
        + [DMA_SEMS],
        compiler_params=_params("arbitrary"),
        name="ffn",
    )(*hs, g, w_in, w_out)


CONV_TM = 1024
CONV_SUB = 512
CARRY = 8


def _conv_kernel(tiles_per_seq, mixer, x_ref, g_ref, win_hbm, cw_ref, wout_hbm, o_ref, vp_ref,
                 win_ref, win_stage, wout_ref, wout_stage, sem):
    @pl.when(pl.program_id(0) == 0)
    def _():
        _load_weights(win_hbm.at[mixer], win_ref, win_stage, sem)
        _load_weights(wout_hbm.at[mixer], wout_ref, wout_stage, sem)

    @pl.when(pl.program_id(0) % tiles_per_seq == 0)
    def _():
        vp_ref[pl.ds(0, CARRY), :] = jnp.zeros((CARRY, D_MODEL), F32)

    for t in range(CONV_TM // CONV_SUB):
        rows = pl.ds(t * CONV_SUB, CONV_SUB)
        x = x_ref[rows, :]
        xn = _rmsnorm_bf16(x, g_ref[...])
        b_gate = _dot(xn, win_ref[:, pl.ds(0, D_MODEL)])
        c_gate = _dot(xn, win_ref[:, pl.ds(D_MODEL, D_MODEL)])
        u = _dot(xn, win_ref[:, pl.ds(2 * D_MODEL, D_MODEL)])
        v = c_gate * u
        vp_ref[pl.ds(CARRY, CONV_SUB), :] = v
        conv = cw_ref[pl.ds(0, 1), :] * v
        for lag in range(1, CONV_WIDTH):
            conv = conv + cw_ref[pl.ds(lag, 1), :] * vp_ref[pl.ds(CARRY - lag, CONV_SUB), :]
        z = (b_gate * conv).astype(BF16)
        o_ref[rows, :] = x + _dot(z, wout_ref[...])
        vp_ref[pl.ds(0, CARRY), :] = vp_ref[pl.ds(CONV_SUB, CARRY), :]


def _conv_mixer(h, seq, layer, mixer, g, w_in, w_conv, w_out):
    n_tok = h.shape[0]
    return pl.pallas_call(
        functools.partial(_conv_kernel, seq // CONV_TM, mixer),
        grid=(n_tok // CONV_TM,),
        in_specs=[
            pl.BlockSpec((CONV_TM, D_MODEL), lambda i: (i, 0)),
            _layer((1, D_MODEL), layer),
            HBM,
            _layer((CONV_WIDTH, D_MODEL), mixer),
            HBM,
        ],
        out_specs=pl.BlockSpec((CONV_TM, D_MODEL), lambda i: (i, 0)),
        out_shape=jax.ShapeDtypeStruct((n_tok, D_MODEL), F32),
        scratch_shapes=[pltpu.VMEM((CARRY + CONV_SUB, D_MODEL), F32)]
        + _weight_scratch(D_MODEL, 3 * D_MODEL, LANES) + _weight_scratch(D_MODEL, D_MODEL, MXU_DIM) + [DMA_SEMS],
        compiler_params=_params("arbitrary"),
        name="conv_mixer",
    )(h, g, w_in, w_conv, w_out)


QKV_TM = 1024
QKV_SUB = 512
QKV_TN = 2 * MXU_DIM
N_QKV_TILES = 3 * N_GROUPS


def _qkv_kernel(mixer, group, x_ref, g_ref, w_hbm, gain_ref, bd_ref, perm_ref, o_ref, w_ref, w_stage, sem, xs_ref):
    @pl.when(pl.program_id(0) == 0)
    def _():
        _load_weights(w_hbm.at[mixer, :, pl.ds(group * 3 * D_MODEL, 3 * D_MODEL)], w_ref, w_stage, sem)

    r = DILATION_GROUPS[group][1]
    rows_per_strand = UNIT // r
    for t in range(QKV_TM // QKV_SUB):
        rows = pl.ds(t * QKV_SUB, QKV_SUB)
        x = x_ref[rows, :]
        xn = x * lax.rsqrt(jnp.mean(x * x, axis=-1, keepdims=True) + RMS_EPS) * g_ref[...]
        if r == ROW_STRIDE:
            slabs = range(D_MODEL // LANES)
            for p in slabs:
                xs_ref[t, p] = xn[:, p * LANES:(p + 1) * LANES]
            xn = jnp.concatenate(
                [jnp.concatenate([xs_ref[t, p, pl.ds(u * UNIT + s, rows_per_strand, stride=r), :] for p in slabs],
                                 axis=1)
                 for u in range(QKV_SUB // UNIT) for s in range(r)], axis=0)
        xn = xn.astype(BF16)
        if r > ROW_STRIDE:
            xn = jnp.concatenate(
                [_dot(perm_ref[...], xn[u * UNIT:(u + 1) * UNIT]).astype(BF16) for u in range(QKV_SUB // UNIT)],
                axis=0)
        for c in range(3 * D_MODEL // QKV_TN):
            cols = pl.ds(c * QKV_TN, QKV_TN)
            y = _dot(xn, w_ref[:, cols])
            if c * QKV_TN >= 2 * D_MODEL:
                res = y.astype(BF16)
            else:
                y2 = (y * y).astype(BF16)
                ms = jnp.concatenate(
                    [_dot(y2[:, s * MXU_DIM:(s + 1) * MXU_DIM], bd_ref[...]) for s in range(QKV_TN // MXU_DIM)],
                    axis=1)
                res = (y * lax.rsqrt(ms + RMS_EPS) * gain_ref[:, cols]).astype(BF16)
            for s in range(QKV_TN // LANES):
                o_ref[c * (QKV_TN // LANES) + s, rows, :] = res[:, s * LANES:(s + 1) * LANES]


def _qkv_proj(h, layer, mixer, group, g, w_qkv, q_gain, k_gain):
    n_tok = h.shape[0]
    gains = jnp.concatenate(
        [jnp.tile(q_gain[group], N_HEADS) * (HEAD_DIM ** -0.5 * LOG2E), jnp.tile(k_gain[group], N_HEADS),
         jnp.ones((D_MODEL,), F32)]).reshape(1, 3 * D_MODEL)
    seg = np.arange(MXU_DIM) // HEAD_DIM
    block_diag = jnp.asarray((seg[:, None] == seg[None, :]) / HEAD_DIM, BF16)
    r = DILATION_GROUPS[group][1]
    tok = np.arange(UNIT)
    perm = np.zeros((UNIT, UNIT), np.float32)
    perm[(tok % r) * (UNIT // r) + tok // r, tok] = 1.0
    return pl.pallas_call(
        functools.partial(_qkv_kernel, mixer, group),
        grid=(n_tok // QKV_TM,),
        in_specs=[
            pl.BlockSpec((QKV_TM, D_MODEL), lambda i: (i, 0)),
            _layer((1, D_MODEL), layer),
            HBM,
            _resident((1, 3 * D_MODEL)),
            _resident((MXU_DIM, MXU_DIM)),
            _resident((UNIT, UNIT)),
        ],
        out_specs=pl.BlockSpec((3 * N_PAIRS, QKV_TM, LANES), lambda i: (0, i, 0)),
        out_shape=jax.ShapeDtypeStruct((3 * N_PAIRS, n_tok, LANES), BF16),
        scratch_shapes=_weight_scratch(D_MODEL, 3 * D_MODEL, LANES) + [
            DMA_SEMS, pltpu.VMEM((QKV_TM // QKV_SUB, D_MODEL // LANES, QKV_SUB, LANES), F32)],
        compiler_params=_params("arbitrary"),
        name=f"qkv_proj_g{group}",
    )(h, g, w_qkv, gains, block_diag, jnp.asarray(perm, BF16))


def _t5_bucket_np(dist):
    nf = np.maximum(dist, 1).astype(np.float32)
    large = MAX_EXACT + (np.log(nf / np.float32(MAX_EXACT)) / np.float32(math.log(MAX_DISTANCE / MAX_EXACT))
                         * np.float32(N_BUCKETS - MAX_EXACT)).astype(np.int32)
    large = np.minimum(large, N_BUCKETS - 1)
    return np.where(dist < MAX_EXACT, dist, large)


def _bucket_maps():
    i = (np.arange(2 * BLOCK) % BLOCK)[:, None]
    j = np.arange(2 * BLOCK)[None, :]
    step = BLOCK + i - j
    maps = []
    for window, dilation in DILATION_GROUPS:
        n_steps = window // dilation
        in_band = (step >= 0) & (step <= n_steps)
        bucket = _t5_bucket_np(np.clip(step, 0, n_steps) * dilation)
        maps.append(np.where(in_band, bucket, -1).astype(np.int32))
    return np.stack(maps)


def _bias_kernel(buckets, rel_ref, bm_ref, o_ref):
    p = pl.program_id(0)
    shape = bm_ref.shape[1:]
    top = lax.broadcasted_iota(jnp.int32, shape, 0) < BLOCK
    col = lax.broadcasted_iota(jnp.int32, shape, 1)
    for g in range(N_GROUPS):
        bm = bm_ref[g]
        acc = jnp.zeros(shape, F32)
        for b in buckets[g]:
            val = jnp.where(top, rel_ref[b, g * N_HEADS + 2 * p], rel_ref[b, g * N_HEADS + 2 * p + 1])
            acc = jnp.where(bm == b, val, acc)
        rest = jnp.where(bm < 0, NEG_INF, acc * LOG2E)
        o_ref[g, 1] = rest
        o_ref[g, 0] = jnp.where(col < BLOCK, NEG_INF, rest)


def _bias_tiles(rel_bias):
    tile = (2 * BLOCK, 2 * BLOCK)
    maps = _bucket_maps()
    buckets = tuple(tuple(int(b) for b in np.unique(m[m >= 0])) for m in maps)
    return pl.pallas_call(
        functools.partial(_bias_kernel, buckets),
        grid=(N_PAIRS,),
        in_specs=[pl.BlockSpec(memory_space=pltpu.SMEM), _resident((N_GROUPS,) + tile)],
        out_specs=pl.BlockSpec((N_GROUPS, 2, None) + tile, lambda p: (0, 0, p, 0, 0)),
        out_shape=jax.ShapeDtypeStruct((N_GROUPS, 2, N_PAIRS) + tile, F32),
        compiler_params=_params("arbitrary"),
        name="bias_tiles",
    )(rel_bias, jnp.asarray(maps))


ATTN_BLOCKS = 16


def _attn_kernel(nb, q_ref, kp_ref, kc_ref, vp_ref, vc_ref, bias_ref, o_ref, lse_ref):
    upb = q_ref.shape[1] // nb
    n_strands, rows_per_unit = q_ref.shape[2], q_ref.shape[3]
    not_first = jnp.minimum(pl.program_id(2), 1)
    lane = lax.broadcasted_iota(jnp.int32, (BLOCK, LANES), 1)
    lo = lane < HEAD_DIM
    ones = jnp.ones((2 * BLOCK, LANES), BF16)

    def keys_of(prev_ref, cur_ref, blk, p, j):
        if blk == 0:
            return jnp.concatenate([prev_ref[p, :, j].reshape(BLOCK, LANES),
                                    cur_ref[p, pl.ds(0, upb), j].reshape(BLOCK, LANES)], axis=0)
        return cur_ref[p, pl.ds((blk - 1) * upb, 2 * upb), j].reshape(2 * BLOCK, LANES)

    for j in range(n_strands):
        for blk in range(nb):
            units = pl.ds(blk * upb, upb)
            m_all = jnp.zeros((BLOCK, LANES), F32)
            s_all = jnp.ones((BLOCK, LANES), F32)
            for p in range(N_PAIRS):
                q = q_ref[p, units, j].reshape(BLOCK, LANES)
                zero = jnp.zeros_like(q)
                q2 = jnp.concatenate([jnp.where(lo, q, zero), jnp.where(lo, zero, q)], axis=0)
                k = keys_of(kp_ref, kc_ref, blk, p, j)
                logits = lax.dot_general(q2, k, (((1,), (1,)), ((), ())), preferred_element_type=F32)
                logits = logits + (bias_ref[not_first, p] if blk == 0 else bias_ref[1, p])
                m = jnp.max(logits, axis=-1, keepdims=True)
                prob = jnp.exp2(logits - m).astype(BF16)
                acc = _dot(prob, jnp.concatenate([keys_of(vp_ref, vc_ref, blk, p, j), ones], axis=1))
                o = jnp.where(lo, acc[:BLOCK, :LANES], acc[BLOCK:, :LANES])
                s = jnp.where(lo, acc[:BLOCK, LANES:], acc[BLOCK:, LANES:])
                mm = jnp.where(lo, m[:BLOCK], m[BLOCK:])
                o_ref[p, units, j] = (o / s).astype(BF16).reshape(upb, rows_per_unit, LANES)
                keep = (lane == 2 * p) | (lane == HEAD_DIM + 2 * p + 1)
                m_all = jnp.where(keep, mm, m_all)
                s_all = jnp.where(keep, s, s_all)
            lse_ref[units, j] = (m_all * LN2 + jnp.log(s_all)).reshape(upb, rows_per_unit, LANES)


def _attn_group(qkv, bias, g, batch, seq):
    r = DILATION_GROUPS[g][1]
    rows_per_unit = min(UNIT // r, BLOCK)
    n_units = seq // (r * rows_per_unit)
    upb = BLOCK // rows_per_unit
    nb = min(ATTN_BLOCKS, seq // (r * BLOCK))
    n_strands = ATTN_BLOCKS // nb
    view = qkv.reshape(3 * N_PAIRS, batch, n_units, r, rows_per_unit, LANES)

    def cur(which):
        return lambda b, s, c: (which, b, c, s, 0, 0)

    def prev(which):
        return lambda b, s, c: (which, b, jnp.maximum(c * nb - 1, 0), s, 0, 0)

    cur_blk = (N_PAIRS, None, nb * upb, n_strands, rows_per_unit, LANES)
    prev_blk = (N_PAIRS, None, upb, n_strands, rows_per_unit, LANES)
    o, lse = pl.pallas_call(
        functools.partial(_attn_kernel, nb),
        grid=(batch, r // n_strands, seq // (r * nb * BLOCK)),
        in_specs=[
            pl.BlockSpec(cur_blk, cur(0)),
            pl.BlockSpec(prev_blk, prev(1)),
            pl.BlockSpec(cur_blk, cur(1)),
            pl.BlockSpec(prev_blk, prev(2)),
            pl.BlockSpec(cur_blk, cur(2)),
            pl.BlockSpec((None, 2, N_PAIRS, 2 * BLOCK, 2 * BLOCK), lambda b, s, c: (g, 0, 0, 0, 0)),
        ],
        out_specs=[
            pl.BlockSpec(cur_blk, lambda b, s, c: (0, b, c, s, 0, 0)),
            pl.BlockSpec((None, nb * upb, n_strands, rows_per_unit, LANES), lambda b, s, c: (b, c, s, 0, 0)),
        ],
        out_shape=[
            jax.ShapeDtypeStruct((N_PAIRS, batch, n_units, r, rows_per_unit, LANES), BF16),
            jax.ShapeDtypeStruct((batch, n_units, r, rows_per_unit, LANES), F32),
        ],
        compiler_params=_params("arbitrary", "arbitrary", "arbitrary"),
        name=f"attn_r{r}",
    )(view, view, view, view, view, bias)
    return o.reshape(N_PAIRS, batch * seq, LANES), lse.reshape(batch * seq, LANES)


MERGE_TM = 1024
MERGE_SUB = 512


def _merge_kernel(mixer, o0_ref, o1_ref, o2_ref, l0_ref, l1_ref, l2_ref, e_ref, w_hbm, out_ref,
                  w_ref, w_stage, sem, tok_ref, ltok_ref, tmp_ref, ltmp_ref, merged_ref):
    @pl.when(pl.program_id(0) == 0)
    def _():
        _load_weights(w_hbm.at[mixer], w_ref, w_stage, sem)

    o_refs = (o0_ref, o1_ref, o2_ref)
    l_refs = (l0_ref, l1_ref, l2_ref)
    for t in range(MERGE_TM // MERGE_SUB):
        first = t * MERGE_SUB
        rows = pl.ds(first, MERGE_SUB)
        lses = [l0_ref[rows, :]]
        for g in range(1, N_GROUPS):
            r = DILATION_GROUPS[g][1]
            rows_per_strand = UNIT // r
            for u in range(MERGE_SUB // UNIT):
                unit = first + u * UNIT
                if r == ROW_STRIDE:
                    for s in range(r):
                        src_rows = pl.ds(unit + s * rows_per_strand, rows_per_strand)
                        dst_rows = pl.ds(unit + s, rows_per_strand, stride=r)
                        ltok_ref[g - 1, dst_rows, :] = l_refs[g][src_rows, :]
                        for p in range(N_PAIRS):
                            tok_ref[g - 1, p, dst_rows, :] = o_refs[g][p, src_rows, :].astype(F32)
                else:
                    assert _stride_passes(r) == 2
                    quarter = UNIT // ROW_STRIDE
                    for a in range(ROW_STRIDE):
                        for b in range(ROW_STRIDE):
                            src_rows = pl.ds(unit + (ROW_STRIDE * a + b) * rows_per_strand, rows_per_strand)
                            dst_rows = pl.ds(unit + b * quarter + a, rows_per_strand, stride=ROW_STRIDE)
                            ltmp_ref[dst_rows, :] = l_refs[g][src_rows, :]
                            for p in range(N_PAIRS):
                                tmp_ref[p, dst_rows, :] = o_refs[g][p, src_rows, :].astype(F32)
                    for b in range(ROW_STRIDE):
                        src_rows = pl.ds(unit + b * quarter, quarter)
                        dst_rows = pl.ds(unit + b, quarter, stride=ROW_STRIDE)
                        ltok_ref[g - 1, dst_rows, :] = ltmp_ref[src_rows, :]
                        for p in range(N_PAIRS):
                            tok_ref[g - 1, p, dst_rows, :] = tmp_ref[p, src_rows, :]
            lses.append(ltok_ref[g - 1, rows, :])
        m = jnp.maximum(jnp.maximum(lses[0], lses[1]), lses[2])
        es = [jnp.exp(l - m) for l in lses]
        den = es[0] + es[1] + es[2]
        w_cat = []
        for g in range(1, N_GROUPS):
            w = es[g] / den
            w_hi = w.astype(BF16)
            w_cat.append(jnp.concatenate([w_hi, (w - w_hi.astype(F32)).astype(BF16)], axis=1))
        for c in range(D_MODEL // MXU_DIM):
            pairs = range(c * MXU_DIM // LANES, (c + 1) * MXU_DIM // LANES)
            base = jnp.concatenate([o0_ref[p, rows, :] for p in pairs], axis=1).astype(F32)
            acc = base
            for g in range(1, N_GROUPS):
                w_full = _dot(w_cat[g - 1], e_ref[:, pl.ds(c * MXU_DIM, MXU_DIM)])
                acc = acc + w_full * (jnp.concatenate([tok_ref[g - 1, p, rows, :] for p in pairs], axis=1) - base)
            merged_ref[rows, pl.ds(c * MXU_DIM, MXU_DIM)] = acc.astype(BF16)
        out_ref[rows, :] = _dot(merged_ref[rows, :], w_ref[...])


def _merge_proj(outs, lses, mixer, w_out):
    n_tok = lses[0].shape[0]
    expand = np.zeros((2 * LANES, D_MODEL), np.float32)
    for head in range(N_HEADS):
        src = head if head % 2 == 0 else HEAD_DIM + head
        expand[src, head * HEAD_DIM:(head + 1) * HEAD_DIM] = 1.0
    expand[LANES:] = expand[:LANES]
    tok = lambda width: pl.BlockSpec((MERGE_TM, width), lambda i: (i, 0))
    slab = pl.BlockSpec((N_PAIRS, MERGE_TM, LANES), lambda i: (0, i, 0))
    return pl.pallas_call(
        functools.partial(_merge_kernel, mixer),
        grid=(n_tok // MERGE_TM,),
        in_specs=[slab] * N_GROUPS + [tok(LANES)] * N_GROUPS
        + [_resident((2 * LANES, D_MODEL)), HBM],
        out_specs=tok(D_MODEL),
        out_shape=jax.ShapeDtypeStruct((n_tok, D_MODEL), F32),
        scratch_shapes=_weight_scratch(D_MODEL, D_MODEL, MXU_DIM) + [
            DMA_SEMS,
            pltpu.VMEM((N_GROUPS - 1, N_PAIRS, MERGE_TM, LANES), F32),
            pltpu.VMEM((N_GROUPS - 1, MERGE_TM, LANES), F32),
            pltpu.VMEM((N_PAIRS, MERGE_TM, LANES), F32),
            pltpu.VMEM((MERGE_TM, LANES), F32),
            pltpu.VMEM((MERGE_TM, D_MODEL), BF16),
        ],
        compiler_params=_params("arbitrary"),
        name="merge_proj",
    )(*outs, *lses, jnp.asarray(expand, BF16), w_out)


def kernel(x, norm_ffn1, ffn1_w_in, ffn1_w_out, norm_mix, conv_w_in, conv_w, conv_w_out, attn_w_qkv,
           attn_q_gain, attn_k_gain, attn_w_out, rel_bias, norm_ffn2, ffn2_w_in, ffn2_w_out):
    batch, seq, d_model = x.shape
    assert d_model == D_MODEL and seq % max(QKV_TM, CONV_TM, DILATION_GROUPS[-1][1] * BLOCK) == 0
    depth = norm_ffn1.shape[0]
    h = x.reshape(batch * seq, D_MODEL)
    bias = _bias_tiles(rel_bias) if depth > 1 else None
    gains = [a.reshape(depth, 1, D_MODEL) for a in (norm_ffn1, norm_mix, norm_ffn2)]
    for i in range(depth):
        h = _ffn([h], i, gains[0], ffn1_w_in, ffn1_w_out)
        j = i // 2
        if i % 2 == 0:
            mixed = [_conv_mixer(h, seq, i, j, gains[1], conv_w_in, conv_w, conv_w_out)]
        else:
            parts = [_attn_group(_qkv_proj(h, i, j, g, gains[1], attn_w_qkv, attn_q_gain[j], attn_k_gain[j]),
                                 bias, g, batch, seq) for g in range(N_GROUPS)]
            mixed = [h, _merge_proj([o for o, _ in parts], [l for _, l in parts], j, attn_w_out)]
        h = _ffn(mixed, i, gains[2], ffn2_w_in, ffn2_w_out)
    return h.reshape(batch, seq, D_MODEL)
```

```python
import functools
import math

import numpy as np
import jax
import jax.numpy as jnp
from jax import lax
from jax.experimental import pallas as pl
from jax.experimental.pallas import tpu as pltpu

D_MODEL = 1024
D_FF = 2816
CONV_WIDTH = 3
HEAD_DIM = 64
N_HEADS = D_MODEL // HEAD_DIM
N_PAIRS = N_HEADS // 2
DILATION_GROUPS = ((128, 1), (512, 4), (2048, 16))
N_GROUPS = len(DILATION_GROUPS)
BLOCK = 128
N_BUCKETS = 32
MAX_EXACT = N_BUCKETS // 2
MAX_DISTANCE = 2048
RMS_EPS = 1e-6
NEG_INF = -1e30
LOG2E = math.log2(math.e)
LN2 = math.log(2.0)

LANES = 128
MXU_DIM = 256
UNIT = MXU_DIM
VMEM_LIMIT = 56 * 1024 * 1024

F32 = jnp.float32
BF16 = jnp.bfloat16


def _rmsnorm_bf16(x, g):
    ms = jnp.mean(x * x, axis=-1, keepdims=True)
    return (x * lax.rsqrt(ms + RMS_EPS) * g).astype(BF16)


def _dot(a, b):
    return jnp.dot(a, b, preferred_element_type=F32)


def _resident(shape):
    return pl.BlockSpec(shape, lambda *_: (0,) * len(shape), pipeline_mode=pl.Buffered(1))


def _layer(shape, layer):
    return pl.BlockSpec((None,) + shape, lambda *_: (layer,) + (0,) * len(shape), pipeline_mode=pl.Buffered(1))


def _load_weights(src, dst_ref, stage_ref, sem):
    rows = stage_ref.shape[1]
    n_chunks = dst_ref.shape[0] // rows

    def copy(c):
        return pltpu.make_async_copy(src.at[pl.ds(c * rows, rows)], stage_ref.at[c % 2], sem.at[c % 2])

    copy(0).start()
    for c in range(n_chunks):
        if c + 1 < n_chunks:
            copy(c + 1).start()
        copy(c).wait()
        dst_ref[pl.ds(c * rows, rows), :] = stage_ref[c % 2].astype(BF16)


def _weight_scratch(rows, cols, stage_rows):
    assert rows % stage_rows == 0
    return [pltpu.VMEM((rows, cols), BF16), pltpu.VMEM((2, stage_rows, cols), F32)]


def _ffn_cast_specs(n_steps, layer):
    assert D_MODEL % n_steps == 0 and D_FF % (n_steps // 2) == 0
    in_rows, out_rows = D_MODEL // n_steps, D_FF // (n_steps // 2)
    half = lambda i: jnp.minimum(i, n_steps // 2 - 1)
    in_specs = [pl.BlockSpec((None, in_rows, 2 * D_FF), lambda i: (layer, i, 0)),
                pl.BlockSpec((None, out_rows, D_MODEL), lambda i: (layer, half(i), 0))]
    out_specs = [pl.BlockSpec((in_rows, 2 * D_FF), lambda i: (i, 0)),
                 pl.BlockSpec((out_rows, D_MODEL), lambda i: (half(i), 0))]
    out_shapes = [jax.ShapeDtypeStruct((D_MODEL, 2 * D_FF), BF16), jax.ShapeDtypeStruct((D_FF, D_MODEL), BF16)]
    return in_specs, out_specs, out_shapes


def _cast_blocks(src_refs, dst_refs):
    for src, dst in zip(src_refs, dst_refs):
        dst[...] = src[...].astype(BF16)


HBM = pl.BlockSpec(memory_space=pl.ANY)
DMA_SEMS = pltpu.SemaphoreType.DMA((2,))


def _params(*sem):
    return pltpu.CompilerParams(dimension_semantics=sem, vmem_limit_bytes=VMEM_LIMIT)


assert DILATION_GROUPS[0][1] == 1

ROW_STRIDE = 4


def _stride_passes(r):
    assert r in (1, ROW_STRIDE, ROW_STRIDE ** 2)
    return {1: 1, ROW_STRIDE: 1, ROW_STRIDE ** 2: 2}[r]


FFN_TM = 1024
FFN_SUB = 512
FFN_CHUNKS = ((0, 5 * MXU_DIM), (5 * MXU_DIM, 6 * MXU_DIM))


def _ffn_kernel(layer, n_inputs, preconverted, cast_next, *refs):
    refs = list(refs)
    x_refs, (g_ref, w_in, w_out) = refs[:n_inputs], refs[n_inputs:n_inputs + 3]
    refs = refs[n_inputs + 3:]
    n_cast = 2 if cast_next else 0
    next_f32, o_ref, next_bf16, scratch = refs[:n_cast], refs[n_cast], refs[n_cast + 1:2 * n_cast + 1], refs[2 * n_cast + 1:]
    if preconverted:
        win_ref, wout_ref = w_in, w_out
    else:
        win_ref, win_stage, wout_ref, wout_stage, sem = scratch

        @pl.when(pl.program_id(0) == 0)
        def _():
            _load_weights(w_in.at[layer], win_ref, win_stage, sem)
            _load_weights(w_out.at[layer], wout_ref, wout_stage, sem)

    _cast_blocks(next_f32, next_bf16)

    for t in range(FFN_TM // FFN_SUB):
        rows = pl.ds(t * FFN_SUB, FFN_SUB)
        x = x_refs[0][rows, :]
        for extra_ref in x_refs[1:]:
            x = x + extra_ref[rows, :]
        xn = _rmsnorm_bf16(x, g_ref[...])
        acc = None
        for start, size in FFN_CHUNKS:
            gate = _dot(xn, win_ref[:, pl.ds(start, size)])
            up = _dot(xn, win_ref[:, pl.ds(D_FF + start, size)])
            a = (gate * jax.nn.sigmoid(gate) * up).astype(BF16)
            part = _dot(a, wout_ref[pl.ds(start, size), :])
            acc = part if acc is None else acc + part
        o_ref[rows, :] = x + 0.5 * acc


def _ffn(hs, layer, g, weights, cast_next=None):
    n_tok = hs[0].shape[0]
    n_steps = n_tok // FFN_TM
    preconverted = weights[0].dtype == BF16
    tile = pl.BlockSpec((FFN_TM, D_MODEL), lambda i: (i, 0))
    if preconverted:
        w_specs, scratch = [_resident((D_MODEL, 2 * D_FF)), _resident((D_FF, D_MODEL))], []
    else:
        w_specs = [HBM, HBM]
        scratch = (_weight_scratch(D_MODEL, 2 * D_FF, LANES // len(hs)) + _weight_scratch(D_FF, D_MODEL, D_FF // 8)
                   + [DMA_SEMS])
    cast_in, cast_out, cast_shapes = _ffn_cast_specs(n_steps, cast_next[2]) if cast_next else ([], [], [])
    outs = pl.pallas_call(
        functools.partial(_ffn_kernel, layer, len(hs), preconverted, cast_next is not None),
        grid=(n_steps,),
        in_specs=[tile] * len(hs) + [_layer((1, D_MODEL), layer)] + w_specs + cast_in,
        out_specs=[tile] + cast_out,
        out_shape=[jax.ShapeDtypeStruct((n_tok, D_MODEL), F32)] + cast_shapes,
        scratch_shapes=scratch,
        compiler_params=_params("arbitrary"),
        name="ffn",
    )(*hs, g, *weights, *(cast_next[:2] if cast_next else ()))
    return outs[0], tuple(outs[1:])


CONV_TM = 1024
CONV_SUB = 512
CARRY = 8


def _conv_kernel(tiles_per_seq, mixer, x_ref, g_ref, win_hbm, cw_ref, wout_hbm, next_win, next_wout,
                 o_ref, next_win_bf16, next_wout_bf16, vp_ref, win_ref, win_stage, wout_ref, wout_stage, sem):
    _cast_blocks((next_win, next_wout), (next_win_bf16, next_wout_bf16))

    @pl.when(pl.program_id(0) == 0)
    def _():
        _load_weights(win_hbm.at[mixer], win_ref, win_stage, sem)
        _load_weights(wout_hbm.at[mixer], wout_ref, wout_stage, sem)

    @pl.when(pl.program_id(0) % tiles_per_seq == 0)
    def _():
        vp_ref[pl.ds(0, CARRY), :] = jnp.zeros((CARRY, D_MODEL), F32)

    for t in range(CONV_TM // CONV_SUB):
        rows = pl.ds(t * CONV_SUB, CONV_SUB)
        x = x_ref[rows, :]
        xn = _rmsnorm_bf16(x, g_ref[...])
        b_gate = _dot(xn, win_ref[:, pl.ds(0, D_MODEL)])
        c_gate = _dot(xn, win_ref[:, pl.ds(D_MODEL, D_MODEL)])
        u = _dot(xn, win_ref[:, pl.ds(2 * D_MODEL, D_MODEL)])
        v = c_gate * u
        vp_ref[pl.ds(CARRY, CONV_SUB), :] = v
        conv = cw_ref[pl.ds(0, 1), :] * v
        for lag in range(1, CONV_WIDTH):
            conv = conv + cw_ref[pl.ds(lag, 1), :] * vp_ref[pl.ds(CARRY - lag, CONV_SUB), :]
        z = (b_gate * conv).astype(BF16)
        o_ref[rows, :] = x + _dot(z, wout_ref[...])
        vp_ref[pl.ds(0, CARRY), :] = vp_ref[pl.ds(CONV_SUB, CARRY), :]


def _conv_mixer(h, seq, layer, mixer, g, w_in, w_conv, w_out, cast_next):
    n_tok = h.shape[0]
    cast_in, cast_out, cast_shapes = _ffn_cast_specs(n_tok // CONV_TM, cast_next[2])
    outs = pl.pallas_call(
        functools.partial(_conv_kernel, seq // CONV_TM, mixer),
        grid=(n_tok // CONV_TM,),
        in_specs=[
            pl.BlockSpec((CONV_TM, D_MODEL), lambda i: (i, 0)),
            _layer((1, D_MODEL), layer),
            HBM,
            _layer((CONV_WIDTH, D_MODEL), mixer),
            HBM,
        ] + cast_in,
        out_specs=[pl.BlockSpec((CONV_TM, D_MODEL), lambda i: (i, 0))] + cast_out,
        out_shape=[jax.ShapeDtypeStruct((n_tok, D_MODEL), F32)] + cast_shapes,
        scratch_shapes=[pltpu.VMEM((CARRY + CONV_SUB, D_MODEL), F32)]
        + _weight_scratch(D_MODEL, 3 * D_MODEL, LANES) + _weight_scratch(D_MODEL, D_MODEL, MXU_DIM) + [DMA_SEMS],
        compiler_params=_params("arbitrary"),
        name="conv_mixer",
    )(h, g, w_in, w_conv, w_out, *cast_next[:2])
    return outs[0], tuple(outs[1:])


QKV_TM = 1024
QKV_SUB = 512
QKV_TN = 2 * MXU_DIM
N_QKV_TILES = 3 * N_GROUPS


def _qkv_kernel(mixer, group, x_ref, g_ref, w_hbm, gain_ref, bd_ref, perm_ref, o_ref, w_ref, w_stage, sem, xs_ref):
    @pl.when(pl.program_id(0) == 0)
    def _():
        _load_weights(w_hbm.at[mixer, :, pl.ds(group * 3 * D_MODEL, 3 * D_MODEL)], w_ref, w_stage, sem)

    r = DILATION_GROUPS[group][1]
    rows_per_strand = UNIT // r
    for t in range(QKV_TM // QKV_SUB):
        rows = pl.ds(t * QKV_SUB, QKV_SUB)
        x = x_ref[rows, :]
        xn = x * lax.rsqrt(jnp.mean(x * x, axis=-1, keepdims=True) + RMS_EPS) * g_ref[...]
        if r == ROW_STRIDE:
            slabs = range(D_MODEL // LANES)
            for p in slabs:
                xs_ref[t, p] = xn[:, p * LANES:(p + 1) * LANES]
            xn = jnp.concatenate(
                [jnp.concatenate([xs_ref[t, p, pl.ds(u * UNIT + s, rows_per_strand, stride=r), :] for p in slabs],
                                 axis=1)
                 for u in range(QKV_SUB // UNIT) for s in range(r)], axis=0)
        xn = xn.astype(BF16)
        if r > ROW_STRIDE:
            xn = jnp.concatenate(
                [_dot(perm_ref[...], xn[u * UNIT:(u + 1) * UNIT]).astype(BF16) for u in range(QKV_SUB // UNIT)],
                axis=0)
        for c in range(3 * D_MODEL // QKV_TN):
            cols = pl.ds(c * QKV_TN, QKV_TN)
            y = _dot(xn, w_ref[:, cols])
            if c * QKV_TN >= 2 * D_MODEL:
                res = y.astype(BF16)
            else:
                y2 = (y * y).astype(BF16)
                ms = jnp.concatenate(
                    [_dot(y2[:, s * MXU_DIM:(s + 1) * MXU_DIM], bd_ref[...]) for s in range(QKV_TN // MXU_DIM)],
                    axis=1)
                res = (y * lax.rsqrt(ms + RMS_EPS) * gain_ref[:, cols]).astype(BF16)
            for s in range(QKV_TN // LANES):
                o_ref[c * (QKV_TN // LANES) + s, rows, :] = res[:, s * LANES:(s + 1) * LANES]


def _qkv_proj(h, layer, mixer, group, g, w_qkv, q_gain, k_gain):
    n_tok = h.shape[0]
    gains = jnp.concatenate(
        [jnp.tile(q_gain[group], N_HEADS) * (HEAD_DIM ** -0.5 * LOG2E), jnp.tile(k_gain[group], N_HEADS),
         jnp.ones((D_MODEL,), F32)]).reshape(1, 3 * D_MODEL)
    seg = np.arange(MXU_DIM) // HEAD_DIM
    block_diag = jnp.asarray((seg[:, None] == seg[None, :]) / HEAD_DIM, BF16)
    r = DILATION_GROUPS[group][1]
    tok = np.arange(UNIT)
    perm = np.zeros((UNIT, UNIT), np.float32)
    perm[(tok % r) * (UNIT // r) + tok // r, tok] = 1.0
    return pl.pallas_call(
        functools.partial(_qkv_kernel, mixer, group),
        grid=(n_tok // QKV_TM,),
        in_specs=[
            pl.BlockSpec((QKV_TM, D_MODEL), lambda i: (i, 0)),
            _layer((1, D_MODEL), layer),
            HBM,
            _resident((1, 3 * D_MODEL)),
            _resident((MXU_DIM, MXU_DIM)),
            _resident((UNIT, UNIT)),
        ],
        out_specs=pl.BlockSpec((3 * N_PAIRS, QKV_TM, LANES), lambda i: (0, i, 0)),
        out_shape=jax.ShapeDtypeStruct((3 * N_PAIRS, n_tok, LANES), BF16),
        scratch_shapes=_weight_scratch(D_MODEL, 3 * D_MODEL, LANES) + [
            DMA_SEMS, pltpu.VMEM((QKV_TM // QKV_SUB, D_MODEL // LANES, QKV_SUB, LANES), F32)],
        compiler_params=_params("arbitrary"),
        name=f"qkv_proj_g{group}",
    )(h, g, w_qkv, gains, block_diag, jnp.asarray(perm, BF16))


def _t5_bucket_np(dist):
    nf = np.maximum(dist, 1).astype(np.float32)
    large = MAX_EXACT + (np.log(nf / np.float32(MAX_EXACT)) / np.float32(math.log(MAX_DISTANCE / MAX_EXACT))
                         * np.float32(N_BUCKETS - MAX_EXACT)).astype(np.int32)
    large = np.minimum(large, N_BUCKETS - 1)
    return np.where(dist < MAX_EXACT, dist, large)


def _bucket_maps():
    i = (np.arange(2 * BLOCK) % BLOCK)[:, None]
    j = np.arange(2 * BLOCK)[None, :]
    step = BLOCK + i - j
    maps = []
    for window, dilation in DILATION_GROUPS:
        n_steps = window // dilation
        in_band = (step >= 0) & (step <= n_steps)
        bucket = _t5_bucket_np(np.clip(step, 0, n_steps) * dilation)
        maps.append(np.where(in_band, bucket, -1).astype(np.int32))
    return np.stack(maps)


def _bias_kernel(buckets, rel_ref, bm_ref, o_ref):
    p = pl.program_id(0)
    shape = bm_ref.shape[1:]
    top = lax.broadcasted_iota(jnp.int32, shape, 0) < BLOCK
    col = lax.broadcasted_iota(jnp.int32, shape, 1)
    for g in range(N_GROUPS):
        bm = bm_ref[g]
        acc = jnp.zeros(shape, F32)
        for b in buckets[g]:
            val = jnp.where(top, rel_ref[b, g * N_HEADS + 2 * p], rel_ref[b, g * N_HEADS + 2 * p + 1])
            acc = jnp.where(bm == b, val, acc)
        rest = jnp.where(bm < 0, NEG_INF, acc * LOG2E)
        o_ref[g, 1] = rest
        o_ref[g, 0] = jnp.where(col < BLOCK, NEG_INF, rest)


def _bias_tiles(rel_bias):
    tile = (2 * BLOCK, 2 * BLOCK)
    maps = _bucket_maps()
    buckets = tuple(tuple(int(b) for b in np.unique(m[m >= 0])) for m in maps)
    return pl.pallas_call(
        functools.partial(_bias_kernel, buckets),
        grid=(N_PAIRS,),
        in_specs=[pl.BlockSpec(memory_space=pltpu.SMEM), _resident((N_GROUPS,) + tile)],
        out_specs=pl.BlockSpec((N_GROUPS, 2, None) + tile, lambda p: (0, 0, p, 0, 0)),
        out_shape=jax.ShapeDtypeStruct((N_GROUPS, 2, N_PAIRS) + tile, F32),
        compiler_params=_params("arbitrary"),
        name="bias_tiles",
    )(rel_bias, jnp.asarray(maps))


ATTN_BLOCKS = 16


def _attn_kernel(nb, q_ref, kp_ref, kc_ref, vp_ref, vc_ref, bias_ref, o_ref, lse_ref):
    upb = q_ref.shape[1] // nb
    n_strands, rows_per_unit = q_ref.shape[2], q_ref.shape[3]
    not_first = jnp.minimum(pl.program_id(2), 1)
    lane = lax.broadcasted_iota(jnp.int32, (BLOCK, LANES), 1)
    lo = lane < HEAD_DIM
    ones = jnp.ones((2 * BLOCK, LANES), BF16)

    def keys_of(prev_ref, cur_ref, blk, p, j):
        if blk == 0:
            return jnp.concatenate([prev_ref[p, :, j].reshape(BLOCK, LANES),
                                    cur_ref[p, pl.ds(0, upb), j].reshape(BLOCK, LANES)], axis=0)
        return cur_ref[p, pl.ds((blk - 1) * upb, 2 * upb), j].reshape(2 * BLOCK, LANES)

    for j in range(n_strands):
        for blk in range(nb):
            units = pl.ds(blk * upb, upb)
            m_all = jnp.zeros((BLOCK, LANES), F32)
            s_all = jnp.ones((BLOCK, LANES), F32)
            for p in range(N_PAIRS):
                q = q_ref[p, units, j].reshape(BLOCK, LANES)
                zero = jnp.zeros_like(q)
                q2 = jnp.concatenate([jnp.where(lo, q, zero), jnp.where(lo, zero, q)], axis=0)
                k = keys_of(kp_ref, kc_ref, blk, p, j)
                logits = lax.dot_general(q2, k, (((1,), (1,)), ((), ())), preferred_element_type=F32)
                logits = logits + (bias_ref[not_first, p] if blk == 0 else bias_ref[1, p])
                m = jnp.max(logits, axis=-1, keepdims=True)
                prob = jnp.exp2(logits - m).astype(BF16)
                acc = _dot(prob, jnp.concatenate([keys_of(vp_ref, vc_ref, blk, p, j), ones], axis=1))
                o = jnp.where(lo, acc[:BLOCK, :LANES], acc[BLOCK:, :LANES])
                s = jnp.where(lo, acc[:BLOCK, LANES:], acc[BLOCK:, LANES:])
                mm = jnp.where(lo, m[:BLOCK], m[BLOCK:])
                o_ref[p, units, j] = (o / s).astype(BF16).reshape(upb, rows_per_unit, LANES)
                keep = (lane == 2 * p) | (lane == HEAD_DIM + 2 * p + 1)
                m_all = jnp.where(keep, mm, m_all)
                s_all = jnp.where(keep, s, s_all)
            lse_ref[units, j] = (m_all * LN2 + jnp.log(s_all)).reshape(upb, rows_per_unit, LANES)


def _attn_group(qkv, bias, g, batch, seq):
    r = DILATION_GROUPS[g][1]
    rows_per_unit = min(UNIT // r, BLOCK)
    n_units = seq // (r * rows_per_unit)
    upb = BLOCK // rows_per_unit
    nb = min(ATTN_BLOCKS, seq // (r * BLOCK))
    n_strands = ATTN_BLOCKS // nb
    view = qkv.reshape(3 * N_PAIRS, batch, n_units, r, rows_per_unit, LANES)

    def cur(which):
        return lambda b, s, c: (which, b, c, s, 0, 0)

    def prev(which):
        return lambda b, s, c: (which, b, jnp.maximum(c * nb - 1, 0), s, 0, 0)

    cur_blk = (N_PAIRS, None, nb * upb, n_strands, rows_per_unit, LANES)
    prev_blk = (N_PAIRS, None, upb, n_strands, rows_per_unit, LANES)
    o, lse = pl.pallas_call(
        functools.partial(_attn_kernel, nb),
        grid=(batch, r // n_strands, seq // (r * nb * BLOCK)),
        in_specs=[
            pl.BlockSpec(cur_blk, cur(0)),
            pl.BlockSpec(prev_blk, prev(1)),
            pl.BlockSpec(cur_blk, cur(1)),
            pl.BlockSpec(prev_blk, prev(2)),
            pl.BlockSpec(cur_blk, cur(2)),
            pl.BlockSpec((None, 2, N_PAIRS, 2 * BLOCK, 2 * BLOCK), lambda b, s, c: (g, 0, 0, 0, 0)),
        ],
        out_specs=[
            pl.BlockSpec(cur_blk, lambda b, s, c: (0, b, c, s, 0, 0)),
            pl.BlockSpec((None, nb * upb, n_strands, rows_per_unit, LANES), lambda b, s, c: (b, c, s, 0, 0)),
        ],
        out_shape=[
            jax.ShapeDtypeStruct((N_PAIRS, batch, n_units, r, rows_per_unit, LANES), BF16),
            jax.ShapeDtypeStruct((batch, n_units, r, rows_per_unit, LANES), F32),
        ],
        compiler_params=_params("arbitrary", "arbitrary", "arbitrary"),
        name=f"attn_r{r}",
    )(view, view, view, view, view, bias)
    return o.reshape(N_PAIRS, batch * seq, LANES), lse.reshape(batch * seq, LANES)


MERGE_TM = 1024
MERGE_SUB = 512


def _merge_kernel(mixer, o0_ref, o1_ref, o2_ref, l0_ref, l1_ref, l2_ref, e_ref, w_hbm, out_ref,
                  w_ref, w_stage, sem, tok_ref, ltok_ref, tmp_ref, ltmp_ref, merged_ref):
    @pl.when(pl.program_id(0) == 0)
    def _():
        _load_weights(w_hbm.at[mixer], w_ref, w_stage, sem)

    o_refs = (o0_ref, o1_ref, o2_ref)
    l_refs = (l0_ref, l1_ref, l2_ref)
    for t in range(MERGE_TM // MERGE_SUB):
        first = t * MERGE_SUB
        rows = pl.ds(first, MERGE_SUB)
        lses = [l0_ref[rows, :]]
        for g in range(1, N_GROUPS):
            r = DILATION_GROUPS[g][1]
            rows_per_strand = UNIT // r
            for u in range(MERGE_SUB // UNIT):
                unit = first + u * UNIT
                if r == ROW_STRIDE:
                    for s in range(r):
                        src_rows = pl.ds(unit + s * rows_per_strand, rows_per_strand)
                        dst_rows = pl.ds(unit + s, rows_per_strand, stride=r)
                        ltok_ref[g - 1, dst_rows, :] = l_refs[g][src_rows, :]
                        for p in range(N_PAIRS):
                            tok_ref[g - 1, p, dst_rows, :] = o_refs[g][p, src_rows, :].astype(F32)
                else:
                    assert _stride_passes(r) == 2
                    quarter = UNIT // ROW_STRIDE
                    for a in range(ROW_STRIDE):
                        for b in range(ROW_STRIDE):
                            src_rows = pl.ds(unit + (ROW_STRIDE * a + b) * rows_per_strand, rows_per_strand)
                            dst_rows = pl.ds(unit + b * quarter + a, rows_per_strand, stride=ROW_STRIDE)
                            ltmp_ref[dst_rows, :] = l_refs[g][src_rows, :]
                            for p in range(N_PAIRS):
                                tmp_ref[p, dst_rows, :] = o_refs[g][p, src_rows, :].astype(F32)
                    for b in range(ROW_STRIDE):
                        src_rows = pl.ds(unit + b * quarter, quarter)
                        dst_rows = pl.ds(unit + b, quarter, stride=ROW_STRIDE)
                        ltok_ref[g - 1, dst_rows, :] = ltmp_ref[src_rows, :]
                        for p in range(N_PAIRS):
                            tok_ref[g - 1, p, dst_rows, :] = tmp_ref[p, src_rows, :]
            lses.append(ltok_ref[g - 1, rows, :])
        m = jnp.maximum(jnp.maximum(lses[0], lses[1]), lses[2])
        es = [jnp.exp(l - m) for l in lses]
        den = es[0] + es[1] + es[2]
        w_cat = []
        for g in range(1, N_GROUPS):
            w = es[g] / den
            w_hi = w.astype(BF16)
            w_cat.append(jnp.concatenate([w_hi, (w - w_hi.astype(F32)).astype(BF16)], axis=1))
        for c in range(D_MODEL // MXU_DIM):
            pairs = range(c * MXU_DIM // LANES, (c + 1) * MXU_DIM // LANES)
            base = jnp.concatenate([o0_ref[p, rows, :] for p in pairs], axis=1).astype(F32)
            acc = base
            for g in range(1, N_GROUPS):
                w_full = _dot(w_cat[g - 1], e_ref[:, pl.ds(c * MXU_DIM, MXU_DIM)])
                acc = acc + w_full * (jnp.concatenate([tok_ref[g - 1, p, rows, :] for p in pairs], axis=1) - base)
            merged_ref[rows, pl.ds(c * MXU_DIM, MXU_DIM)] = acc.astype(BF16)
        out_ref[rows, :] = _dot(merged_ref[rows, :], w_ref[...])


def _merge_proj(outs, lses, mixer, w_out):
    n_tok = lses[0].shape[0]
    expand = np.zeros((2 * LANES, D_MODEL), np.float32)
    for head in range(N_HEADS):
        src = head if head % 2 == 0 else HEAD_DIM + head
        expand[src, head * HEAD_DIM:(head + 1) * HEAD_DIM] = 1.0
    expand[LANES:] = expand[:LANES]
    tok = lambda width: pl.BlockSpec((MERGE_TM, width), lambda i: (i, 0))
    slab = pl.BlockSpec((N_PAIRS, MERGE_TM, LANES), lambda i: (0, i, 0))
    return pl.pallas_call(
        functools.partial(_merge_kernel, mixer),
        grid=(n_tok // MERGE_TM,),
        in_specs=[slab] * N_GROUPS + [tok(LANES)] * N_GROUPS
        + [_resident((2 * LANES, D_MODEL)), HBM],
        out_specs=tok(D_MODEL),
        out_shape=jax.ShapeDtypeStruct((n_tok, D_MODEL), F32),
        scratch_shapes=_weight_scratch(D_MODEL, D_MODEL, MXU_DIM) + [
            DMA_SEMS,
            pltpu.VMEM((N_GROUPS - 1, N_PAIRS, MERGE_TM, LANES), F32),
            pltpu.VMEM((N_GROUPS - 1, MERGE_TM, LANES), F32),
            pltpu.VMEM((N_PAIRS, MERGE_TM, LANES), F32),
            pltpu.VMEM((MERGE_TM, LANES), F32),
            pltpu.VMEM((MERGE_TM, D_MODEL), BF16),
        ],
        compiler_params=_params("arbitrary"),
        name="merge_proj",
    )(*outs, *lses, jnp.asarray(expand, BF16), w_out)


def kernel(x, norm_ffn1, ffn1_w_in, ffn1_w_out, norm_mix, conv_w_in, conv_w, conv_w_out, attn_w_qkv,
           attn_q_gain, attn_k_gain, attn_w_out, rel_bias, norm_ffn2, ffn2_w_in, ffn2_w_out):
    batch, seq, d_model = x.shape
    assert d_model == D_MODEL and seq % max(QKV_TM, CONV_TM, DILATION_GROUPS[-1][1] * BLOCK) == 0
    depth = norm_ffn1.shape[0]
    h = x.reshape(batch * seq, D_MODEL)
    bias = _bias_tiles(rel_bias) if depth > 1 else None
    gains = [a.reshape(depth, 1, D_MODEL) for a in (norm_ffn1, norm_mix, norm_ffn2)]
    ffn1_weights = (ffn1_w_in, ffn1_w_out)
    for i in range(depth):
        ffn2_next = (ffn2_w_in, ffn2_w_out, i)
        ffn1_next = (ffn1_w_in, ffn1_w_out, i + 1) if i + 1 < depth else None
        j = i // 2
        if i % 2 == 0:
            h, _ = _ffn([h], i, gains[0], ffn1_weights)
            h, ffn2_weights = _conv_mixer(h, seq, i, j, gains[1], conv_w_in, conv_w, conv_w_out, ffn2_next)
            mixed = [h]
        else:
            h, ffn2_weights = _ffn([h], i, gains[0], ffn1_weights, ffn2_next)
            parts = [_attn_group(_qkv_proj(h, i, j, g, gains[1], attn_w_qkv, attn_q_gain[j], attn_k_gain[j]),
                                 bias, g, batch, seq) for g in range(N_GROUPS)]
            mixed = [h, _merge_proj([o for o, _ in parts], [l for _, l in parts], j, attn_w_out)]
        h, ffn1_weights = _ffn(mixed, i, gains[2], ffn2_weights, ffn1_next)
    return h.reshape(batch, seq, D_MODEL)
```

```python
import functools
import math

import numpy as np
import jax
import jax.numpy as jnp
from jax import lax
from jax.experimental import pallas as pl
from jax.experimental.pallas import tpu as pltpu

D_MODEL = 1024
D_FF = 2816
CONV_WIDTH = 3
HEAD_DIM = 64
N_HEADS = D_MODEL // HEAD_DIM
N_PAIRS = N_HEADS // 2
DILATION_GROUPS = ((128, 1), (512, 4), (2048, 16))
N_GROUPS = len(DILATION_GROUPS)
BLOCK = 128
N_BUCKETS = 32
MAX_EXACT = N_BUCKETS // 2
MAX_DISTANCE = 2048
RMS_EPS = 1e-6
NEG_INF = -1e30
LOG2E = math.log2(math.e)
LN2 = math.log(2.0)

LANES = 128
BF16_TILE_ROWS = 16
MXU_DIM = 256
UNIT = MXU_DIM
VMEM_LIMIT = 56 * 1024 * 1024

F32 = jnp.float32
BF16 = jnp.bfloat16


def _rmsnorm_bf16(x, g):
    ms = jnp.mean(x * x, axis=-1, keepdims=True)
    return (x * lax.rsqrt(ms + RMS_EPS) * g).astype(BF16)


def _dot(a, b):
    return jnp.dot(a, b, preferred_element_type=F32)


def _resident(shape):
    return pl.BlockSpec(shape, lambda *_: (0,) * len(shape), pipeline_mode=pl.Buffered(1))


def _layer(shape, layer):
    return pl.BlockSpec((None,) + shape, lambda *_: (layer,) + (0,) * len(shape), pipeline_mode=pl.Buffered(1))


def _load_weights(src, dst_ref, stage_ref, sem):
    rows = stage_ref.shape[1]
    n_chunks = dst_ref.shape[0] // rows

    def copy(c):
        return pltpu.make_async_copy(src.at[pl.ds(c * rows, rows)], stage_ref.at[c % 2], sem.at[c % 2])

    copy(0).start()
    for c in range(n_chunks):
        if c + 1 < n_chunks:
            copy(c + 1).start()
        copy(c).wait()
        dst_ref[pl.ds(c * rows, rows), :] = stage_ref[c % 2].astype(BF16)


def _weight_scratch(rows, cols, stage_rows):
    assert rows % stage_rows == 0
    return [pltpu.VMEM((rows, cols), BF16), pltpu.VMEM((2, stage_rows, cols), F32)]


def _cast_specs(n_steps, items):
    in_specs, out_specs, out_shapes = [], [], []
    for array, lead in items:
        rows, cols = array.shape[1:]
        n_blocks = n_steps if rows % (BF16_TILE_ROWS * n_steps) == 0 else n_steps // 2
        assert rows % (BF16_TILE_ROWS * n_blocks) == 0
        in_specs.append(pl.BlockSpec((None, rows // n_blocks, cols),
                                     lambda i, lead=lead, last=n_blocks - 1: (lead, jnp.minimum(i, last), 0)))
        out_specs.append(pl.BlockSpec((rows // n_blocks, cols), lambda i, last=n_blocks - 1: (jnp.minimum(i, last), 0)))
        out_shapes.append(jax.ShapeDtypeStruct((rows, cols), BF16))
    return in_specs, out_specs, out_shapes


def _cast_blocks(src_refs, dst_refs):
    for src, dst in zip(src_refs, dst_refs):
        dst[...] = src[...].astype(BF16)


HBM = pl.BlockSpec(memory_space=pl.ANY)
DMA_SEMS = pltpu.SemaphoreType.DMA((2,))


def _params(*sem):
    return pltpu.CompilerParams(dimension_semantics=sem, vmem_limit_bytes=VMEM_LIMIT)


assert DILATION_GROUPS[0][1] == 1

ROW_STRIDE = 4


def _stride_passes(r):
    assert r in (1, ROW_STRIDE, ROW_STRIDE ** 2)
    return {1: 1, ROW_STRIDE: 1, ROW_STRIDE ** 2: 2}[r]


FFN_TM = 1024
FFN_SUB = 512
FFN_CHUNKS = ((0, 5 * MXU_DIM), (5 * MXU_DIM, 6 * MXU_DIM))


def _ffn_kernel(layer, n_inputs, preconverted, n_cast, *refs):
    refs = list(refs)
    x_refs, (g_ref, w_in, w_out) = refs[:n_inputs], refs[n_inputs:n_inputs + 3]
    refs = refs[n_inputs + 3:]
    next_f32, o_ref, next_bf16, scratch = refs[:n_cast], refs[n_cast], refs[n_cast + 1:2 * n_cast + 1], refs[2 * n_cast + 1:]
    if preconverted:
        win_ref, wout_ref = w_in, w_out
    else:
        win_ref, win_stage, wout_ref, wout_stage, sem = scratch

        @pl.when(pl.program_id(0) == 0)
        def _():
            _load_weights(w_in.at[layer], win_ref, win_stage, sem)
            _load_weights(w_out.at[layer], wout_ref, wout_stage, sem)

    _cast_blocks(next_f32, next_bf16)

    for t in range(FFN_TM // FFN_SUB):
        rows = pl.ds(t * FFN_SUB, FFN_SUB)
        x = x_refs[0][rows, :]
        for extra_ref in x_refs[1:]:
            x = x + extra_ref[rows, :]
        xn = _rmsnorm_bf16(x, g_ref[...])
        acc = None
        for start, size in FFN_CHUNKS:
            gate = _dot(xn, win_ref[:, pl.ds(start, size)])
            up = _dot(xn, win_ref[:, pl.ds(D_FF + start, size)])
            a = (gate * jax.nn.sigmoid(gate) * up).astype(BF16)
            part = _dot(a, wout_ref[pl.ds(start, size), :])
            acc = part if acc is None else acc + part
        o_ref[rows, :] = x + 0.5 * acc


def _ffn(hs, layer, g, weights, cast_items=()):
    n_tok = hs[0].shape[0]
    n_steps = n_tok // FFN_TM
    preconverted = weights[0].dtype == BF16
    tile = pl.BlockSpec((FFN_TM, D_MODEL), lambda i: (i, 0))
    if preconverted:
        w_specs, scratch = [_resident((D_MODEL, 2 * D_FF)), _resident((D_FF, D_MODEL))], []
    else:
        w_specs = [HBM, HBM]
        scratch = (_weight_scratch(D_MODEL, 2 * D_FF, LANES // len(hs)) + _weight_scratch(D_FF, D_MODEL, D_FF // 8)
                   + [DMA_SEMS])
    cast_in, cast_out, cast_shapes = _cast_specs(n_steps, cast_items)
    outs = pl.pallas_call(
        functools.partial(_ffn_kernel, layer, len(hs), preconverted, len(cast_items)),
        grid=(n_steps,),
        in_specs=[tile] * len(hs) + [_layer((1, D_MODEL), layer)] + w_specs + cast_in,
        out_specs=[tile] + cast_out,
        out_shape=[jax.ShapeDtypeStruct((n_tok, D_MODEL), F32)] + cast_shapes,
        scratch_shapes=scratch,
        compiler_params=_params("arbitrary"),
        name="ffn",
    )(*hs, g, *weights, *(array for array, _ in cast_items))
    return outs[0], tuple(outs[1:])


CONV_TM = 1024
CONV_SUB = 512
CARRY = 8


def _conv_kernel(tiles_per_seq, n_cast, x_ref, g_ref, win_ref, cw_ref, wout_ref, *refs):
    next_f32, o_ref, next_bf16, vp_ref = refs[:n_cast], refs[n_cast], refs[n_cast + 1:2 * n_cast + 1], refs[-1]
    _cast_blocks(next_f32, next_bf16)

    @pl.when(pl.program_id(0) % tiles_per_seq == 0)
    def _():
        vp_ref[pl.ds(0, CARRY), :] = jnp.zeros((CARRY, D_MODEL), F32)

    for t in range(CONV_TM // CONV_SUB):
        rows = pl.ds(t * CONV_SUB, CONV_SUB)
        x = x_ref[rows, :]
        xn = _rmsnorm_bf16(x, g_ref[...])
        b_gate = _dot(xn, win_ref[:, pl.ds(0, D_MODEL)])
        c_gate = _dot(xn, win_ref[:, pl.ds(D_MODEL, D_MODEL)])
        u = _dot(xn, win_ref[:, pl.ds(2 * D_MODEL, D_MODEL)])
        v = c_gate * u
        vp_ref[pl.ds(CARRY, CONV_SUB), :] = v
        conv = cw_ref[pl.ds(0, 1), :] * v
        for lag in range(1, CONV_WIDTH):
            conv = conv + cw_ref[pl.ds(lag, 1), :] * vp_ref[pl.ds(CARRY - lag, CONV_SUB), :]
        z = (b_gate * conv).astype(BF16)
        o_ref[rows, :] = x + _dot(z, wout_ref[...])
        vp_ref[pl.ds(0, CARRY), :] = vp_ref[pl.ds(CONV_SUB, CARRY), :]


def _conv_mixer(h, seq, layer, mixer, g, weights, w_conv, cast_items):
    n_tok = h.shape[0]
    cast_in, cast_out, cast_shapes = _cast_specs(n_tok // CONV_TM, cast_items)
    outs = pl.pallas_call(
        functools.partial(_conv_kernel, seq // CONV_TM, len(cast_items)),
        grid=(n_tok // CONV_TM,),
        in_specs=[
            pl.BlockSpec((CONV_TM, D_MODEL), lambda i: (i, 0)),
            _layer((1, D_MODEL), layer),
            _resident((D_MODEL, 3 * D_MODEL)),
            _layer((CONV_WIDTH, D_MODEL), mixer),
            _resident((D_MODEL, D_MODEL)),
        ] + cast_in,
        out_specs=[pl.BlockSpec((CONV_TM, D_MODEL), lambda i: (i, 0))] + cast_out,
        out_shape=[jax.ShapeDtypeStruct((n_tok, D_MODEL), F32)] + cast_shapes,
        scratch_shapes=[pltpu.VMEM((CARRY + CONV_SUB, D_MODEL), F32)],
        compiler_params=_params("arbitrary"),
        name="conv_mixer",
    )(h, g, weights[0], w_conv, weights[1], *(array for array, _ in cast_items))
    return outs[0], tuple(outs[1:])


QKV_TM = 1024
QKV_SUB = 512
QKV_TN = 2 * MXU_DIM
N_QKV_TILES = 3 * N_GROUPS


def _qkv_kernel(group, x_ref, g_ref, w_ref, gain_ref, bd_ref, perm_ref, o_ref, xs_ref):
    r = DILATION_GROUPS[group][1]
    rows_per_strand = UNIT // r
    for t in range(QKV_TM // QKV_SUB):
        rows = pl.ds(t * QKV_SUB, QKV_SUB)
        x = x_ref[rows, :]
        xn = x * lax.rsqrt(jnp.mean(x * x, axis=-1, keepdims=True) + RMS_EPS) * g_ref[...]
        if r == ROW_STRIDE:
            slabs = range(D_MODEL // LANES)
            for p in slabs:
                xs_ref[t, p] = xn[:, p * LANES:(p + 1) * LANES]
            xn = jnp.concatenate(
                [jnp.concatenate([xs_ref[t, p, pl.ds(u * UNIT + s, rows_per_strand, stride=r), :] for p in slabs],
                                 axis=1)
                 for u in range(QKV_SUB // UNIT) for s in range(r)], axis=0)
        xn = xn.astype(BF16)
        if r > ROW_STRIDE:
            xn = jnp.concatenate(
                [_dot(perm_ref[...], xn[u * UNIT:(u + 1) * UNIT]).astype(BF16) for u in range(QKV_SUB // UNIT)],
                axis=0)
        for c in range(3 * D_MODEL // QKV_TN):
            cols = pl.ds(c * QKV_TN, QKV_TN)
            y = _dot(xn, w_ref[:, cols])
            if c * QKV_TN >= 2 * D_MODEL:
                res = y.astype(BF16)
            else:
                y2 = (y * y).astype(BF16)
                ms = jnp.concatenate(
                    [_dot(y2[:, s * MXU_DIM:(s + 1) * MXU_DIM], bd_ref[...]) for s in range(QKV_TN // MXU_DIM)],
                    axis=1)
                res = (y * lax.rsqrt(ms + RMS_EPS) * gain_ref[:, cols]).astype(BF16)
            for s in range(QKV_TN // LANES):
                o_ref[c * (QKV_TN // LANES) + s, rows, :] = res[:, s * LANES:(s + 1) * LANES]


def _qkv_proj(h, layer, group, g, w_qkv, q_gain, k_gain):
    n_tok = h.shape[0]
    gains = jnp.concatenate(
        [jnp.tile(q_gain[group], N_HEADS) * (HEAD_DIM ** -0.5 * LOG2E), jnp.tile(k_gain[group], N_HEADS),
         jnp.ones((D_MODEL,), F32)]).reshape(1, 3 * D_MODEL)
    seg = np.arange(MXU_DIM) // HEAD_DIM
    block_diag = jnp.asarray((seg[:, None] == seg[None, :]) / HEAD_DIM, BF16)
    r = DILATION_GROUPS[group][1]
    tok = np.arange(UNIT)
    perm = np.zeros((UNIT, UNIT), np.float32)
    perm[(tok % r) * (UNIT // r) + tok // r, tok] = 1.0
    return pl.pallas_call(
        functools.partial(_qkv_kernel, group),
        grid=(n_tok // QKV_TM,),
        in_specs=[
            pl.BlockSpec((QKV_TM, D_MODEL), lambda i: (i, 0)),
            _layer((1, D_MODEL), layer),
            pl.BlockSpec((D_MODEL, 3 * D_MODEL), lambda i: (0, group), pipeline_mode=pl.Buffered(1)),
            _resident((1, 3 * D_MODEL)),
            _resident((MXU_DIM, MXU_DIM)),
            _resident((UNIT, UNIT)),
        ],
        out_specs=pl.BlockSpec((3 * N_PAIRS, QKV_TM, LANES), lambda i: (0, i, 0)),
        out_shape=jax.ShapeDtypeStruct((3 * N_PAIRS, n_tok, LANES), BF16),
        scratch_shapes=[pltpu.VMEM((QKV_TM // QKV_SUB, D_MODEL // LANES, QKV_SUB, LANES), F32)],
        compiler_params=_params("arbitrary"),
        name=f"qkv_proj_g{group}",
    )(h, g, w_qkv, gains, block_diag, jnp.asarray(perm, BF16))


def _t5_bucket_np(dist):
    nf = np.maximum(dist, 1).astype(np.float32)
    large = MAX_EXACT + (np.log(nf / np.float32(MAX_EXACT)) / np.float32(math.log(MAX_DISTANCE / MAX_EXACT))
                         * np.float32(N_BUCKETS - MAX_EXACT)).astype(np.int32)
    large = np.minimum(large, N_BUCKETS - 1)
    return np.where(dist < MAX_EXACT, dist, large)


def _bucket_maps():
    i = (np.arange(2 * BLOCK) % BLOCK)[:, None]
    j = np.arange(2 * BLOCK)[None, :]
    step = BLOCK + i - j
    maps = []
    for window, dilation in DILATION_GROUPS:
        n_steps = window // dilation
        in_band = (step >= 0) & (step <= n_steps)
        bucket = _t5_bucket_np(np.clip(step, 0, n_steps) * dilation)
        maps.append(np.where(in_band, bucket, -1).astype(np.int32))
    return np.stack(maps)


def _bias_kernel(buckets, rel_ref, bm_ref, o_ref):
    p = pl.program_id(0)
    shape = bm_ref.shape[1:]
    top = lax.broadcasted_iota(jnp.int32, shape, 0) < BLOCK
    col = lax.broadcasted_iota(jnp.int32, shape, 1)
    for g in range(N_GROUPS):
        bm = bm_ref[g]
        acc = jnp.zeros(shape, F32)
        for b in buckets[g]:
            val = jnp.where(top, rel_ref[b, g * N_HEADS + 2 * p], rel_ref[b, g * N_HEADS + 2 * p + 1])
            acc = jnp.where(bm == b, val, acc)
        rest = jnp.where(bm < 0, NEG_INF, acc * LOG2E)
        o_ref[g, 1] = rest
        o_ref[g, 0] = jnp.where(col < BLOCK, NEG_INF, rest)


def _bias_tiles(rel_bias):
    tile = (2 * BLOCK, 2 * BLOCK)
    maps = _bucket_maps()
    buckets = tuple(tuple(int(b) for b in np.unique(m[m >= 0])) for m in maps)
    return pl.pallas_call(
        functools.partial(_bias_kernel, buckets),
        grid=(N_PAIRS,),
        in_specs=[pl.BlockSpec(memory_space=pltpu.SMEM), _resident((N_GROUPS,) + tile)],
        out_specs=pl.BlockSpec((N_GROUPS, 2, None) + tile, lambda p: (0, 0, p, 0, 0)),
        out_shape=jax.ShapeDtypeStruct((N_GROUPS, 2, N_PAIRS) + tile, F32),
        compiler_params=_params("arbitrary"),
        name="bias_tiles",
    )(rel_bias, jnp.asarray(maps))


ATTN_BLOCKS = 16


def _attn_kernel(nb, q_ref, kp_ref, kc_ref, vp_ref, vc_ref, bias_ref, o_ref, lse_ref):
    upb = q_ref.shape[1] // nb
    n_strands, rows_per_unit = q_ref.shape[2], q_ref.shape[3]
    not_first = jnp.minimum(pl.program_id(2), 1)
    lane = lax.broadcasted_iota(jnp.int32, (BLOCK, LANES), 1)
    lo = lane < HEAD_DIM
    ones = jnp.ones((2 * BLOCK, LANES), BF16)

    def keys_of(prev_ref, cur_ref, blk, p, j):
        if blk == 0:
            return jnp.concatenate([prev_ref[p, :, j].reshape(BLOCK, LANES),
                                    cur_ref[p, pl.ds(0, upb), j].reshape(BLOCK, LANES)], axis=0)
        return cur_ref[p, pl.ds((blk - 1) * upb, 2 * upb), j].reshape(2 * BLOCK, LANES)

    for j in range(n_strands):
        for blk in range(nb):
            units = pl.ds(blk * upb, upb)
            m_all = jnp.zeros((BLOCK, LANES), F32)
            s_all = jnp.ones((BLOCK, LANES), F32)
            for p in range(N_PAIRS):
                q = q_ref[p, units, j].reshape(BLOCK, LANES)
                zero = jnp.zeros_like(q)
                q2 = jnp.concatenate([jnp.where(lo, q, zero), jnp.where(lo, zero, q)], axis=0)
                k = keys_of(kp_ref, kc_ref, blk, p, j)
                logits = lax.dot_general(q2, k, (((1,), (1,)), ((), ())), preferred_element_type=F32)
                logits = logits + (bias_ref[not_first, p] if blk == 0 else bias_ref[1, p])
                m = jnp.max(logits, axis=-1, keepdims=True)
                prob = jnp.exp2(logits - m).astype(BF16)
                acc = _dot(prob, jnp.concatenate([keys_of(vp_ref, vc_ref, blk, p, j), ones], axis=1))
                o = jnp.where(lo, acc[:BLOCK, :LANES], acc[BLOCK:, :LANES])
                s = jnp.where(lo, acc[:BLOCK, LANES:], acc[BLOCK:, LANES:])
                mm = jnp.where(lo, m[:BLOCK], m[BLOCK:])
                o_ref[p, units, j] = (o / s).astype(BF16).reshape(upb, rows_per_unit, LANES)
                keep = (lane == 2 * p) | (lane == HEAD_DIM + 2 * p + 1)
                m_all = jnp.where(keep, mm, m_all)
                s_all = jnp.where(keep, s, s_all)
            lse_ref[units, j] = (m_all * LN2 + jnp.log(s_all)).reshape(upb, rows_per_unit, LANES)


def _attn_group(qkv, bias, g, batch, seq):
    r = DILATION_GROUPS[g][1]
    rows_per_unit = min(UNIT // r, BLOCK)
    n_units = seq // (r * rows_per_unit)
    upb = BLOCK // rows_per_unit
    nb = min(ATTN_BLOCKS, seq // (r * BLOCK))
    n_strands = ATTN_BLOCKS // nb
    view = qkv.reshape(3 * N_PAIRS, batch, n_units, r, rows_per_unit, LANES)

    def cur(which):
        return lambda b, s, c: (which, b, c, s, 0, 0)

    def prev(which):
        return lambda b, s, c: (which, b, jnp.maximum(c * nb - 1, 0), s, 0, 0)

    cur_blk = (N_PAIRS, None, nb * upb, n_strands, rows_per_unit, LANES)
    prev_blk = (N_PAIRS, None, upb, n_strands, rows_per_unit, LANES)
    o, lse = pl.pallas_call(
        functools.partial(_attn_kernel, nb),
        grid=(batch, r // n_strands, seq // (r * nb * BLOCK)),
        in_specs=[
            pl.BlockSpec(cur_blk, cur(0)),
            pl.BlockSpec(prev_blk, prev(1)),
            pl.BlockSpec(cur_blk, cur(1)),
            pl.BlockSpec(prev_blk, prev(2)),
            pl.BlockSpec(cur_blk, cur(2)),
            pl.BlockSpec((None, 2, N_PAIRS, 2 * BLOCK, 2 * BLOCK), lambda b, s, c: (g, 0, 0, 0, 0)),
        ],
        out_specs=[
            pl.BlockSpec(cur_blk, lambda b, s, c: (0, b, c, s, 0, 0)),
            pl.BlockSpec((None, nb * upb, n_strands, rows_per_unit, LANES), lambda b, s, c: (b, c, s, 0, 0)),
        ],
        out_shape=[
            jax.ShapeDtypeStruct((N_PAIRS, batch, n_units, r, rows_per_unit, LANES), BF16),
            jax.ShapeDtypeStruct((batch, n_units, r, rows_per_unit, LANES), F32),
        ],
        compiler_params=_params("arbitrary", "arbitrary", "arbitrary"),
        name=f"attn_r{r}",
    )(view, view, view, view, view, bias)
    return o.reshape(N_PAIRS, batch * seq, LANES), lse.reshape(batch * seq, LANES)


MERGE_TM = 1024
MERGE_SUB = 512


def _merge_kernel(o0_ref, o1_ref, o2_ref, l0_ref, l1_ref, l2_ref, e_ref, w_ref, out_ref,
                  tok_ref, ltok_ref, tmp_ref, ltmp_ref, merged_ref):
    o_refs = (o0_ref, o1_ref, o2_ref)
    l_refs = (l0_ref, l1_ref, l2_ref)
    for t in range(MERGE_TM // MERGE_SUB):
        first = t * MERGE_SUB
        rows = pl.ds(first, MERGE_SUB)
        lses = [l0_ref[rows, :]]
        for g in range(1, N_GROUPS):
            r = DILATION_GROUPS[g][1]
            rows_per_strand = UNIT // r
            for u in range(MERGE_SUB // UNIT):
                unit = first + u * UNIT
                if r == ROW_STRIDE:
                    for s in range(r):
                        src_rows = pl.ds(unit + s * rows_per_strand, rows_per_strand)
                        dst_rows = pl.ds(unit + s, rows_per_strand, stride=r)
                        ltok_ref[g - 1, dst_rows, :] = l_refs[g][src_rows, :]
                        for p in range(N_PAIRS):
                            tok_ref[g - 1, p, dst_rows, :] = o_refs[g][p, src_rows, :].astype(F32)
                else:
                    assert _stride_passes(r) == 2
                    quarter = UNIT // ROW_STRIDE
                    for a in range(ROW_STRIDE):
                        for b in range(ROW_STRIDE):
                            src_rows = pl.ds(unit + (ROW_STRIDE * a + b) * rows_per_strand, rows_per_strand)
                            dst_rows = pl.ds(unit + b * quarter + a, rows_per_strand, stride=ROW_STRIDE)
                            ltmp_ref[dst_rows, :] = l_refs[g][src_rows, :]
                            for p in range(N_PAIRS):
                                tmp_ref[p, dst_rows, :] = o_refs[g][p, src_rows, :].astype(F32)
                    for b in range(ROW_STRIDE):
                        src_rows = pl.ds(unit + b * quarter, quarter)
                        dst_rows = pl.ds(unit + b, quarter, stride=ROW_STRIDE)
                        ltok_ref[g - 1, dst_rows, :] = ltmp_ref[src_rows, :]
                        for p in range(N_PAIRS):
                            tok_ref[g - 1, p, dst_rows, :] = tmp_ref[p, src_rows, :]
            lses.append(ltok_ref[g - 1, rows, :])
        m = jnp.maximum(jnp.maximum(lses[0], lses[1]), lses[2])
        es = [jnp.exp(l - m) for l in lses]
        den = es[0] + es[1] + es[2]
        w_cat = []
        for g in range(1, N_GROUPS):
            w = es[g] / den
            w_hi = w.astype(BF16)
            w_cat.append(jnp.concatenate([w_hi, (w - w_hi.astype(F32)).astype(BF16)], axis=1))
        for c in range(D_MODEL // MXU_DIM):
            pairs = range(c * MXU_DIM // LANES, (c + 1) * MXU_DIM // LANES)
            base = jnp.concatenate([o0_ref[p, rows, :] for p in pairs], axis=1).astype(F32)
            acc = base
            for g in range(1, N_GROUPS):
                w_full = _dot(w_cat[g - 1], e_ref[:, pl.ds(c * MXU_DIM, MXU_DIM)])
                acc = acc + w_full * (jnp.concatenate([tok_ref[g - 1, p, rows, :] for p in pairs], axis=1) - base)
            merged_ref[rows, pl.ds(c * MXU_DIM, MXU_DIM)] = acc.astype(BF16)
        out_ref[rows, :] = _dot(merged_ref[rows, :], w_ref[...])


def _merge_proj(outs, lses, w_out):
    n_tok = lses[0].shape[0]
    expand = np.zeros((2 * LANES, D_MODEL), np.float32)
    for head in range(N_HEADS):
        src = head if head % 2 == 0 else HEAD_DIM + head
        expand[src, head * HEAD_DIM:(head + 1) * HEAD_DIM] = 1.0
    expand[LANES:] = expand[:LANES]
    tok = lambda width: pl.BlockSpec((MERGE_TM, width), lambda i: (i, 0))
    slab = pl.BlockSpec((N_PAIRS, MERGE_TM, LANES), lambda i: (0, i, 0))
    return pl.pallas_call(
        _merge_kernel,
        grid=(n_tok // MERGE_TM,),
        in_specs=[slab] * N_GROUPS + [tok(LANES)] * N_GROUPS
        + [_resident((2 * LANES, D_MODEL)), _resident((D_MODEL, D_MODEL))],
        out_specs=tok(D_MODEL),
        out_shape=jax.ShapeDtypeStruct((n_tok, D_MODEL), F32),
        scratch_shapes=[
            pltpu.VMEM((N_GROUPS - 1, N_PAIRS, MERGE_TM, LANES), F32),
            pltpu.VMEM((N_GROUPS - 1, MERGE_TM, LANES), F32),
            pltpu.VMEM((N_PAIRS, MERGE_TM, LANES), F32),
            pltpu.VMEM((MERGE_TM, LANES), F32),
            pltpu.VMEM((MERGE_TM, D_MODEL), BF16),
        ],
        compiler_params=_params("arbitrary"),
        name="merge_proj",
    )(*outs, *lses, jnp.asarray(expand, BF16), w_out)


def kernel(x, norm_ffn1, ffn1_w_in, ffn1_w_out, norm_mix, conv_w_in, conv_w, conv_w_out, attn_w_qkv,
           attn_q_gain, attn_k_gain, attn_w_out, rel_bias, norm_ffn2, ffn2_w_in, ffn2_w_out):
    batch, seq, d_model = x.shape
    assert d_model == D_MODEL and seq % max(QKV_TM, CONV_TM, DILATION_GROUPS[-1][1] * BLOCK) == 0
    depth = norm_ffn1.shape[0]
    h = x.reshape(batch * seq, D_MODEL)
    bias = _bias_tiles(rel_bias) if depth > 1 else None
    gains = [a.reshape(depth, 1, D_MODEL) for a in (norm_ffn1, norm_mix, norm_ffn2)]
    ffn1_weights = (ffn1_w_in, ffn1_w_out)
    for i in range(depth):
        ffn2_items = [(ffn2_w_in, i), (ffn2_w_out, i)]
        ffn1_items = [(ffn1_w_in, i + 1), (ffn1_w_out, i + 1)] if i + 1 < depth else []
        j = i // 2
        if i % 2 == 0:
            h, mixer_weights = _ffn([h], i, gains[0], ffn1_weights, [(conv_w_in, j), (conv_w_out, j)])
            h, ffn2_weights = _conv_mixer(h, seq, i, j, gains[1], mixer_weights, conv_w, ffn2_items)
            mixed = [h]
        else:
            h, cast = _ffn([h], i, gains[0], ffn1_weights, [(attn_w_qkv, j), (attn_w_out, j)] + ffn2_items)
            (w_qkv, w_out), ffn2_weights = cast[:2], cast[2:]
            parts = [_attn_group(_qkv_proj(h, i, g, gains[1], w_qkv, attn_q_gain[j], attn_k_gain[j]),
                                 bias, g, batch, seq) for g in range(N_GROUPS)]
            mixed = [h, _merge_proj([o for o, _ in parts], [l for _, l in parts], w_out)]
        h, ffn1_weights = _ffn(mixed, i, gains[2], ffn2_weights, ffn1_items)
    return h.reshape(batch, seq, D_MODEL)
```

```python
import functools
import math

import numpy as np
import jax
import jax.numpy as jnp
from jax import lax
from jax.experimental import pallas as pl
from jax.experimental.pallas import tpu as pltpu

D_MODEL = 1024
D_FF = 2816
CONV_WIDTH = 3
HEAD_DIM = 64
N_HEADS = D_MODEL // HEAD_DIM
N_PAIRS = N_HEADS // 2
DILATION_GROUPS = ((128, 1), (512, 4), (2048, 16))
N_GROUPS = len(DILATION_GROUPS)
BLOCK = 128
N_BUCKETS = 32
MAX_EXACT = N_BUCKETS // 2
MAX_DISTANCE = 2048
RMS_EPS = 1e-6
NEG_INF = -1e30
LOG2E = math.log2(math.e)
LN2 = math.log(2.0)

LANES = 128
BF16_TILE_ROWS = 16
MXU_DIM = 256
UNIT = MXU_DIM
VMEM_LIMIT = 56 * 1024 * 1024

F32 = jnp.float32
BF16 = jnp.bfloat16


def _rmsnorm_bf16(x, g):
    ms = jnp.mean(x * x, axis=-1, keepdims=True)
    return (x * lax.rsqrt(ms + RMS_EPS) * g).astype(BF16)


def _dot(a, b):
    return jnp.dot(a, b, preferred_element_type=F32)


def _resident(shape):
    return pl.BlockSpec(shape, lambda *_: (0,) * len(shape), pipeline_mode=pl.Buffered(1))


def _layer(shape, layer):
    return pl.BlockSpec((None,) + shape, lambda *_: (layer,) + (0,) * len(shape), pipeline_mode=pl.Buffered(1))


def _load_weights(src, dst_ref, stage_ref, sem):
    rows = stage_ref.shape[1]
    n_chunks = dst_ref.shape[0] // rows

    def copy(c):
        return pltpu.make_async_copy(src.at[pl.ds(c * rows, rows)], stage_ref.at[c % 2], sem.at[c % 2])

    copy(0).start()
    for c in range(n_chunks):
        if c + 1 < n_chunks:
            copy(c + 1).start()
        copy(c).wait()
        dst_ref[pl.ds(c * rows, rows), :] = stage_ref[c % 2].astype(BF16)


def _weight_scratch(rows, cols, stage_rows):
    assert rows % stage_rows == 0
    return [pltpu.VMEM((rows, cols), BF16), pltpu.VMEM((2, stage_rows, cols), F32)]


def _cast_specs(n_steps, items):
    in_specs, out_specs, out_shapes = [], [], []
    for array, lead in items:
        rows, cols = array.shape[1:]
        n_blocks = n_steps if rows % (BF16_TILE_ROWS * n_steps) == 0 else n_steps // 2
        assert rows % (BF16_TILE_ROWS * n_blocks) == 0
        in_specs.append(pl.BlockSpec((None, rows // n_blocks, cols),
                                     lambda i, lead=lead, last=n_blocks - 1: (lead, jnp.minimum(i, last), 0)))
        out_specs.append(pl.BlockSpec((rows // n_blocks, cols), lambda i, last=n_blocks - 1: (jnp.minimum(i, last), 0)))
        out_shapes.append(jax.ShapeDtypeStruct((rows, cols), BF16))
    return in_specs, out_specs, out_shapes


def _cast_blocks(src_refs, dst_refs):
    for src, dst in zip(src_refs, dst_refs):
        dst[...] = src[...].astype(BF16)


HBM = pl.BlockSpec(memory_space=pl.ANY)
DMA_SEMS = pltpu.SemaphoreType.DMA((2,))


def _params(*sem):
    return pltpu.CompilerParams(dimension_semantics=sem, vmem_limit_bytes=VMEM_LIMIT)


assert DILATION_GROUPS[0][1] == 1

ROW_STRIDE = 4


def _stride_passes(r):
    assert r in (1, ROW_STRIDE, ROW_STRIDE ** 2)
    return {1: 1, ROW_STRIDE: 1, ROW_STRIDE ** 2: 2}[r]


FFN_TM = 1024
FFN_SUB = 512
FFN_CHUNKS = ((0, 5 * MXU_DIM), (5 * MXU_DIM, 6 * MXU_DIM))


def _ffn_kernel(layer, n_inputs, preconverted, n_cast, *refs):
    refs = list(refs)
    x_refs, (g_ref, w_in, w_out) = refs[:n_inputs], refs[n_inputs:n_inputs + 3]
    refs = refs[n_inputs + 3:]
    next_f32, o_ref, next_bf16, scratch = refs[:n_cast], refs[n_cast], refs[n_cast + 1:2 * n_cast + 1], refs[2 * n_cast + 1:]
    if preconverted:
        win_ref, wout_ref = w_in, w_out
    else:
        win_ref, win_stage, wout_ref, wout_stage, sem = scratch

        @pl.when(pl.program_id(0) == 0)
        def _():
            _load_weights(w_in.at[layer], win_ref, win_stage, sem)
            _load_weights(w_out.at[layer], wout_ref, wout_stage, sem)

    _cast_blocks(next_f32, next_bf16)

    for t in range(FFN_TM // FFN_SUB):
        rows = pl.ds(t * FFN_SUB, FFN_SUB)
        x = x_refs[0][rows, :]
        for extra_ref in x_refs[1:]:
            x = x + extra_ref[rows, :]
        xn = _rmsnorm_bf16(x, g_ref[...])
        acc = None
        for start, size in FFN_CHUNKS:
            gate = _dot(xn, win_ref[:, pl.ds(start, size)])
            up = _dot(xn, win_ref[:, pl.ds(D_FF + start, size)])
            a = (gate * jax.nn.sigmoid(gate) * up).astype(BF16)
            part = _dot(a, wout_ref[pl.ds(start, size), :])
            acc = part if acc is None else acc + part
        o_ref[rows, :] = x + 0.5 * acc


def _ffn(hs, layer, g, weights, cast_items=()):
    n_tok = hs[0].shape[0]
    n_steps = n_tok // FFN_TM
    preconverted = weights[0].dtype == BF16
    tile = pl.BlockSpec((FFN_TM, D_MODEL), lambda i: (i, 0))
    if preconverted:
        w_specs, scratch = [_resident((D_MODEL, 2 * D_FF)), _resident((D_FF, D_MODEL))], []
    else:
        w_specs = [HBM, HBM]
        scratch = (_weight_scratch(D_MODEL, 2 * D_FF, LANES // len(hs)) + _weight_scratch(D_FF, D_MODEL, D_FF // 8)
                   + [DMA_SEMS])
    cast_in, cast_out, cast_shapes = _cast_specs(n_steps, cast_items)
    outs = pl.pallas_call(
        functools.partial(_ffn_kernel, layer, len(hs), preconverted, len(cast_items)),
        grid=(n_steps,),
        in_specs=[tile] * len(hs) + [_layer((1, D_MODEL), layer)] + w_specs + cast_in,
        out_specs=[tile] + cast_out,
        out_shape=[jax.ShapeDtypeStruct((n_tok, D_MODEL), F32)] + cast_shapes,
        scratch_shapes=scratch,
        compiler_params=_params("arbitrary"),
        name="ffn",
    )(*hs, g, *weights, *(array for array, _ in cast_items))
    return outs[0], tuple(outs[1:])


CONV_TM = 1024
CONV_SUB = 512
CARRY = 8


def _conv_kernel(tiles_per_seq, n_cast, x_ref, g_ref, win_ref, cw_ref, wout_ref, *refs):
    next_f32, o_ref, next_bf16, vp_ref = refs[:n_cast], refs[n_cast], refs[n_cast + 1:2 * n_cast + 1], refs[-1]
    _cast_blocks(next_f32, next_bf16)

    @pl.when(pl.program_id(0) % tiles_per_seq == 0)
    def _():
        vp_ref[pl.ds(0, CARRY), :] = jnp.zeros((CARRY, D_MODEL), F32)

    for t in range(CONV_TM // CONV_SUB):
        rows = pl.ds(t * CONV_SUB, CONV_SUB)
        x = x_ref[rows, :]
        xn = _rmsnorm_bf16(x, g_ref[...])
        b_gate = _dot(xn, win_ref[:, pl.ds(0, D_MODEL)])
        c_gate = _dot(xn, win_ref[:, pl.ds(D_MODEL, D_MODEL)])
        u = _dot(xn, win_ref[:, pl.ds(2 * D_MODEL, D_MODEL)])
        v = c_gate * u
        vp_ref[pl.ds(CARRY, CONV_SUB), :] = v
        conv = cw_ref[pl.ds(0, 1), :] * v
        for lag in range(1, CONV_WIDTH):
            conv = conv + cw_ref[pl.ds(lag, 1), :] * vp_ref[pl.ds(CARRY - lag, CONV_SUB), :]
        z = (b_gate * conv).astype(BF16)
        o_ref[rows, :] = x + _dot(z, wout_ref[...])
        vp_ref[pl.ds(0, CARRY), :] = vp_ref[pl.ds(CONV_SUB, CARRY), :]


def _conv_mixer(h, seq, layer, mixer, g, weights, w_conv, cast_items):
    n_tok = h.shape[0]
    cast_in, cast_out, cast_shapes = _cast_specs(n_tok // CONV_TM, cast_items)
    outs = pl.pallas_call(
        functools.partial(_conv_kernel, seq // CONV_TM, len(cast_items)),
        grid=(n_tok // CONV_TM,),
        in_specs=[
            pl.BlockSpec((CONV_TM, D_MODEL), lambda i: (i, 0)),
            _layer((1, D_MODEL), layer),
            _resident((D_MODEL, 3 * D_MODEL)),
            _layer((CONV_WIDTH, D_MODEL), mixer),
            _resident((D_MODEL, D_MODEL)),
        ] + cast_in,
        out_specs=[pl.BlockSpec((CONV_TM, D_MODEL), lambda i: (i, 0))] + cast_out,
        out_shape=[jax.ShapeDtypeStruct((n_tok, D_MODEL), F32)] + cast_shapes,
        scratch_shapes=[pltpu.VMEM((CARRY + CONV_SUB, D_MODEL), F32)],
        compiler_params=_params("arbitrary"),
        name="conv_mixer",
    )(h, g, weights[0], w_conv, weights[1], *(array for array, _ in cast_items))
    return outs[0], tuple(outs[1:])


QKV_TM = 1024
QKV_SUB = 512
QKV_TN = 2 * MXU_DIM


def _qkv_kernel(group, x_ref, g_ref, w_ref, gain_ref, bd_ref, perm_ref, o_ref, xs_ref):
    r = DILATION_GROUPS[group][1]
    rows_per_strand = UNIT // r
    for t in range(QKV_TM // QKV_SUB):
        rows = pl.ds(t * QKV_SUB, QKV_SUB)
        x = x_ref[rows, :]
        xn = x * lax.rsqrt(jnp.mean(x * x, axis=-1, keepdims=True) + RMS_EPS) * g_ref[...]
        if r == ROW_STRIDE:
            slabs = range(D_MODEL // LANES)
            for p in slabs:
                xs_ref[t, p] = xn[:, p * LANES:(p + 1) * LANES]
            xn = jnp.concatenate(
                [jnp.concatenate([xs_ref[t, p, pl.ds(u * UNIT + s, rows_per_strand, stride=r), :] for p in slabs],
                                 axis=1)
                 for u in range(QKV_SUB // UNIT) for s in range(r)], axis=0)
        xn = xn.astype(BF16)
        if r > ROW_STRIDE:
            xn = jnp.concatenate(
                [_dot(perm_ref[...], xn[u * UNIT:(u + 1) * UNIT]).astype(BF16) for u in range(QKV_SUB // UNIT)],
                axis=0)
        for c in range(3 * D_MODEL // QKV_TN):
            cols = pl.ds(c * QKV_TN, QKV_TN)
            y = _dot(xn, w_ref[:, cols])
            if c * QKV_TN >= 2 * D_MODEL:
                res = y.astype(BF16)
            else:
                y2 = (y * y).astype(BF16)
                ms = jnp.concatenate(
                    [_dot(y2[:, s * MXU_DIM:(s + 1) * MXU_DIM], bd_ref[...]) for s in range(QKV_TN // MXU_DIM)],
                    axis=1)
                res = (y * lax.rsqrt(ms + RMS_EPS) * gain_ref[:, cols]).astype(BF16)
            for s in range(QKV_TN // LANES):
                o_ref[c * (QKV_TN // LANES) + s, rows, :] = res[:, s * LANES:(s + 1) * LANES]


def _qkv_proj(h, layer, group, g, w_qkv, q_gain, k_gain):
    n_tok = h.shape[0]
    gains = jnp.concatenate(
        [jnp.tile(q_gain[group], N_HEADS) * (HEAD_DIM ** -0.5 * LOG2E), jnp.tile(k_gain[group], N_HEADS),
         jnp.ones((D_MODEL,), F32)]).reshape(1, 3 * D_MODEL)
    seg = np.arange(MXU_DIM) // HEAD_DIM
    block_diag = jnp.asarray((seg[:, None] == seg[None, :]) / HEAD_DIM, BF16)
    r = DILATION_GROUPS[group][1]
    tok = np.arange(UNIT)
    perm = np.zeros((UNIT, UNIT), np.float32)
    perm[(tok % r) * (UNIT // r) + tok // r, tok] = 1.0
    return pl.pallas_call(
        functools.partial(_qkv_kernel, group),
        grid=(n_tok // QKV_TM,),
        in_specs=[
            pl.BlockSpec((QKV_TM, D_MODEL), lambda i: (i, 0)),
            _layer((1, D_MODEL), layer),
            pl.BlockSpec((D_MODEL, 3 * D_MODEL), lambda i: (0, group), pipeline_mode=pl.Buffered(1)),
            _resident((1, 3 * D_MODEL)),
            _resident((MXU_DIM, MXU_DIM)),
            _resident((UNIT, UNIT)),
        ],
        out_specs=pl.BlockSpec((3 * N_PAIRS, QKV_TM, LANES), lambda i: (0, i, 0)),
        out_shape=jax.ShapeDtypeStruct((3 * N_PAIRS, n_tok, LANES), BF16),
        scratch_shapes=[pltpu.VMEM((QKV_TM // QKV_SUB, D_MODEL // LANES, QKV_SUB, LANES), F32)],
        compiler_params=_params("arbitrary"),
        name=f"qkv_proj_g{group}",
    )(h, g, w_qkv, gains, block_diag, jnp.asarray(perm, BF16))


def _t5_bucket_np(dist):
    nf = np.maximum(dist, 1).astype(np.float32)
    large = MAX_EXACT + (np.log(nf / np.float32(MAX_EXACT)) / np.float32(math.log(MAX_DISTANCE / MAX_EXACT))
                         * np.float32(N_BUCKETS - MAX_EXACT)).astype(np.int32)
    large = np.minimum(large, N_BUCKETS - 1)
    return np.where(dist < MAX_EXACT, dist, large)


def _bucket_maps():
    i = (np.arange(2 * BLOCK) % BLOCK)[:, None]
    j = np.arange(2 * BLOCK)[None, :]
    step = BLOCK + i - j
    maps = []
    for window, dilation in DILATION_GROUPS:
        n_steps = window // dilation
        in_band = (step >= 0) & (step <= n_steps)
        bucket = _t5_bucket_np(np.clip(step, 0, n_steps) * dilation)
        maps.append(np.where(in_band, bucket, -1).astype(np.int32))
    return np.stack(maps)


def _bias_kernel(buckets, rel_ref, bm_ref, o_ref):
    p = pl.program_id(0)
    shape = bm_ref.shape[1:]
    top = lax.broadcasted_iota(jnp.int32, shape, 0) < BLOCK
    col = lax.broadcasted_iota(jnp.int32, shape, 1)
    for g in range(N_GROUPS):
        bm = bm_ref[g]
        acc = jnp.zeros(shape, F32)
        for b in buckets[g]:
            val = jnp.where(top, rel_ref[b, g * N_HEADS + 2 * p], rel_ref[b, g * N_HEADS + 2 * p + 1])
            acc = jnp.where(bm == b, val, acc)
        rest = jnp.where(bm < 0, NEG_INF, acc * LOG2E)
        o_ref[g, 1] = rest
        o_ref[g, 0] = jnp.where(col < BLOCK, NEG_INF, rest)


def _bias_tiles(rel_bias):
    tile = (2 * BLOCK, 2 * BLOCK)
    maps = _bucket_maps()
    buckets = tuple(tuple(int(b) for b in np.unique(m[m >= 0])) for m in maps)
    return pl.pallas_call(
        functools.partial(_bias_kernel, buckets),
        grid=(N_PAIRS,),
        in_specs=[pl.BlockSpec(memory_space=pltpu.SMEM), _resident((N_GROUPS,) + tile)],
        out_specs=pl.BlockSpec((N_GROUPS, 2, None) + tile, lambda p: (0, 0, p, 0, 0)),
        out_shape=jax.ShapeDtypeStruct((N_GROUPS, 2, N_PAIRS) + tile, F32),
        compiler_params=_params("arbitrary"),
        name="bias_tiles",
    )(rel_bias, jnp.asarray(maps))


ATTN_BLOCKS = 16


def _attn_kernel(nb, q_ref, kp_ref, kc_ref, vp_ref, vc_ref, bias_ref, o_ref, lse_ref):
    upb = q_ref.shape[1] // nb
    n_strands, rows_per_unit = q_ref.shape[2], q_ref.shape[3]
    not_first = jnp.minimum(pl.program_id(2), 1)
    lane = lax.broadcasted_iota(jnp.int32, (BLOCK, LANES), 1)
    lo = lane < HEAD_DIM
    ones = jnp.ones((2 * BLOCK, LANES), BF16)

    def keys_of(prev_ref, cur_ref, blk, p, j):
        if blk == 0:
            return jnp.concatenate([prev_ref[p, :, j].reshape(BLOCK, LANES),
                                    cur_ref[p, pl.ds(0, upb), j].reshape(BLOCK, LANES)], axis=0)
        return cur_ref[p, pl.ds((blk - 1) * upb, 2 * upb), j].reshape(2 * BLOCK, LANES)

    for j in range(n_strands):
        for blk in range(nb):
            units = pl.ds(blk * upb, upb)
            m_all = jnp.zeros((BLOCK, LANES), F32)
            s_all = jnp.ones((BLOCK, LANES), F32)
            for p in range(N_PAIRS):
                q = q_ref[p, units, j].reshape(BLOCK, LANES)
                zero = jnp.zeros_like(q)
                q2 = jnp.concatenate([jnp.where(lo, q, zero), jnp.where(lo, zero, q)], axis=0)
                k = keys_of(kp_ref, kc_ref, blk, p, j)
                logits = lax.dot_general(q2, k, (((1,), (1,)), ((), ())), preferred_element_type=F32)
                logits = logits + (bias_ref[not_first, p] if blk == 0 else bias_ref[1, p])
                m = jnp.max(logits, axis=-1, keepdims=True)
                prob = jnp.exp2(logits - m).astype(BF16)
                acc = _dot(prob, jnp.concatenate([keys_of(vp_ref, vc_ref, blk, p, j), ones], axis=1))
                o = jnp.where(lo, acc[:BLOCK, :LANES], acc[BLOCK:, :LANES])
                s = jnp.where(lo, acc[:BLOCK, LANES:], acc[BLOCK:, LANES:])
                mm = jnp.where(lo, m[:BLOCK], m[BLOCK:])
                o_ref[p, units, j] = (o / s).astype(BF16).reshape(upb, rows_per_unit, LANES)
                keep = (lane == 2 * p) | (lane == HEAD_DIM + 2 * p + 1)
                m_all = jnp.where(keep, mm, m_all)
                s_all = jnp.where(keep, s, s_all)
            lse_ref[units, j] = (m_all * LN2 + jnp.log(s_all)).reshape(upb, rows_per_unit, LANES)


def _attn_group(qkv, bias, g, batch, seq):
    r = DILATION_GROUPS[g][1]
    rows_per_unit = min(UNIT // r, BLOCK)
    n_units = seq // (r * rows_per_unit)
    upb = BLOCK // rows_per_unit
    nb = min(ATTN_BLOCKS, seq // (r * BLOCK))
    n_strands = ATTN_BLOCKS // nb
    view = qkv.reshape(3 * N_PAIRS, batch, n_units, r, rows_per_unit, LANES)

    def cur(which):
        return lambda b, s, c: (which, b, c, s, 0, 0)

    def prev(which):
        return lambda b, s, c: (which, b, jnp.maximum(c * nb - 1, 0), s, 0, 0)

    cur_blk = (N_PAIRS, None, nb * upb, n_strands, rows_per_unit, LANES)
    prev_blk = (N_PAIRS, None, upb, n_strands, rows_per_unit, LANES)
    o, lse = pl.pallas_call(
        functools.partial(_attn_kernel, nb),
        grid=(batch, r // n_strands, seq // (r * nb * BLOCK)),
        in_specs=[
            pl.BlockSpec(cur_blk, cur(0)),
            pl.BlockSpec(prev_blk, prev(1)),
            pl.BlockSpec(cur_blk, cur(1)),
            pl.BlockSpec(prev_blk, prev(2)),
            pl.BlockSpec(cur_blk, cur(2)),
            pl.BlockSpec((None, 2, N_PAIRS, 2 * BLOCK, 2 * BLOCK), lambda b, s, c: (g, 0, 0, 0, 0)),
        ],
        out_specs=[
            pl.BlockSpec(cur_blk, lambda b, s, c: (0, b, c, s, 0, 0)),
            pl.BlockSpec((None, nb * upb, n_strands, rows_per_unit, LANES), lambda b, s, c: (b, c, s, 0, 0)),
        ],
        out_shape=[
            jax.ShapeDtypeStruct((N_PAIRS, batch, n_units, r, rows_per_unit, LANES), BF16),
            jax.ShapeDtypeStruct((batch, n_units, r, rows_per_unit, LANES), F32),
        ],
        compiler_params=_params("arbitrary", "arbitrary", "arbitrary"),
        name=f"attn_r{r}",
    )(view, view, view, view, view, bias)
    return o.reshape(N_PAIRS, batch * seq, LANES), lse.reshape(batch * seq, LANES)


MERGE_TM = 1024
MERGE_SUB = 512


def _merge_kernel(o0_ref, o1_ref, o2_ref, l0_ref, l1_ref, l2_ref, e_ref, w_ref, out_ref,
                  tok_ref, ltok_ref, tmp_ref, ltmp_ref, merged_ref):
    o_refs = (o0_ref, o1_ref, o2_ref)
    l_refs = (l0_ref, l1_ref, l2_ref)
    for t in range(MERGE_TM // MERGE_SUB):
        first = t * MERGE_SUB
        rows = pl.ds(first, MERGE_SUB)
        lses = [l0_ref[rows, :]]
        for g in range(1, N_GROUPS):
            r = DILATION_GROUPS[g][1]
            rows_per_strand = UNIT // r
            for u in range(MERGE_SUB // UNIT):
                unit = first + u * UNIT
                if r == ROW_STRIDE:
                    for s in range(r):
                        src_rows = pl.ds(unit + s * rows_per_strand, rows_per_strand)
                        dst_rows = pl.ds(unit + s, rows_per_strand, stride=r)
                        ltok_ref[g - 1, dst_rows, :] = l_refs[g][src_rows, :]
                        for p in range(N_PAIRS):
                            tok_ref[g - 1, p, dst_rows, :] = o_refs[g][p, src_rows, :].astype(F32)
                else:
                    assert _stride_passes(r) == 2
                    quarter = UNIT // ROW_STRIDE
                    for a in range(ROW_STRIDE):
                        for b in range(ROW_STRIDE):
                            src_rows = pl.ds(unit + (ROW_STRIDE * a + b) * rows_per_strand, rows_per_strand)
                            dst_rows = pl.ds(unit + b * quarter + a, rows_per_strand, stride=ROW_STRIDE)
                            ltmp_ref[dst_rows, :] = l_refs[g][src_rows, :]
                            for p in range(N_PAIRS):
                                tmp_ref[p, dst_rows, :] = o_refs[g][p, src_rows, :].astype(F32)
                    for b in range(ROW_STRIDE):
                        src_rows = pl.ds(unit + b * quarter, quarter)
                        dst_rows = pl.ds(unit + b, quarter, stride=ROW_STRIDE)
                        ltok_ref[g - 1, dst_rows, :] = ltmp_ref[src_rows, :]
                        for p in range(N_PAIRS):
                            tok_ref[g - 1, p, dst_rows, :] = tmp_ref[p, src_rows, :]
            lses.append(ltok_ref[g - 1, rows, :])
        m = jnp.maximum(jnp.maximum(lses[0], lses[1]), lses[2])
        es = [jnp.exp(l - m) for l in lses]
        den = es[0] + es[1] + es[2]
        w_cat = []
        for g in range(1, N_GROUPS):
            w = es[g] / den
            w_hi = w.astype(BF16)
            w_cat.append(jnp.concatenate([w_hi, (w - w_hi.astype(F32)).astype(BF16)], axis=1))
        for c in range(D_MODEL // MXU_DIM):
            pairs = range(c * MXU_DIM // LANES, (c + 1) * MXU_DIM // LANES)
            base = jnp.concatenate([o0_ref[p, rows, :] for p in pairs], axis=1).astype(F32)
            acc = base
            for g in range(1, N_GROUPS):
                w_full = _dot(w_cat[g - 1], e_ref[:, pl.ds(c * MXU_DIM, MXU_DIM)])
                acc = acc + w_full * (jnp.concatenate([tok_ref[g - 1, p, rows, :] for p in pairs], axis=1) - base)
            merged_ref[rows, pl.ds(c * MXU_DIM, MXU_DIM)] = acc.astype(BF16)
        out_ref[rows, :] = _dot(merged_ref[rows, :], w_ref[...])


def _merge_proj(outs, lses, w_out):
    n_tok = lses[0].shape[0]
    expand = np.zeros((2 * LANES, D_MODEL), np.float32)
    for head in range(N_HEADS):
        src = head if head % 2 == 0 else HEAD_DIM + head
        expand[src, head * HEAD_DIM:(head + 1) * HEAD_DIM] = 1.0
    expand[LANES:] = expand[:LANES]
    tok = lambda width: pl.BlockSpec((MERGE_TM, width), lambda i: (i, 0))
    slab = pl.BlockSpec((N_PAIRS, MERGE_TM, LANES), lambda i: (0, i, 0))
    return pl.pallas_call(
        _merge_kernel,
        grid=(n_tok // MERGE_TM,),
        in_specs=[slab] * N_GROUPS + [tok(LANES)] * N_GROUPS
        + [_resident((2 * LANES, D_MODEL)), _resident((D_MODEL, D_MODEL))],
        out_specs=tok(D_MODEL),
        out_shape=jax.ShapeDtypeStruct((n_tok, D_MODEL), F32),
        scratch_shapes=[
            pltpu.VMEM((N_GROUPS - 1, N_PAIRS, MERGE_TM, LANES), F32),
            pltpu.VMEM((N_GROUPS - 1, MERGE_TM, LANES), F32),
            pltpu.VMEM((N_PAIRS, MERGE_TM, LANES), F32),
            pltpu.VMEM((MERGE_TM, LANES), F32),
            pltpu.VMEM((MERGE_TM, D_MODEL), BF16),
        ],
        compiler_params=_params("arbitrary"),
        name="merge_proj",
    )(*outs, *lses, jnp.asarray(expand, BF16), w_out)


def kernel(x, norm_ffn1, ffn1_w_in, ffn1_w_out, norm_mix, conv_w_in, conv_w, conv_w_out, attn_w_qkv,
           attn_q_gain, attn_k_gain, attn_w_out, rel_bias, norm_ffn2, ffn2_w_in, ffn2_w_out):
    batch, seq, d_model = x.shape
    assert d_model == D_MODEL and seq % max(QKV_TM, CONV_TM, DILATION_GROUPS[-1][1] * BLOCK) == 0
    depth = norm_ffn1.shape[0]
    h = x.reshape(batch * seq, D_MODEL)
    bias = _bias_tiles(rel_bias) if depth > 1 else None
    gains = [a.reshape(depth, 1, D_MODEL) for a in (norm_ffn1, norm_mix, norm_ffn2)]
    ffn1_weights = (ffn1_w_in, ffn1_w_out)
    for i in range(depth):
        ffn2_items = [(ffn2_w_in, i), (ffn2_w_out, i)]
        ffn1_items = [(ffn1_w_in, i + 1), (ffn1_w_out, i + 1)] if i + 1 < depth else []
        j = i // 2
        if i % 2 == 0:
            h, mixer_weights = _ffn([h], i, gains[0], ffn1_weights, [(conv_w_in, j), (conv_w_out, j)])
            h, ffn2_weights = _conv_mixer(h, seq, i, j, gains[1], mixer_weights, conv_w, ffn2_items)
            mixed = [h]
        else:
            h, cast = _ffn([h], i, gains[0], ffn1_weights, [(attn_w_qkv, j), (attn_w_out, j)] + ffn2_items)
            (w_qkv, w_out), ffn2_weights = cast[:2], cast[2:]
            parts = [_attn_group(_qkv_proj(h, i, g, gains[1], w_qkv, attn_q_gain[j], attn_k_gain[j]),
                                 bias, g, batch, seq) for g in range(N_GROUPS)]
            mixed = [h, _merge_proj([o for o, _ in parts], [l for _, l in parts], w_out)]
        h, ffn1_weights = _ffn(mixed, i, gains[2], ffn2_weights, ffn1_items)
    return h.reshape(batch, seq, D_MODEL)
```

```python
import functools
import math

import numpy as np
import jax
import jax.numpy as jnp
from jax import lax
from jax.experimental import pallas as pl
from jax.experimental.pallas import tpu as pltpu

D_MODEL = 1024
D_FF = 2816
CONV_WIDTH = 3
HEAD_DIM = 64
N_HEADS = D_MODEL // HEAD_DIM
N_PAIRS = N_HEADS // 2
DILATION_GROUPS = ((128, 1), (512, 4), (2048, 16))
N_GROUPS = len(DILATION_GROUPS)
BLOCK = 128
N_BUCKETS = 32
MAX_EXACT = N_BUCKETS // 2
MAX_DISTANCE = 2048
RMS_EPS = 1e-6
NEG_INF = -1e30
LOG2E = math.log2(math.e)
LN2 = math.log(2.0)

LANES = 128
BF16_TILE_ROWS = 16
MXU_DIM = 256
UNIT = MXU_DIM
VMEM_LIMIT = 56 * 1024 * 1024

F32 = jnp.float32
BF16 = jnp.bfloat16


def _rmsnorm_bf16(x, g):
    ms = jnp.mean(x * x, axis=-1, keepdims=True)
    return (x * lax.rsqrt(ms + RMS_EPS) * g).astype(BF16)


def _dot(a, b):
    return jnp.dot(a, b, preferred_element_type=F32)


def _resident(shape):
    return pl.BlockSpec(shape, lambda *_: (0,) * len(shape), pipeline_mode=pl.Buffered(1))


def _layer(shape, layer):
    return pl.BlockSpec((None,) + shape, lambda *_: (layer,) + (0,) * len(shape), pipeline_mode=pl.Buffered(1))


def _cast_specs(n_steps, items):
    in_specs, out_specs, out_shapes = [], [], []
    for array, lead in items:
        rows, cols = array.shape[1:]
        n_blocks = n_steps if rows % (BF16_TILE_ROWS * n_steps) == 0 else n_steps // 2
        assert rows % (BF16_TILE_ROWS * n_blocks) == 0
        in_specs.append(pl.BlockSpec((None, rows // n_blocks, cols),
                                     lambda i, lead=lead, last=n_blocks - 1: (lead, jnp.minimum(i, last), 0)))
        out_specs.append(pl.BlockSpec((rows // n_blocks, cols), lambda i, last=n_blocks - 1: (jnp.minimum(i, last), 0)))
        out_shapes.append(jax.ShapeDtypeStruct((rows, cols), BF16))
    return in_specs, out_specs, out_shapes


def _cast_blocks(src_refs, dst_refs):
    for src, dst in zip(src_refs, dst_refs):
        dst[...] = src[...].astype(BF16)


def _params(*sem):
    return pltpu.CompilerParams(dimension_semantics=sem, vmem_limit_bytes=VMEM_LIMIT)


assert DILATION_GROUPS[0][1] == 1

ROW_STRIDE = 4


def _stride_passes(r):
    assert r in (1, ROW_STRIDE, ROW_STRIDE ** 2)
    return {1: 1, ROW_STRIDE: 1, ROW_STRIDE ** 2: 2}[r]


FFN_TM = 1024
FFN_SUB = 512
FFN_CHUNKS = ((0, 5 * MXU_DIM), (5 * MXU_DIM, 6 * MXU_DIM))


def _ffn_kernel(n_inputs, n_cast, *refs):
    x_refs, (g_ref, win_ref, wout_ref) = refs[:n_inputs], refs[n_inputs:n_inputs + 3]
    refs = refs[n_inputs + 3:]
    next_f32, o_ref, next_bf16 = refs[:n_cast], refs[n_cast], refs[n_cast + 1:]
    _cast_blocks(next_f32, next_bf16)

    for t in range(FFN_TM // FFN_SUB):
        rows = pl.ds(t * FFN_SUB, FFN_SUB)
        x = x_refs[0][rows, :]
        for extra_ref in x_refs[1:]:
            x = x + extra_ref[rows, :]
        xn = _rmsnorm_bf16(x, g_ref[...])
        acc = None
        for start, size in FFN_CHUNKS:
            gate = _dot(xn, win_ref[:, pl.ds(start, size)])
            up = _dot(xn, win_ref[:, pl.ds(D_FF + start, size)])
            a = (gate * jax.nn.sigmoid(gate) * up).astype(BF16)
            part = _dot(a, wout_ref[pl.ds(start, size), :])
            acc = part if acc is None else acc + part
        o_ref[rows, :] = x + 0.5 * acc


def _ffn(hs, layer, g, weights, cast_items=()):
    n_tok = hs[0].shape[0]
    n_steps = n_tok // FFN_TM
    tile = pl.BlockSpec((FFN_TM, D_MODEL), lambda i: (i, 0))
    cast_in, cast_out, cast_shapes = _cast_specs(n_steps, cast_items)
    outs = pl.pallas_call(
        functools.partial(_ffn_kernel, len(hs), len(cast_items)),
        grid=(n_steps,),
        in_specs=[tile] * len(hs) + [_layer((1, D_MODEL), layer), _resident((D_MODEL, 2 * D_FF)),
                                     _resident((D_FF, D_MODEL))] + cast_in,
        out_specs=[tile] + cast_out,
        out_shape=[jax.ShapeDtypeStruct((n_tok, D_MODEL), F32)] + cast_shapes,
        compiler_params=_params("arbitrary"),
        name="ffn",
    )(*hs, g, *weights, *(array for array, _ in cast_items))
    return outs[0], tuple(outs[1:])


CONV_TM = 1024
CONV_SUB = 512
CARRY = 8


def _conv_kernel(tiles_per_seq, n_cast, x_ref, g_ref, win_ref, cw_ref, wout_ref, *refs):
    next_f32, o_ref, next_bf16, vp_ref = refs[:n_cast], refs[n_cast], refs[n_cast + 1:2 * n_cast + 1], refs[-1]
    _cast_blocks(next_f32, next_bf16)

    @pl.when(pl.program_id(0) % tiles_per_seq == 0)
    def _():
        vp_ref[pl.ds(0, CARRY), :] = jnp.zeros((CARRY, D_MODEL), F32)

    for t in range(CONV_TM // CONV_SUB):
        rows = pl.ds(t * CONV_SUB, CONV_SUB)
        x = x_ref[rows, :]
        xn = _rmsnorm_bf16(x, g_ref[...])
        b_gate = _dot(xn, win_ref[:, pl.ds(0, D_MODEL)])
        c_gate = _dot(xn, win_ref[:, pl.ds(D_MODEL, D_MODEL)])
        u = _dot(xn, win_ref[:, pl.ds(2 * D_MODEL, D_MODEL)])
        v = c_gate * u
        vp_ref[pl.ds(CARRY, CONV_SUB), :] = v
        conv = cw_ref[pl.ds(0, 1), :] * v
        for lag in range(1, CONV_WIDTH):
            conv = conv + cw_ref[pl.ds(lag, 1), :] * vp_ref[pl.ds(CARRY - lag, CONV_SUB), :]
        z = (b_gate * conv).astype(BF16)
        o_ref[rows, :] = x + _dot(z, wout_ref[...])
        vp_ref[pl.ds(0, CARRY), :] = vp_ref[pl.ds(CONV_SUB, CARRY), :]


def _conv_mixer(h, seq, layer, mixer, g, weights, w_conv, cast_items):
    n_tok = h.shape[0]
    cast_in, cast_out, cast_shapes = _cast_specs(n_tok // CONV_TM, cast_items)
    outs = pl.pallas_call(
        functools.partial(_conv_kernel, seq // CONV_TM, len(cast_items)),
        grid=(n_tok // CONV_TM,),
        in_specs=[
            pl.BlockSpec((CONV_TM, D_MODEL), lambda i: (i, 0)),
            _layer((1, D_MODEL), layer),
            _resident((D_MODEL, 3 * D_MODEL)),
            _layer((CONV_WIDTH, D_MODEL), mixer),
            _resident((D_MODEL, D_MODEL)),
        ] + cast_in,
        out_specs=[pl.BlockSpec((CONV_TM, D_MODEL), lambda i: (i, 0))] + cast_out,
        out_shape=[jax.ShapeDtypeStruct((n_tok, D_MODEL), F32)] + cast_shapes,
        scratch_shapes=[pltpu.VMEM((CARRY + CONV_SUB, D_MODEL), F32)],
        compiler_params=_params("arbitrary"),
        name="conv_mixer",
    )(h, g, weights[0], w_conv, weights[1], *(array for array, _ in cast_items))
    return outs[0], tuple(outs[1:])


QKV_TM = 1024
QKV_SUB = 512
QKV_TN = 2 * MXU_DIM


def _qkv_kernel(group, x_ref, g_ref, w_ref, gain_ref, bd_ref, perm_ref, o_ref, xs_ref):
    r = DILATION_GROUPS[group][1]
    rows_per_strand = UNIT // r
    for t in range(QKV_TM // QKV_SUB):
        rows = pl.ds(t * QKV_SUB, QKV_SUB)
        x = x_ref[rows, :]
        xn = x * lax.rsqrt(jnp.mean(x * x, axis=-1, keepdims=True) + RMS_EPS) * g_ref[...]
        if r == ROW_STRIDE:
            slabs = range(D_MODEL // LANES)
            for p in slabs:
                xs_ref[t, p] = xn[:, p * LANES:(p + 1) * LANES]
            xn = jnp.concatenate(
                [jnp.concatenate([xs_ref[t, p, pl.ds(u * UNIT + s, rows_per_strand, stride=r), :] for p in slabs],
                                 axis=1)
                 for u in range(QKV_SUB // UNIT) for s in range(r)], axis=0)
        xn = xn.astype(BF16)
        if r > ROW_STRIDE:
            xn = jnp.concatenate(
                [_dot(perm_ref[...], xn[u * UNIT:(u + 1) * UNIT]).astype(BF16) for u in range(QKV_SUB // UNIT)],
                axis=0)
        for c in range(3 * D_MODEL // QKV_TN):
            cols = pl.ds(c * QKV_TN, QKV_TN)
            y = _dot(xn, w_ref[:, cols])
            if c * QKV_TN >= 2 * D_MODEL:
                res = y.astype(BF16)
            else:
                y2 = (y * y).astype(BF16)
                ms = jnp.concatenate(
                    [_dot(y2[:, s * MXU_DIM:(s + 1) * MXU_DIM], bd_ref[...]) for s in range(QKV_TN // MXU_DIM)],
                    axis=1)
                res = (y * lax.rsqrt(ms + RMS_EPS) * gain_ref[:, cols]).astype(BF16)
            for s in range(QKV_TN // LANES):
                o_ref[c * (QKV_TN // LANES) + s, rows, :] = res[:, s * LANES:(s + 1) * LANES]


def _qkv_proj(h, layer, group, g, w_qkv, q_gain, k_gain):
    n_tok = h.shape[0]
    gains = jnp.concatenate(
        [jnp.tile(q_gain[group], N_HEADS) * (HEAD_DIM ** -0.5 * LOG2E), jnp.tile(k_gain[group], N_HEADS),
         jnp.ones((D_MODEL,), F32)]).reshape(1, 3 * D_MODEL)
    seg = np.arange(MXU_DIM) // HEAD_DIM
    block_diag = jnp.asarray((seg[:, None] == seg[None, :]) / HEAD_DIM, BF16)
    r = DILATION_GROUPS[group][1]
    tok = np.arange(UNIT)
    perm = np.zeros((UNIT, UNIT), np.float32)
    perm[(tok % r) * (UNIT // r) + tok // r, tok] = 1.0
    return pl.pallas_call(
        functools.partial(_qkv_kernel, group),
        grid=(n_tok // QKV_TM,),
        in_specs=[
            pl.BlockSpec((QKV_TM, D_MODEL), lambda i: (i, 0)),
            _layer((1, D_MODEL), layer),
            pl.BlockSpec((D_MODEL, 3 * D_MODEL), lambda i: (0, group), pipeline_mode=pl.Buffered(1)),
            _resident((1, 3 * D_MODEL)),
            _resident((MXU_DIM, MXU_DIM)),
            _resident((UNIT, UNIT)),
        ],
        out_specs=pl.BlockSpec((3 * N_PAIRS, QKV_TM, LANES), lambda i: (0, i, 0)),
        out_shape=jax.ShapeDtypeStruct((3 * N_PAIRS, n_tok, LANES), BF16),
        scratch_shapes=[pltpu.VMEM((QKV_TM // QKV_SUB, D_MODEL // LANES, QKV_SUB, LANES), F32)],
        compiler_params=_params("arbitrary"),
        name=f"qkv_proj_g{group}",
    )(h, g, w_qkv, gains, block_diag, jnp.asarray(perm, BF16))


def _t5_bucket_np(dist):
    nf = np.maximum(dist, 1).astype(np.float32)
    large = MAX_EXACT + (np.log(nf / np.float32(MAX_EXACT)) / np.float32(math.log(MAX_DISTANCE / MAX_EXACT))
                         * np.float32(N_BUCKETS - MAX_EXACT)).astype(np.int32)
    large = np.minimum(large, N_BUCKETS - 1)
    return np.where(dist < MAX_EXACT, dist, large)


def _bucket_maps():
    i = (np.arange(2 * BLOCK) % BLOCK)[:, None]
    j = np.arange(2 * BLOCK)[None, :]
    step = BLOCK + i - j
    maps = []
    for window, dilation in DILATION_GROUPS:
        n_steps = window // dilation
        in_band = (step >= 0) & (step <= n_steps)
        bucket = _t5_bucket_np(np.clip(step, 0, n_steps) * dilation)
        maps.append(np.where(in_band, bucket, -1).astype(np.int32))
    return np.stack(maps)


def _bias_kernel(buckets, n_cast, rel_ref, bm_ref, *refs):
    next_f32, o_ref, next_bf16 = refs[:n_cast], refs[n_cast], refs[n_cast + 1:]
    _cast_blocks(next_f32, next_bf16)
    p = pl.program_id(0)
    shape = bm_ref.shape[1:]
    top = lax.broadcasted_iota(jnp.int32, shape, 0) < BLOCK
    col = lax.broadcasted_iota(jnp.int32, shape, 1)
    for g in range(N_GROUPS):
        bm = bm_ref[g]
        acc = jnp.zeros(shape, F32)
        for b in buckets[g]:
            val = jnp.where(top, rel_ref[b, g * N_HEADS + 2 * p], rel_ref[b, g * N_HEADS + 2 * p + 1])
            acc = jnp.where(bm == b, val, acc)
        rest = jnp.where(bm < 0, NEG_INF, acc * LOG2E)
        o_ref[g, 1] = rest
        o_ref[g, 0] = jnp.where(col < BLOCK, NEG_INF, rest)


def _bias_tiles(rel_bias, cast_items):
    tile = (2 * BLOCK, 2 * BLOCK)
    maps = _bucket_maps()
    buckets = tuple(tuple(int(b) for b in np.unique(m[m >= 0])) for m in maps)
    cast_in, cast_out, cast_shapes = _cast_specs(N_PAIRS, cast_items)
    outs = pl.pallas_call(
        functools.partial(_bias_kernel, buckets, len(cast_items)),
        grid=(N_PAIRS,),
        in_specs=[pl.BlockSpec(memory_space=pltpu.SMEM), _resident((N_GROUPS,) + tile)] + cast_in,
        out_specs=[pl.BlockSpec((N_GROUPS, 2, None) + tile, lambda p: (0, 0, p, 0, 0))] + cast_out,
        out_shape=[jax.ShapeDtypeStruct((N_GROUPS, 2, N_PAIRS) + tile, F32)] + cast_shapes,
        compiler_params=_params("arbitrary"),
        name="bias_tiles",
    )(rel_bias, jnp.asarray(maps), *(array for array, _ in cast_items))
    return outs[0], tuple(outs[1:])


ATTN_BLOCKS = 16


def _attn_kernel(nb, q_ref, kp_ref, kc_ref, vp_ref, vc_ref, bias_ref, o_ref, lse_ref):
    upb = q_ref.shape[1] // nb
    n_strands, rows_per_unit = q_ref.shape[2], q_ref.shape[3]
    not_first = jnp.minimum(pl.program_id(2), 1)
    lane = lax.broadcasted_iota(jnp.int32, (BLOCK, LANES), 1)
    lo = lane < HEAD_DIM
    ones = jnp.ones((2 * BLOCK, LANES), BF16)

    def keys_of(prev_ref, cur_ref, blk, p, j):
        if blk == 0:
            return jnp.concatenate([prev_ref[p, :, j].reshape(BLOCK, LANES),
                                    cur_ref[p, pl.ds(0, upb), j].reshape(BLOCK, LANES)], axis=0)
        return cur_ref[p, pl.ds((blk - 1) * upb, 2 * upb), j].reshape(2 * BLOCK, LANES)

    for j in range(n_strands):
        for blk in range(nb):
            units = pl.ds(blk * upb, upb)
            m_all = jnp.zeros((BLOCK, LANES), F32)
            s_all = jnp.ones((BLOCK, LANES), F32)
            for p in range(N_PAIRS):
                q = q_ref[p, units, j].reshape(BLOCK, LANES)
                zero = jnp.zeros_like(q)
                q2 = jnp.concatenate([jnp.where(lo, q, zero), jnp.where(lo, zero, q)], axis=0)
                k = keys_of(kp_ref, kc_ref, blk, p, j)
                logits = lax.dot_general(q2, k, (((1,), (1,)), ((), ())), preferred_element_type=F32)
                logits = logits + (bias_ref[not_first, p] if blk == 0 else bias_ref[1, p])
                m = jnp.max(logits, axis=-1, keepdims=True)
                prob = jnp.exp2(logits - m).astype(BF16)
                acc = _dot(prob, jnp.concatenate([keys_of(vp_ref, vc_ref, blk, p, j), ones], axis=1))
                o = jnp.where(lo, acc[:BLOCK, :LANES], acc[BLOCK:, :LANES])
                s = jnp.where(lo, acc[:BLOCK, LANES:], acc[BLOCK:, LANES:])
                mm = jnp.where(lo, m[:BLOCK], m[BLOCK:])
                o_ref[p, units, j] = (o / s).astype(BF16).reshape(upb, rows_per_unit, LANES)
                keep = (lane == 2 * p) | (lane == HEAD_DIM + 2 * p + 1)
                m_all = jnp.where(keep, mm, m_all)
                s_all = jnp.where(keep, s, s_all)
            lse_ref[units, j] = (m_all * LN2 + jnp.log(s_all)).reshape(upb, rows_per_unit, LANES)


def _attn_group(qkv, bias, g, batch, seq):
    r = DILATION_GROUPS[g][1]
    rows_per_unit = min(UNIT // r, BLOCK)
    n_units = seq // (r * rows_per_unit)
    upb = BLOCK // rows_per_unit
    nb = min(ATTN_BLOCKS, seq // (r * BLOCK))
    n_strands = ATTN_BLOCKS // nb
    view = qkv.reshape(3 * N_PAIRS, batch, n_units, r, rows_per_unit, LANES)

    def cur(which):
        return lambda b, s, c: (which, b, c, s, 0, 0)

    def prev(which):
        return lambda b, s, c: (which, b, jnp.maximum(c * nb - 1, 0), s, 0, 0)

    cur_blk = (N_PAIRS, None, nb * upb, n_strands, rows_per_unit, LANES)
    prev_blk = (N_PAIRS, None, upb, n_strands, rows_per_unit, LANES)
    o, lse = pl.pallas_call(
        functools.partial(_attn_kernel, nb),
        grid=(batch, r // n_strands, seq // (r * nb * BLOCK)),
        in_specs=[
            pl.BlockSpec(cur_blk, cur(0)),
            pl.BlockSpec(prev_blk, prev(1)),
            pl.BlockSpec(cur_blk, cur(1)),
            pl.BlockSpec(prev_blk, prev(2)),
            pl.BlockSpec(cur_blk, cur(2)),
            pl.BlockSpec((None, 2, N_PAIRS, 2 * BLOCK, 2 * BLOCK), lambda b, s, c: (g, 0, 0, 0, 0)),
        ],
        out_specs=[
            pl.BlockSpec(cur_blk, lambda b, s, c: (0, b, c, s, 0, 0)),
            pl.BlockSpec((None, nb * upb, n_strands, rows_per_unit, LANES), lambda b, s, c: (b, c, s, 0, 0)),
        ],
        out_shape=[
            jax.ShapeDtypeStruct((N_PAIRS, batch, n_units, r, rows_per_unit, LANES), BF16),
            jax.ShapeDtypeStruct((batch, n_units, r, rows_per_unit, LANES), F32),
        ],
        compiler_params=_params("arbitrary", "arbitrary", "arbitrary"),
        name=f"attn_r{r}",
    )(view, view, view, view, view, bias)
    return o.reshape(N_PAIRS, batch * seq, LANES), lse.reshape(batch * seq, LANES)


MERGE_TM = 1024
MERGE_SUB = 512


def _merge_kernel(o0_ref, o1_ref, o2_ref, l0_ref, l1_ref, l2_ref, e_ref, w_ref, out_ref,
                  tok_ref, ltok_ref, tmp_ref, ltmp_ref, merged_ref):
    o_refs = (o0_ref, o1_ref, o2_ref)
    l_refs = (l0_ref, l1_ref, l2_ref)
    for t in range(MERGE_TM // MERGE_SUB):
        first = t * MERGE_SUB
        rows = pl.ds(first, MERGE_SUB)
        lses = [l0_ref[rows, :]]
        for g in range(1, N_GROUPS):
            r = DILATION_GROUPS[g][1]
            rows_per_strand = UNIT // r
            for u in range(MERGE_SUB // UNIT):
                unit = first + u * UNIT
                if r == ROW_STRIDE:
                    for s in range(r):
                        src_rows = pl.ds(unit + s * rows_per_strand, rows_per_strand)
                        dst_rows = pl.ds(unit + s, rows_per_strand, stride=r)
                        ltok_ref[g - 1, dst_rows, :] = l_refs[g][src_rows, :]
                        for p in range(N_PAIRS):
                            tok_ref[g - 1, p, dst_rows, :] = o_refs[g][p, src_rows, :].astype(F32)
                else:
                    assert _stride_passes(r) == 2
                    quarter = UNIT // ROW_STRIDE
                    for a in range(ROW_STRIDE):
                        for b in range(ROW_STRIDE):
                            src_rows = pl.ds(unit + (ROW_STRIDE * a + b) * rows_per_strand, rows_per_strand)
                            dst_rows = pl.ds(unit + b * quarter + a, rows_per_strand, stride=ROW_STRIDE)
                            ltmp_ref[dst_rows, :] = l_refs[g][src_rows, :]
                            for p in range(N_PAIRS):
                                tmp_ref[p, dst_rows, :] = o_refs[g][p, src_rows, :].astype(F32)
                    for b in range(ROW_STRIDE):
                        src_rows = pl.ds(unit + b * quarter, quarter)
                        dst_rows = pl.ds(unit + b, quarter, stride=ROW_STRIDE)
                        ltok_ref[g - 1, dst_rows, :] = ltmp_ref[src_rows, :]
                        for p in range(N_PAIRS):
                            tok_ref[g - 1, p, dst_rows, :] = tmp_ref[p, src_rows, :]
            lses.append(ltok_ref[g - 1, rows, :])
        m = jnp.maximum(jnp.maximum(lses[0], lses[1]), lses[2])
        es = [jnp.exp(l - m) for l in lses]
        den = es[0] + es[1] + es[2]
        w_cat = []
        for g in range(1, N_GROUPS):
            w = es[g] / den
            w_hi = w.astype(BF16)
            w_cat.append(jnp.concatenate([w_hi, (w - w_hi.astype(F32)).astype(BF16)], axis=1))
        for c in range(D_MODEL // MXU_DIM):
            pairs = range(c * MXU_DIM // LANES, (c + 1) * MXU_DIM // LANES)
            base = jnp.concatenate([o0_ref[p, rows, :] for p in pairs], axis=1).astype(F32)
            acc = base
            for g in range(1, N_GROUPS):
                w_full = _dot(w_cat[g - 1], e_ref[:, pl.ds(c * MXU_DIM, MXU_DIM)])
                acc = acc + w_full * (jnp.concatenate([tok_ref[g - 1, p, rows, :] for p in pairs], axis=1) - base)
            merged_ref[rows, pl.ds(c * MXU_DIM, MXU_DIM)] = acc.astype(BF16)
        out_ref[rows, :] = _dot(merged_ref[rows, :], w_ref[...])


def _merge_proj(outs, lses, w_out):
    n_tok = lses[0].shape[0]
    expand = np.zeros((2 * LANES, D_MODEL), np.float32)
    for head in range(N_HEADS):
        src = head if head % 2 == 0 else HEAD_DIM + head
        expand[src, head * HEAD_DIM:(head + 1) * HEAD_DIM] = 1.0
    expand[LANES:] = expand[:LANES]
    tok = lambda width: pl.BlockSpec((MERGE_TM, width), lambda i: (i, 0))
    slab = pl.BlockSpec((N_PAIRS, MERGE_TM, LANES), lambda i: (0, i, 0))
    return pl.pallas_call(
        _merge_kernel,
        grid=(n_tok // MERGE_TM,),
        in_specs=[slab] * N_GROUPS + [tok(LANES)] * N_GROUPS
        + [_resident((2 * LANES, D_MODEL)), _resident((D_MODEL, D_MODEL))],
        out_specs=tok(D_MODEL),
        out_shape=jax.ShapeDtypeStruct((n_tok, D_MODEL), F32),
        scratch_shapes=[
            pltpu.VMEM((N_GROUPS - 1, N_PAIRS, MERGE_TM, LANES), F32),
            pltpu.VMEM((N_GROUPS - 1, MERGE_TM, LANES), F32),
            pltpu.VMEM((N_PAIRS, MERGE_TM, LANES), F32),
            pltpu.VMEM((MERGE_TM, LANES), F32),
            pltpu.VMEM((MERGE_TM, D_MODEL), BF16),
        ],
        compiler_params=_params("arbitrary"),
        name="merge_proj",
    )(*outs, *lses, jnp.asarray(expand, BF16), w_out)


def kernel(x, norm_ffn1, ffn1_w_in, ffn1_w_out, norm_mix, conv_w_in, conv_w, conv_w_out, attn_w_qkv,
           attn_q_gain, attn_k_gain, attn_w_out, rel_bias, norm_ffn2, ffn2_w_in, ffn2_w_out):
    batch, seq, d_model = x.shape
    assert d_model == D_MODEL and seq % max(QKV_TM, CONV_TM, DILATION_GROUPS[-1][1] * BLOCK) == 0
    depth = norm_ffn1.shape[0]
    h = x.reshape(batch * seq, D_MODEL)
    bias, ffn1_weights = _bias_tiles(rel_bias, [(ffn1_w_in, 0), (ffn1_w_out, 0)])
    gains = [a.reshape(depth, 1, D_MODEL) for a in (norm_ffn1, norm_mix, norm_ffn2)]
    for i in range(depth):
        ffn2_items = [(ffn2_w_in, i), (ffn2_w_out, i)]
        ffn1_items = [(ffn1_w_in, i + 1), (ffn1_w_out, i + 1)] if i + 1 < depth else []
        j = i // 2
        if i % 2 == 0:
            h, mixer_weights = _ffn([h], i, gains[0], ffn1_weights, [(conv_w_in, j), (conv_w_out, j)])
            h, ffn2_weights = _conv_mixer(h, seq, i, j, gains[1], mixer_weights, conv_w, ffn2_items)
            mixed = [h]
        else:
            h, cast = _ffn([h], i, gains[0], ffn1_weights, [(attn_w_qkv, j), (attn_w_out, j)] + ffn2_items)
            (w_qkv, w_out), ffn2_weights = cast[:2], cast[2:]
            parts = [_attn_group(_qkv_proj(h, i, g, gains[1], w_qkv, attn_q_gain[j], attn_k_gain[j]),
                                 bias, g, batch, seq) for g in range(N_GROUPS)]
            mixed = [h, _merge_proj([o for o, _ in parts], [l for _, l in parts], w_out)]
        h, ffn1_weights = _ffn(mixed, i, gains[2], ffn2_weights, ffn1_items)
    return h.reshape(batch, seq, D_MODEL)
```

```python
import functools
import math

import numpy as np
import jax
import jax.numpy as jnp
from jax import lax
from jax.experimental import pallas as pl
from jax.experimental.pallas import tpu as pltpu

D_MODEL = 1024
D_FF = 2816
CONV_WIDTH = 3
HEAD_DIM = 64
N_HEADS = D_MODEL // HEAD_DIM
N_PAIRS = N_HEADS // 2
DILATION_GROUPS = ((128, 1), (512, 4), (2048, 16))
N_GROUPS = len(DILATION_GROUPS)
BLOCK = 128
N_BUCKETS = 32
MAX_EXACT = N_BUCKETS // 2
MAX_DISTANCE = 2048
RMS_EPS = 1e-6
NEG_INF = -1e30
LOG2E = math.log2(math.e)
LN2 = math.log(2.0)

LANES = 128
BF16_TILE_ROWS = 16
MXU_DIM = 256
UNIT = MXU_DIM
VMEM_LIMIT = 56 * 1024 * 1024

F32 = jnp.float32
BF16 = jnp.bfloat16


def _rmsnorm_bf16(x, g):
    ms = jnp.mean(x * x, axis=-1, keepdims=True)
    return (x * lax.rsqrt(ms + RMS_EPS) * g).astype(BF16)


def _dot(a, b):
    return jnp.dot(a, b, preferred_element_type=F32)


def _resident(shape):
    return pl.BlockSpec(shape, lambda *_: (0,) * len(shape), pipeline_mode=pl.Buffered(1))


def _layer(shape, layer):
    return pl.BlockSpec((None,) + shape, lambda *_: (layer,) + (0,) * len(shape), pipeline_mode=pl.Buffered(1))


def _cast_specs(n_steps, items):
    in_specs, out_specs, out_shapes = [], [], []
    for array, lead in items:
        rows, cols = array.shape[1:]
        n_blocks = n_steps if rows % (BF16_TILE_ROWS * n_steps) == 0 else n_steps // 2
        assert rows % (BF16_TILE_ROWS * n_blocks) == 0
        in_specs.append(pl.BlockSpec((None, rows // n_blocks, cols),
                                     lambda i, lead=lead, last=n_blocks - 1: (lead, jnp.minimum(i, last), 0)))
        out_specs.append(pl.BlockSpec((rows // n_blocks, cols), lambda i, last=n_blocks - 1: (jnp.minimum(i, last), 0)))
        out_shapes.append(jax.ShapeDtypeStruct((rows, cols), BF16))
    return in_specs, out_specs, out_shapes


def _cast_blocks(src_refs, dst_refs):
    for src, dst in zip(src_refs, dst_refs):
        dst[...] = src[...].astype(BF16)


def _params(*sem):
    return pltpu.CompilerParams(dimension_semantics=sem, vmem_limit_bytes=VMEM_LIMIT)


assert DILATION_GROUPS[0][1] == 1

ROW_STRIDE = 4


def _stride_passes(r):
    assert r in (1, ROW_STRIDE, ROW_STRIDE ** 2)
    return {1: 1, ROW_STRIDE: 1, ROW_STRIDE ** 2: 2}[r]


FFN_TM = 1024
FFN_SUB = 512
FFN_CHUNKS = ((0, 5 * MXU_DIM), (5 * MXU_DIM, 6 * MXU_DIM))


def _ffn_kernel(n_inputs, n_cast, *refs):
    x_refs, (g_ref, win_ref, wout_ref) = refs[:n_inputs], refs[n_inputs:n_inputs + 3]
    refs = refs[n_inputs + 3:]
    next_f32, o_ref, next_bf16 = refs[:n_cast], refs[n_cast], refs[n_cast + 1:]
    _cast_blocks(next_f32, next_bf16)

    for t in range(FFN_TM // FFN_SUB):
        rows = pl.ds(t * FFN_SUB, FFN_SUB)
        x = x_refs[0][rows, :]
        for extra_ref in x_refs[1:]:
            x = x + extra_ref[rows, :]
        xn = _rmsnorm_bf16(x, g_ref[...])
        acc = None
        for start, size in FFN_CHUNKS:
            gate = _dot(xn, win_ref[:, pl.ds(start, size)])
            up = _dot(xn, win_ref[:, pl.ds(D_FF + start, size)])
            a = (gate * jax.nn.sigmoid(gate) * up).astype(BF16)
            part = _dot(a, wout_ref[pl.ds(start, size), :])
            acc = part if acc is None else acc + part
        o_ref[rows, :] = x + 0.5 * acc


def _ffn(hs, layer, g, weights, cast_items=()):
    n_tok = hs[0].shape[0]
    n_steps = n_tok // FFN_TM
    tile = pl.BlockSpec((FFN_TM, D_MODEL), lambda i: (i, 0))
    cast_in, cast_out, cast_shapes = _cast_specs(n_steps, cast_items)
    outs = pl.pallas_call(
        functools.partial(_ffn_kernel, len(hs), len(cast_items)),
        grid=(n_steps,),
        in_specs=[tile] * len(hs) + [_layer((1, D_MODEL), layer), _resident((D_MODEL, 2 * D_FF)),
                                     _resident((D_FF, D_MODEL))] + cast_in,
        out_specs=[tile] + cast_out,
        out_shape=[jax.ShapeDtypeStruct((n_tok, D_MODEL), F32)] + cast_shapes,
        compiler_params=_params("arbitrary"),
        name="ffn",
    )(*hs, g, *weights, *(array for array, _ in cast_items))
    return outs[0], tuple(outs[1:])


CONV_TM = 1024
CONV_SUB = 512
CARRY = 8


def _conv_kernel(tiles_per_seq, n_cast, x_ref, g_ref, win_ref, cw_ref, wout_ref, *refs):
    next_f32, o_ref, next_bf16, vp_ref = refs[:n_cast], refs[n_cast], refs[n_cast + 1:2 * n_cast + 1], refs[-1]
    _cast_blocks(next_f32, next_bf16)

    @pl.when(pl.program_id(0) % tiles_per_seq == 0)
    def _():
        vp_ref[pl.ds(0, CARRY), :] = jnp.zeros((CARRY, D_MODEL), F32)

    for t in range(CONV_TM // CONV_SUB):
        rows = pl.ds(t * CONV_SUB, CONV_SUB)
        x = x_ref[rows, :]
        xn = _rmsnorm_bf16(x, g_ref[...])
        b_gate = _dot(xn, win_ref[:, pl.ds(0, D_MODEL)])
        c_gate = _dot(xn, win_ref[:, pl.ds(D_MODEL, D_MODEL)])
        u = _dot(xn, win_ref[:, pl.ds(2 * D_MODEL, D_MODEL)])
        v = c_gate * u
        vp_ref[pl.ds(CARRY, CONV_SUB), :] = v
        conv = cw_ref[pl.ds(0, 1), :] * v
        for lag in range(1, CONV_WIDTH):
            conv = conv + cw_ref[pl.ds(lag, 1), :] * vp_ref[pl.ds(CARRY - lag, CONV_SUB), :]
        z = (b_gate * conv).astype(BF16)
        o_ref[rows, :] = x + _dot(z, wout_ref[...])
        vp_ref[pl.ds(0, CARRY), :] = vp_ref[pl.ds(CONV_SUB, CARRY), :]


def _conv_mixer(h, seq, layer, mixer, g, weights, w_conv, cast_items):
    n_tok = h.shape[0]
    cast_in, cast_out, cast_shapes = _cast_specs(n_tok // CONV_TM, cast_items)
    outs = pl.pallas_call(
        functools.partial(_conv_kernel, seq // CONV_TM, len(cast_items)),
        grid=(n_tok // CONV_TM,),
        in_specs=[
            pl.BlockSpec((CONV_TM, D_MODEL), lambda i: (i, 0)),
            _layer((1, D_MODEL), layer),
            _resident((D_MODEL, 3 * D_MODEL)),
            _layer((CONV_WIDTH, D_MODEL), mixer),
            _resident((D_MODEL, D_MODEL)),
        ] + cast_in,
        out_specs=[pl.BlockSpec((CONV_TM, D_MODEL), lambda i: (i, 0))] + cast_out,
        out_shape=[jax.ShapeDtypeStruct((n_tok, D_MODEL), F32)] + cast_shapes,
        scratch_shapes=[pltpu.VMEM((CARRY + CONV_SUB, D_MODEL), F32)],
        compiler_params=_params("arbitrary"),
        name="conv_mixer",
    )(h, g, weights[0], w_conv, weights[1], *(array for array, _ in cast_items))
    return outs[0], tuple(outs[1:])


QKV_TM = 1024
QKV_SUB = 512
QKV_TN = 2 * MXU_DIM


def _qkv_kernel(group, x_ref, g_ref, w_ref, gain_ref, bd_ref, perm_ref, o_ref, xs_ref):
    r = DILATION_GROUPS[group][1]
    rows_per_strand = UNIT // r
    for t in range(QKV_TM // QKV_SUB):
        rows = pl.ds(t * QKV_SUB, QKV_SUB)
        x = x_ref[rows, :]
        xn = x * lax.rsqrt(jnp.mean(x * x, axis=-1, keepdims=True) + RMS_EPS) * g_ref[...]
        if r == ROW_STRIDE:
            slabs = range(D_MODEL // LANES)
            for p in slabs:
                xs_ref[t, p] = xn[:, p * LANES:(p + 1) * LANES]
            xn = jnp.concatenate(
                [jnp.concatenate([xs_ref[t, p, pl.ds(u * UNIT + s, rows_per_strand, stride=r), :] for p in slabs],
                                 axis=1)
                 for u in range(QKV_SUB // UNIT) for s in range(r)], axis=0)
        xn = xn.astype(BF16)
        if r > ROW_STRIDE:
            xn = jnp.concatenate(
                [_dot(perm_ref[...], xn[u * UNIT:(u + 1) * UNIT]).astype(BF16) for u in range(QKV_SUB // UNIT)],
                axis=0)
        for c in range(3 * D_MODEL // QKV_TN):
            cols = pl.ds(c * QKV_TN, QKV_TN)
            y = _dot(xn, w_ref[:, cols])
            if c * QKV_TN >= 2 * D_MODEL:
                res = y.astype(BF16)
            else:
                y2 = (y * y).astype(BF16)
                ms = jnp.concatenate(
                    [_dot(y2[:, s * MXU_DIM:(s + 1) * MXU_DIM], bd_ref[...]) for s in range(QKV_TN // MXU_DIM)],
                    axis=1)
                res = (y * lax.rsqrt(ms + RMS_EPS) * gain_ref[:, cols]).astype(BF16)
            for s in range(QKV_TN // LANES):
                o_ref[c * (QKV_TN // LANES) + s, rows, :] = res[:, s * LANES:(s + 1) * LANES]


def _qkv_proj(h, layer, group, g, w_qkv, q_gain, k_gain):
    n_tok = h.shape[0]
    gains = jnp.concatenate(
        [jnp.tile(q_gain[group], N_HEADS) * (HEAD_DIM ** -0.5 * LOG2E), jnp.tile(k_gain[group], N_HEADS),
         jnp.ones((D_MODEL,), F32)]).reshape(1, 3 * D_MODEL)
    seg = np.arange(MXU_DIM) // HEAD_DIM
    block_diag = jnp.asarray((seg[:, None] == seg[None, :]) / HEAD_DIM, BF16)
    r = DILATION_GROUPS[group][1]
    tok = np.arange(UNIT)
    perm = np.zeros((UNIT, UNIT), np.float32)
    perm[(tok % r) * (UNIT // r) + tok // r, tok] = 1.0
    return pl.pallas_call(
        functools.partial(_qkv_kernel, group),
        grid=(n_tok // QKV_TM,),
        in_specs=[
            pl.BlockSpec((QKV_TM, D_MODEL), lambda i: (i, 0)),
            _layer((1, D_MODEL), layer),
            pl.BlockSpec((D_MODEL, 3 * D_MODEL), lambda i: (0, group), pipeline_mode=pl.Buffered(1)),
            _resident((1, 3 * D_MODEL)),
            _resident((MXU_DIM, MXU_DIM)),
            _resident((UNIT, UNIT)),
        ],
        out_specs=pl.BlockSpec((3 * N_PAIRS, QKV_TM, LANES), lambda i: (0, i, 0)),
        out_shape=jax.ShapeDtypeStruct((3 * N_PAIRS, n_tok, LANES), BF16),
        scratch_shapes=[pltpu.VMEM((QKV_TM // QKV_SUB, D_MODEL // LANES, QKV_SUB, LANES), F32)],
        compiler_params=_params("arbitrary"),
        name=f"qkv_proj_g{group}",
    )(h, g, w_qkv, gains, block_diag, jnp.asarray(perm, BF16))


def _t5_bucket_np(dist):
    nf = np.maximum(dist, 1).astype(np.float32)
    large = MAX_EXACT + (np.log(nf / np.float32(MAX_EXACT)) / np.float32(math.log(MAX_DISTANCE / MAX_EXACT))
                         * np.float32(N_BUCKETS - MAX_EXACT)).astype(np.int32)
    large = np.minimum(large, N_BUCKETS - 1)
    return np.where(dist < MAX_EXACT, dist, large)


def _bucket_maps():
    i = (np.arange(2 * BLOCK) % BLOCK)[:, None]
    j = np.arange(2 * BLOCK)[None, :]
    step = BLOCK + i - j
    maps = []
    for window, dilation in DILATION_GROUPS:
        n_steps = window // dilation
        in_band = (step >= 0) & (step <= n_steps)
        bucket = _t5_bucket_np(np.clip(step, 0, n_steps) * dilation)
        maps.append(np.where(in_band, bucket, -1).astype(np.int32))
    return np.stack(maps)


def _bias_kernel(buckets, n_cast, rel_ref, bm_ref, *refs):
    next_f32, o_ref, next_bf16 = refs[:n_cast], refs[n_cast], refs[n_cast + 1:]
    _cast_blocks(next_f32, next_bf16)
    p = pl.program_id(0)
    shape = bm_ref.shape[1:]
    top = lax.broadcasted_iota(jnp.int32, shape, 0) < BLOCK
    col = lax.broadcasted_iota(jnp.int32, shape, 1)
    for g in range(N_GROUPS):
        bm = bm_ref[g]
        acc = jnp.zeros(shape, F32)
        for b in buckets[g]:
            val = jnp.where(top, rel_ref[b, g * N_HEADS + 2 * p], rel_ref[b, g * N_HEADS + 2 * p + 1])
            acc = jnp.where(bm == b, val, acc)
        rest = jnp.where(bm < 0, NEG_INF, acc * LOG2E)
        o_ref[g, 1] = rest
        o_ref[g, 0] = jnp.where(col < BLOCK, NEG_INF, rest)


def _bias_tiles(rel_bias, cast_items):
    tile = (2 * BLOCK, 2 * BLOCK)
    maps = _bucket_maps()
    buckets = tuple(tuple(int(b) for b in np.unique(m[m >= 0])) for m in maps)
    cast_in, cast_out, cast_shapes = _cast_specs(N_PAIRS, cast_items)
    outs = pl.pallas_call(
        functools.partial(_bias_kernel, buckets, len(cast_items)),
        grid=(N_PAIRS,),
        in_specs=[pl.BlockSpec(memory_space=pltpu.SMEM), _resident((N_GROUPS,) + tile)] + cast_in,
        out_specs=[pl.BlockSpec((N_GROUPS, 2, None) + tile, lambda p: (0, 0, p, 0, 0))] + cast_out,
        out_shape=[jax.ShapeDtypeStruct((N_GROUPS, 2, N_PAIRS) + tile, F32)] + cast_shapes,
        compiler_params=_params("arbitrary"),
        name="bias_tiles",
    )(rel_bias, jnp.asarray(maps), *(array for array, _ in cast_items))
    return outs[0], tuple(outs[1:])


ATTN_BLOCKS = 16


def _attn_kernel(nb, q_ref, kp_ref, kc_ref, vp_ref, vc_ref, bias_ref, o_ref, lse_ref):
    upb = q_ref.shape[1] // nb
    n_strands, rows_per_unit = q_ref.shape[2], q_ref.shape[3]
    not_first = jnp.minimum(pl.program_id(2), 1)
    lane = lax.broadcasted_iota(jnp.int32, (BLOCK, LANES), 1)
    lo = lane < HEAD_DIM
    ones = jnp.ones((2 * BLOCK, LANES), BF16)

    def keys_of(prev_ref, cur_ref, blk, p, j):
        if blk == 0:
            return jnp.concatenate([prev_ref[p, :, j].reshape(BLOCK, LANES),
                                    cur_ref[p, pl.ds(0, upb), j].reshape(BLOCK, LANES)], axis=0)
        return cur_ref[p, pl.ds((blk - 1) * upb, 2 * upb), j].reshape(2 * BLOCK, LANES)

    for j in range(n_strands):
        for blk in range(nb):
            units = pl.ds(blk * upb, upb)
            m_all = jnp.zeros((BLOCK, LANES), F32)
            s_all = jnp.ones((BLOCK, LANES), F32)
            for p in range(N_PAIRS):
                q = q_ref[p, units, j].reshape(BLOCK, LANES)
                zero = jnp.zeros_like(q)
                q2 = jnp.concatenate([jnp.where(lo, q, zero), jnp.where(lo, zero, q)], axis=0)
                k = keys_of(kp_ref, kc_ref, blk, p, j)
                logits = lax.dot_general(q2, k, (((1,), (1,)), ((), ())), preferred_element_type=F32)
                logits = logits + (bias_ref[not_first, p] if blk == 0 else bias_ref[1, p])
                m = jnp.max(logits, axis=-1, keepdims=True)
                prob = jnp.exp2(logits - m).astype(BF16)
                acc = _dot(prob, jnp.concatenate([keys_of(vp_ref, vc_ref, blk, p, j), ones], axis=1))
                o = jnp.where(lo, acc[:BLOCK, :LANES], acc[BLOCK:, :LANES])
                s = jnp.where(lo, acc[:BLOCK, LANES:], acc[BLOCK:, LANES:])
                mm = jnp.where(lo, m[:BLOCK], m[BLOCK:])
                o_ref[p, units, j] = (o / s).astype(BF16).reshape(upb, rows_per_unit, LANES)
                keep = (lane == 2 * p) | (lane == HEAD_DIM + 2 * p + 1)
                m_all = jnp.where(keep, mm, m_all)
                s_all = jnp.where(keep, s, s_all)
            lse_ref[units, j] = (m_all * LN2 + jnp.log(s_all)).reshape(upb, rows_per_unit, LANES)


def _attn_group(qkv, bias, g, batch, seq):
    r = DILATION_GROUPS[g][1]
    rows_per_unit = min(UNIT // r, BLOCK)
    n_units = seq // (r * rows_per_unit)
    upb = BLOCK // rows_per_unit
    nb = min(ATTN_BLOCKS, seq // (r * BLOCK))
    n_strands = ATTN_BLOCKS // nb
    view = qkv.reshape(3 * N_PAIRS, batch, n_units, r, rows_per_unit, LANES)

    def cur(which):
        return lambda b, s, c: (which, b, c, s, 0, 0)

    def prev(which):
        return lambda b, s, c: (which, b, jnp.maximum(c * nb - 1, 0), s, 0, 0)

    cur_blk = (N_PAIRS, None, nb * upb, n_strands, rows_per_unit, LANES)
    prev_blk = (N_PAIRS, None, upb, n_strands, rows_per_unit, LANES)
    o, lse = pl.pallas_call(
        functools.partial(_attn_kernel, nb),
        grid=(batch, r // n_strands, seq // (r * nb * BLOCK)),
        in_specs=[
            pl.BlockSpec(cur_blk, cur(0)),
            pl.BlockSpec(prev_blk, prev(1)),
            pl.BlockSpec(cur_blk, cur(1)),
            pl.BlockSpec(prev_blk, prev(2)),
            pl.BlockSpec(cur_blk, cur(2)),
            pl.BlockSpec((None, 2, N_PAIRS, 2 * BLOCK, 2 * BLOCK), lambda b, s, c: (g, 0, 0, 0, 0)),
        ],
        out_specs=[
            pl.BlockSpec(cur_blk, lambda b, s, c: (0, b, c, s, 0, 0)),
            pl.BlockSpec((None, nb * upb, n_strands, rows_per_unit, LANES), lambda b, s, c: (b, c, s, 0, 0)),
        ],
        out_shape=[
            jax.ShapeDtypeStruct((N_PAIRS, batch, n_units, r, rows_per_unit, LANES), BF16),
            jax.ShapeDtypeStruct((batch, n_units, r, rows_per_unit, LANES), F32),
        ],
        compiler_params=_params("arbitrary", "arbitrary", "arbitrary"),
        name=f"attn_r{r}",
    )(view, view, view, view, view, bias)
    return o.reshape(N_PAIRS, batch * seq, LANES), lse.reshape(batch * seq, LANES)


MERGE_TM = 1024
MERGE_SUB = 512
MERGE_RING = 3


def _merge_kernel(o0_hbm, o1_hbm, o2_hbm, l0_ref, l1_ref, l2_ref, e_ref, w_ref, out_ref,
                  tok_ref, ltok_ref, tmp_ref, ltmp_ref, merged_ref, ring_ref, sem):
    step, n_steps = pl.program_id(0), pl.num_programs(0)

    def fetch(s, g):
        rows = pl.ds(pl.multiple_of(s * MERGE_TM, MERGE_TM), MERGE_TM)
        return pltpu.make_async_copy((o0_hbm, o1_hbm, o2_hbm)[g].at[:, rows, :], ring_ref.at[s % MERGE_RING, g],
                                     sem.at[s % MERGE_RING, g])

    @pl.when(step == 0)
    def _():
        for s in range(MERGE_RING - 1):
            for g in range(N_GROUPS):
                fetch(s, g).start()

    @pl.when(step + MERGE_RING - 1 < n_steps)
    def _():
        for g in range(N_GROUPS):
            fetch(step + MERGE_RING - 1, g).start()

    for g in range(N_GROUPS):
        fetch(step, g).wait()
    o_refs = [ring_ref.at[step % MERGE_RING, g] for g in range(N_GROUPS)]
    l_refs = (l0_ref, l1_ref, l2_ref)
    for t in range(MERGE_TM // MERGE_SUB):
        first = t * MERGE_SUB
        rows = pl.ds(first, MERGE_SUB)
        lses = [l0_ref[rows, :]]
        for g in range(1, N_GROUPS):
            r = DILATION_GROUPS[g][1]
            rows_per_strand = UNIT // r
            for u in range(MERGE_SUB // UNIT):
                unit = first + u * UNIT
                if r == ROW_STRIDE:
                    for s in range(r):
                        src_rows = pl.ds(unit + s * rows_per_strand, rows_per_strand)
                        dst_rows = pl.ds(unit + s, rows_per_strand, stride=r)
                        ltok_ref[g - 1, dst_rows, :] = l_refs[g][src_rows, :]
                        for p in range(N_PAIRS):
                            tok_ref[g - 1, p, dst_rows, :] = o_refs[g][p, src_rows, :].astype(F32)
                else:
                    assert _stride_passes(r) == 2
                    quarter = UNIT // ROW_STRIDE
                    for a in range(ROW_STRIDE):
                        for b in range(ROW_STRIDE):
                            src_rows = pl.ds(unit + (ROW_STRIDE * a + b) * rows_per_strand, rows_per_strand)
                            dst_rows = pl.ds(unit + b * quarter + a, rows_per_strand, stride=ROW_STRIDE)
                            ltmp_ref[dst_rows, :] = l_refs[g][src_rows, :]
                            for p in range(N_PAIRS):
                                tmp_ref[p, dst_rows, :] = o_refs[g][p, src_rows, :].astype(F32)
                    for b in range(ROW_STRIDE):
                        src_rows = pl.ds(unit + b * quarter, quarter)
                        dst_rows = pl.ds(unit + b, quarter, stride=ROW_STRIDE)
                        ltok_ref[g - 1, dst_rows, :] = ltmp_ref[src_rows, :]
                        for p in range(N_PAIRS):
                            tok_ref[g - 1, p, dst_rows, :] = tmp_ref[p, src_rows, :]
            lses.append(ltok_ref[g - 1, rows, :])
        m = jnp.maximum(jnp.maximum(lses[0], lses[1]), lses[2])
        es = [jnp.exp(l - m) for l in lses]
        den = es[0] + es[1] + es[2]
        w_cat = []
        for g in range(1, N_GROUPS):
            w = es[g] / den
            w_hi = w.astype(BF16)
            w_cat.append(jnp.concatenate([w_hi, (w - w_hi.astype(F32)).astype(BF16)], axis=1))
        for c in range(D_MODEL // MXU_DIM):
            pairs = range(c * MXU_DIM // LANES, (c + 1) * MXU_DIM // LANES)
            base = jnp.concatenate([o_refs[0][p, rows, :] for p in pairs], axis=1).astype(F32)
            acc = base
            for g in range(1, N_GROUPS):
                w_full = _dot(w_cat[g - 1], e_ref[:, pl.ds(c * MXU_DIM, MXU_DIM)])
                acc = acc + w_full * (jnp.concatenate([tok_ref[g - 1, p, rows, :] for p in pairs], axis=1) - base)
            merged_ref[rows, pl.ds(c * MXU_DIM, MXU_DIM)] = acc.astype(BF16)
        out_ref[rows, :] = _dot(merged_ref[rows, :], w_ref[...])


def _merge_proj(outs, lses, w_out):
    n_tok = lses[0].shape[0]
    expand = np.zeros((2 * LANES, D_MODEL), np.float32)
    for head in range(N_HEADS):
        src = head if head % 2 == 0 else HEAD_DIM + head
        expand[src, head * HEAD_DIM:(head + 1) * HEAD_DIM] = 1.0
    expand[LANES:] = expand[:LANES]
    tok = lambda width: pl.BlockSpec((MERGE_TM, width), lambda i: (i, 0))
    return pl.pallas_call(
        _merge_kernel,
        grid=(n_tok // MERGE_TM,),
        in_specs=[pl.BlockSpec(memory_space=pl.ANY)] * N_GROUPS + [tok(LANES)] * N_GROUPS
        + [_resident((2 * LANES, D_MODEL)), _resident((D_MODEL, D_MODEL))],
        out_specs=tok(D_MODEL),
        out_shape=jax.ShapeDtypeStruct((n_tok, D_MODEL), F32),
        scratch_shapes=[
            pltpu.VMEM((N_GROUPS - 1, N_PAIRS, MERGE_TM, LANES), F32),
            pltpu.VMEM((N_GROUPS - 1, MERGE_TM, LANES), F32),
            pltpu.VMEM((N_PAIRS, MERGE_TM, LANES), F32),
            pltpu.VMEM((MERGE_TM, LANES), F32),
            pltpu.VMEM((MERGE_TM, D_MODEL), BF16),
            pltpu.VMEM((MERGE_RING, N_GROUPS, N_PAIRS, MERGE_TM, LANES), BF16),
            pltpu.SemaphoreType.DMA((MERGE_RING, N_GROUPS)),
        ],
        compiler_params=_params("arbitrary"),
        name="merge_proj",
    )(*outs, *lses, jnp.asarray(expand, BF16), w_out)


def kernel(x, norm_ffn1, ffn1_w_in, ffn1_w_out, norm_mix, conv_w_in, conv_w, conv_w_out, attn_w_qkv,
           attn_q_gain, attn_k_gain, attn_w_out, rel_bias, norm_ffn2, ffn2_w_in, ffn2_w_out):
    batch, seq, d_model = x.shape
    assert d_model == D_MODEL and seq % max(QKV_TM, CONV_TM, DILATION_GROUPS[-1][1] * BLOCK) == 0
    depth = norm_ffn1.shape[0]
    h = x.reshape(batch * seq, D_MODEL)
    bias, ffn1_weights = _bias_tiles(rel_bias, [(ffn1_w_in, 0), (ffn1_w_out, 0)])
    gains = [a.reshape(depth, 1, D_MODEL) for a in (norm_ffn1, norm_mix, norm_ffn2)]
    for i in range(depth):
        ffn2_items = [(ffn2_w_in, i), (ffn2_w_out, i)]
        ffn1_items = [(ffn1_w_in, i + 1), (ffn1_w_out, i + 1)] if i + 1 < depth else []
        j = i // 2
        if i % 2 == 0:
            h, mixer_weights = _ffn([h], i, gains[0], ffn1_weights, [(conv_w_in, j), (conv_w_out, j)])
            h, ffn2_weights = _conv_mixer(h, seq, i, j, gains[1], mixer_weights, conv_w, ffn2_items)
            mixed = [h]
        else:
            h, cast = _ffn([h], i, gains[0], ffn1_weights, [(attn_w_qkv, j), (attn_w_out, j)] + ffn2_items)
            (w_qkv, w_out), ffn2_weights = cast[:2], cast[2:]
            parts = [_attn_group(_qkv_proj(h, i, g, gains[1], w_qkv, attn_q_gain[j], attn_k_gain[j]),
                                 bias, g, batch, seq) for g in range(N_GROUPS)]
            mixed = [h, _merge_proj([o for o, _ in parts], [l for _, l in parts], w_out)]
        h, ffn1_weights = _ffn(mixed, i, gains[2], ffn2_weights, ffn1_items)
    return h.reshape(batch, seq, D_MODEL)
```
